```python
import math
import jax, jax.numpy as jnp
from jax import lax
import numpy as np

D_MODEL = 1024
BATCH = 8
SEQ = 2048
DEPTH = 1

D_FF = 2816
F_GROUPS = 4
F_GROUP_CH = 128
F_WIDTH = F_GROUPS * F_GROUP_CH
DA_HEADS = 4
DA_HEAD_DIM = 64
DA_V_DIM = 2 * DA_HEAD_DIM
QK_WIDTH = DA_HEADS * 2 * DA_HEAD_DIM
V_WIDTH = DA_HEADS * DA_V_DIM
GATE_WIDTH = 2 * D_MODEL
IN_WIDTH = F_WIDTH + 2 * QK_WIDTH + V_WIDTH + GATE_WIDTH
REL_BUCKETS = 32
REL_MAX_DIST = 128
Q_BLOCK = 128
NORM_EPS = 1e-6
SUBLN_EPS = 1e-5

kernel_name = "hybrid_fnet_diffattn_macaron_encoder"


def rms_norm(x, g, eps=NORM_EPS):
    xf = x.astype(jnp.float32)
    y = xf * lax.rsqrt(jnp.mean(xf * xf, axis=-1, keepdims=True) + eps)
    return (y * g.astype(jnp.float32)).astype(x.dtype)


def swiglu(x, wg, wu, wd):
    return (jax.nn.silu(x @ wg) * (x @ wu)) @ wd


def t5_bidirectional_bucket(rel):
    nb = REL_BUCKETS // 2
    max_exact = nb // 2
    ret = (rel > 0).astype(jnp.int32) * nb
    n = jnp.abs(rel)
    nf = jnp.maximum(n, 1).astype(jnp.float32)
    large = max_exact + (jnp.log(nf / max_exact) / math.log(REL_MAX_DIST / max_exact)
                         * (nb - max_exact)).astype(jnp.int32)
    large = jnp.minimum(large, nb - 1)
    return ret + jnp.where(n < max_exact, n, large)


def fourier_branch(u_f):
    B, S, _ = u_f.shape
    uf = u_f.reshape(B, S, F_GROUPS, F_GROUP_CH).astype(jnp.float32)
    y = jnp.fft.fft2(uf, axes=(1, 3), norm="ortho").real
    return y.astype(u_f.dtype).reshape(B, S, F_WIDTH)


def diff_attention(q, k, v, positions, rel_bias, lam):
    B, S = q.shape[0], q.shape[1]
    nb = S // Q_BLOCK
    scale = DA_HEAD_DIM ** -0.5
    q = jnp.transpose(q, (3, 0, 2, 1, 4))
    k = jnp.transpose(k, (3, 0, 2, 1, 4))
    vh = jnp.transpose(v, (0, 2, 1, 3))
    qb = q.reshape(2, B, DA_HEADS, nb, Q_BLOCK, DA_HEAD_DIM)
    qb = jnp.moveaxis(qb, 3, 0)
    pb = jnp.moveaxis(positions.reshape(B, nb, Q_BLOCK), 1, 0)

    def block(args):
        qblk, pblk = args
        rel = positions[:, None, :] - pblk[:, :, None]
        bias = rel_bias[t5_bidirectional_bucket(rel)]
        bias = jnp.transpose(bias, (0, 3, 1, 2)).astype(jnp.float32)
        s1 = jnp.einsum('bhqd,bhkd->bhqk', qblk[0], k[0]).astype(jnp.float32) * scale + bias
        s2 = jnp.einsum('bhqd,bhkd->bhqk', qblk[1], k[1]).astype(jnp.float32) * scale + bias
        attn = jax.nn.softmax(s1, axis=-1) - lam * jax.nn.softmax(s2, axis=-1)
        return jnp.einsum('bhqk,bhkv->bhqv', attn.astype(vh.dtype), vh)

    out = lax.map(block, (qb, pb))
    out = jnp.transpose(out, (1, 0, 3, 2, 4))
    return out.reshape(B, S, DA_HEADS, DA_V_DIM)


def hybrid_mixer(h, positions, rel_bias, w_in, lq1, lk1, lq2, lk2, subln_g,
                 w_fourier_out, w_attn_out, w_out, layer_idx):
    B, S, _ = h.shape
    proj = h @ w_in
    cuts = np.cumsum([F_WIDTH, QK_WIDTH, QK_WIDTH, V_WIDTH, D_MODEL]).tolist()
    u_f, q, k, v, g_a, g_b = jnp.split(proj, cuts, axis=-1)

    y_a = fourier_branch(u_f) @ w_fourier_out

    lambda_init = 0.8 - 0.6 * math.exp(-0.3 * layer_idx)
    lam = (jnp.exp(jnp.sum(lq1.astype(jnp.float32) * lk1.astype(jnp.float32)))
           - jnp.exp(jnp.sum(lq2.astype(jnp.float32) * lk2.astype(jnp.float32)))
           + lambda_init)
    q = q.reshape(B, S, DA_HEADS, 2, DA_HEAD_DIM)
    k = k.reshape(B, S, DA_HEADS, 2, DA_HEAD_DIM)
    v = v.reshape(B, S, DA_HEADS, DA_V_DIM)
    o = diff_attention(q, k, v, positions, rel_bias, lam)
    o = rms_norm(o, subln_g, SUBLN_EPS) * (1.0 - lambda_init)
    y_b = o.reshape(B, S, V_WIDTH) @ w_attn_out

    merged = jax.nn.sigmoid(g_a) * y_a + jax.nn.sigmoid(g_b) * y_b
    return merged @ w_out


def setup_inputs(seed: int = 0) -> dict:
    key = jax.random.key(seed)
    ks = jax.random.split(key, 24)
    L, D = DEPTH, D_MODEL
    nrm = lambda k, shape, fan_in: jax.random.normal(k, shape, jnp.float32) * fan_in ** -0.5
    gain = lambda k, shape: 1.0 + 0.02 * jax.random.normal(k, shape, jnp.float32)
    return {
        "x": jax.random.normal(ks[0], (BATCH, SEQ, D), jnp.float32),
        "positions": jnp.broadcast_to(jnp.arange(SEQ, dtype=jnp.int32), (BATCH, SEQ)),
        "rel_bias": 0.1 * jax.random.normal(ks[1], (REL_BUCKETS, DA_HEADS), jnp.float32),
        "ffn1_norm": gain(ks[2], (L, D)),
        "ffn1_wg": nrm(ks[3], (L, D, D_FF), D),
        "ffn1_wu": nrm(ks[4], (L, D, D_FF), D),
        "ffn1_wd": nrm(ks[5], (L, D_FF, D), D_FF),
        "mix_norm": gain(ks[6], (L, D)),
        "w_in": nrm(ks[7], (L, D, IN_WIDTH), D),
        "lambda_q1": 0.1 * jax.random.normal(ks[8], (L, DA_HEAD_DIM), jnp.float32),
        "lambda_k1": 0.1 * jax.random.normal(ks[9], (L, DA_HEAD_DIM), jnp.float32),
        "lambda_q2": 0.1 * jax.random.normal(ks[10], (L, DA_HEAD_DIM), jnp.float32),
        "lambda_k2": 0.1 * jax.random.normal(ks[11], (L, DA_HEAD_DIM), jnp.float32),
        "subln_g": gain(ks[12], (L, DA_V_DIM)),
        "w_fourier_out": nrm(ks[13], (L, F_WIDTH, D), F_WIDTH),
        "w_attn_out": nrm(ks[14], (L, V_WIDTH, D), V_WIDTH),
        "w_out": nrm(ks[15], (L, D, D), D),
        "ffn2_norm": gain(ks[16], (L, D)),
        "ffn2_wg": nrm(ks[17], (L, D, D_FF), D),
        "ffn2_wu": nrm(ks[18], (L, D, D_FF), D),
        "ffn2_wd": nrm(ks[19], (L, D_FF, D), D_FF),
        "final_norm": gain(ks[20], (D,)),
    }


def reference(x, positions, rel_bias, ffn1_norm, ffn1_wg, ffn1_wu, ffn1_wd,
              mix_norm, w_in, lambda_q1, lambda_k1, lambda_q2, lambda_k2, subln_g,
              w_fourier_out, w_attn_out, w_out, ffn2_norm, ffn2_wg, ffn2_wu,
              ffn2_wd, final_norm):
    for l in range(DEPTH):
        x = x + 0.5 * swiglu(rms_norm(x, ffn1_norm[l]), ffn1_wg[l], ffn1_wu[l], ffn1_wd[l])
        x = x + hybrid_mixer(rms_norm(x, mix_norm[l]), positions, rel_bias, w_in[l],
                             lambda_q1[l], lambda_k1[l], lambda_q2[l], lambda_k2[l],
                             subln_g[l], w_fourier_out[l], w_attn_out[l], w_out[l], l)
        x = x + 0.5 * swiglu(rms_norm(x, ffn2_norm[l]), ffn2_wg[l], ffn2_wu[l], ffn2_wd[l])
    return rms_norm(x, final_norm)
```

```python
import functools
import math

import numpy as np
import jax
import jax.numpy as jnp
from jax import lax
from jax.experimental import pallas as pl
from jax.experimental.pallas import tpu as pltpu

D_MODEL = 1024
BATCH = 8
SEQ = 2048
D_FF = 2816
F_GROUPS = 4
F_GROUP_CH = 128
F_WIDTH = F_GROUPS * F_GROUP_CH
DA_HEADS = 4
DA_HEAD_DIM = 64
DA_V_DIM = 2 * DA_HEAD_DIM
QK_WIDTH = DA_HEADS * 2 * DA_HEAD_DIM
V_WIDTH = DA_HEADS * DA_V_DIM
IN_WIDTH = F_WIDTH + 2 * QK_WIDTH + V_WIDTH + 2 * D_MODEL
REL_BUCKETS = 32
REL_MAX_DIST = 128
NORM_EPS = 1e-6
SUBLN_EPS = 1e-5
LAMBDA_INIT = 0.8 - 0.6 * math.exp(-0.3 * 0)

N_TOK = BATCH * SEQ
LANES = 128
VMEM_LIMIT = 56 * 1024 * 1024

FFN_TM = 512
PROJ_TM = 512
MERGE_TM = 512
ATT_TQ = 256

BF16 = jnp.bfloat16
F32 = jnp.float32


def _rms(x, g, eps):
    return x * lax.rsqrt(jnp.mean(x * x, axis=-1, keepdims=True) + eps) * g


def _resident(shape):
    return pl.BlockSpec(shape, lambda *_: (0,) * len(shape), pipeline_mode=pl.Buffered(1))


def _params(sem):
    return pltpu.CompilerParams(dimension_semantics=sem, vmem_limit_bytes=VMEM_LIMIT)


def _ffn_kernel(x_ref, g_ref, wg_ref, wu_ref, wd_ref, fn_ref, o_ref, *, final):
    x = x_ref[...]
    h = _rms(x, g_ref[...], NORM_EPS).astype(BF16)
    gate = jnp.dot(h, wg_ref[...], preferred_element_type=F32)
    up = jnp.dot(h, wu_ref[...], preferred_element_type=F32)
    a = (gate * jax.nn.sigmoid(gate) * up).astype(BF16)
    y = x + 0.5 * jnp.dot(a, wd_ref[...], preferred_element_type=F32)
    if final:
        y = _rms(y, fn_ref[...], NORM_EPS)
    o_ref[...] = y


def _ffn(x, g, wg, wu, wd, fn, final):
    tm = FFN_TM
    return pl.pallas_call(
        functools.partial(_ffn_kernel, final=final),
        grid=(N_TOK // tm,),
        in_specs=[
            pl.BlockSpec((tm, D_MODEL), lambda i: (i, 0)),
            _resident((1, D_MODEL)),
            _resident((D_MODEL, D_FF)),
            _resident((D_MODEL, D_FF)),
            _resident((D_FF, D_MODEL)),
            _resident((1, D_MODEL)),
        ],
        out_specs=pl.BlockSpec((tm, D_MODEL), lambda i: (i, 0)),
        out_shape=jax.ShapeDtypeStruct((N_TOK, D_MODEL), F32),
        compiler_params=_params(("arbitrary",)),
        name="ffn_final" if final else "ffn",
    )(x, g, wg, wu, wd, fn)


def _proj_kernel(x_ref, g_ref, w_ref, uf_ref, q_ref, k_ref, v_ref, ga_ref, gb_ref):
    h = _rms(x_ref[...], g_ref[...], NORM_EPS).astype(BF16)
    p = jnp.dot(h, w_ref[...], preferred_element_type=F32)
    c = 0
    uf_ref[...] = p[:, c:c + F_WIDTH].astype(BF16)
    c += F_WIDTH
    q_ref[...] = (p[:, c:c + QK_WIDTH] * (DA_HEAD_DIM ** -0.5)).astype(BF16)
    c += QK_WIDTH
    k_ref[...] = p[:, c:c + QK_WIDTH].astype(BF16)
    c += QK_WIDTH
    v_ref[...] = p[:, c:c + V_WIDTH].astype(BF16)
    c += V_WIDTH
    ga_ref[...] = jax.nn.sigmoid(p[:, c:c + D_MODEL]).astype(BF16)
    c += D_MODEL
    gb_ref[...] = jax.nn.sigmoid(p[:, c:c + D_MODEL]).astype(BF16)


def _in_proj(x, g, w_in):
    tm = PROJ_TM
    widths = (F_WIDTH, QK_WIDTH, QK_WIDTH, V_WIDTH, D_MODEL, D_MODEL)
    return pl.pallas_call(
        _proj_kernel,
        grid=(N_TOK // tm,),
        in_specs=[
            pl.BlockSpec((tm, D_MODEL), lambda i: (i, 0)),
            _resident((1, D_MODEL)),
            _resident((D_MODEL, IN_WIDTH)),
        ],
        out_specs=[pl.BlockSpec((tm, w), lambda i: (i, 0)) for w in widths],
        out_shape=[jax.ShapeDtypeStruct((N_TOK, w), BF16) for w in widths],
        compiler_params=_params(("arbitrary",)),
        name="in_proj",
    )(x, g, w_in)


def _dft_constants():
    n = np.arange(SEQ, dtype=np.int64)
    ang = 2.0 * np.pi * ((n[:, None] * n[None, :]) % SEQ) / SEQ
    seq_mat = np.concatenate([np.cos(ang), np.sin(ang)], axis=1)
    c = np.arange(F_GROUP_CH, dtype=np.int64)
    angc = 2.0 * np.pi * ((c[:, None] * c[None, :]) % F_GROUP_CH) / F_GROUP_CH
    scale = 1.0 / math.sqrt(SEQ * F_GROUP_CH)
    eye = np.eye(F_GROUPS)
    chan = np.concatenate([np.kron(eye, np.cos(angc)), -np.kron(eye, np.sin(angc))], axis=1) * scale
    return jnp.asarray(seq_mat, dtype=BF16), jnp.asarray(chan, dtype=BF16)


def _fourier_kernel(u_ref, seq_ref, chan_ref, w_ref, o_ref, z_ref):
    u = u_ref[...]
    z_ref[0:SEQ, :] = jnp.dot(u, chan_ref[:, 0:F_WIDTH],
                              preferred_element_type=F32).astype(BF16)
    z_ref[SEQ:2 * SEQ, :] = jnp.dot(u, chan_ref[:, F_WIDTH:2 * F_WIDTH],
                                    preferred_element_type=F32).astype(BF16)
    half = SEQ // 2
    for r in range(2):
        rows = pl.ds(r * half, half)
        y = jnp.dot(seq_ref[rows, :], z_ref[...], preferred_element_type=F32)
        o_ref[rows, :] = jnp.dot(y.astype(BF16), w_ref[...],
                                 preferred_element_type=F32).astype(BF16)


def _fourier(uf, seq_mat, chan_mat, w_fo):
    return pl.pallas_call(
        _fourier_kernel,
        grid=(BATCH,),
        in_specs=[
            pl.BlockSpec((SEQ, F_WIDTH), lambda b: (b, 0)),
            _resident((SEQ, 2 * SEQ)),
            _resident((F_WIDTH, 2 * F_WIDTH)),
            _resident((F_WIDTH, D_MODEL)),
        ],
        out_specs=pl.BlockSpec((SEQ, D_MODEL), lambda b: (b, 0)),
        out_shape=jax.ShapeDtypeStruct((N_TOK, D_MODEL), BF16),
        scratch_shapes=[pltpu.VMEM((2 * SEQ, F_WIDTH), BF16)],
        compiler_params=_params(("arbitrary",)),
        name="fourier",
    )(uf, seq_mat, chan_mat, w_fo)


def _bucket(rel):
    nb = REL_BUCKETS // 2
    max_exact = nb // 2
    n = jnp.minimum(jnp.abs(rel), REL_MAX_DIST)
    nf = n.astype(F32)
    expo = lax.shift_right_logical(lax.bitcast_convert_type(nf * nf, jnp.int32), 23) - 127
    large = jnp.minimum(expo + 2, nb - 1)
    return jnp.where(rel > 0, nb, 0) + jnp.where(n < max_exact, n, large)


def _attn_kernel(q_ref, k_ref, v_ref, posc_ref, posr_ref, tbl_ref, lam_ref, sg_ref, o_ref):
    tq = q_ref.shape[0]
    lq1, lk1, lq2, lk2 = (lam_ref[i:i + 1, :] for i in range(4))
    lam = (jnp.exp(jnp.sum(lq1 * lk1, axis=-1, keepdims=True))
           - jnp.exp(jnp.sum(lq2 * lk2, axis=-1, keepdims=True)) + LAMBDA_INIT)

    rel = posr_ref[0] - posc_ref[...]
    bucket = _bucket(rel)
    lane = lax.broadcasted_iota(jnp.int32, (tq, DA_V_DIM), 1)
    first_map = lane < DA_HEAD_DIM
    nt = (((1,), (1,)), ((), ()))

    for h in range(DA_HEADS):
        cols = slice(h * DA_V_DIM, (h + 1) * DA_V_DIM)
        tbl = jnp.broadcast_to(tbl_ref[h:h + 1, :], (tq, LANES))
        bias = jnp.concatenate(
            [jnp.take_along_axis(tbl, bucket[:, c * LANES:(c + 1) * LANES], axis=1)
             for c in range(SEQ // LANES)], axis=1)
        qh = q_ref[:, cols]
        kh = k_ref[:, cols]
        zero = jnp.zeros_like(qh)
        s1 = lax.dot_general(jnp.where(first_map, qh, zero), kh, nt,
                             preferred_element_type=F32) + bias
        s2 = lax.dot_general(jnp.where(first_map, zero, qh), kh, nt,
                             preferred_element_type=F32) + bias
        p1 = jnp.exp(s1 - jnp.max(s1, axis=-1, keepdims=True))
        p2 = jnp.exp(s2 - jnp.max(s2, axis=-1, keepdims=True))
        r1 = 1.0 / jnp.sum(p1, axis=-1, keepdims=True)
        r2 = lam / jnp.sum(p2, axis=-1, keepdims=True)
        attn = (p1 * r1 - p2 * r2).astype(BF16)
        o = jnp.dot(attn, v_ref[:, cols], preferred_element_type=F32)
        o = _rms(o, sg_ref[...], SUBLN_EPS) * (1.0 - LAMBDA_INIT)
        o_ref[:, cols] = o.astype(BF16)


def _attention(q, k, v, positions, tbl, lam_rows, subln_g):
    tq = ATT_TQ
    nq = SEQ // tq
    posc = positions.reshape(N_TOK, 1)
    posr = positions.reshape(BATCH, 1, SEQ)
    return pl.pallas_call(
        _attn_kernel,
        grid=(BATCH, nq),
        in_specs=[
            pl.BlockSpec((tq, QK_WIDTH), lambda b, i: (b * nq + i, 0)),
            pl.BlockSpec((SEQ, QK_WIDTH), lambda b, i: (b, 0)),
            pl.BlockSpec((SEQ, V_WIDTH), lambda b, i: (b, 0)),
            pl.BlockSpec((tq, 1), lambda b, i: (b * nq + i, 0)),
            pl.BlockSpec((1, 1, SEQ), lambda b, i: (b, 0, 0)),
            _resident((8, LANES)),
            _resident((4, DA_HEAD_DIM)),
            _resident((1, DA_V_DIM)),
        ],
        out_specs=pl.BlockSpec((tq, V_WIDTH), lambda b, i: (b * nq + i, 0)),
        out_shape=jax.ShapeDtypeStruct((N_TOK, V_WIDTH), BF16),
        compiler_params=_params(("arbitrary", "arbitrary")),
        name="diff_attn",
    )(q, k, v, posc, posr, tbl, lam_rows, subln_g)


def _merge_kernel(x_ref, ya_ref, o_ref, ga_ref, gb_ref, wa_ref, wo_ref, out_ref):
    yb = jnp.dot(o_ref[...], wa_ref[...], preferred_element_type=F32)
    merged = (ga_ref[...].astype(F32) * ya_ref[...].astype(F32)
              + gb_ref[...].astype(F32) * yb).astype(BF16)
    out_ref[...] = x_ref[...] + jnp.dot(merged, wo_ref[...], preferred_element_type=F32)


def _merge(x, ya, o, ga, gb, w_ao, w_out):
    tm = MERGE_TM
    row = lambda w: pl.BlockSpec((tm, w), lambda i: (i, 0))
    return pl.pallas_call(
        _merge_kernel,
        grid=(N_TOK // tm,),
        in_specs=[row(D_MODEL), row(D_MODEL), row(V_WIDTH), row(D_MODEL), row(D_MODEL),
                  _resident((V_WIDTH, D_MODEL)), _resident((D_MODEL, D_MODEL))],
        out_specs=row(D_MODEL),
        out_shape=jax.ShapeDtypeStruct((N_TOK, D_MODEL), F32),
        compiler_params=_params(("arbitrary",)),
        name="merge",
    )(x, ya, o, ga, gb, w_ao, w_out)


def kernel(x, positions, rel_bias, ffn1_norm, ffn1_wg, ffn1_wu, ffn1_wd, mix_norm, w_in,
           lambda_q1, lambda_k1, lambda_q2, lambda_k2, subln_g, w_fourier_out, w_attn_out,
           w_out, ffn2_norm, ffn2_wg, ffn2_wu, ffn2_wd, final_norm):
    assert x.shape == (BATCH, SEQ, D_MODEL) and positions.shape == (BATCH, SEQ)
    bf = lambda w: w.astype(BF16)
    row = lambda g: g.reshape(1, -1).astype(F32)
    seq_mat, chan_mat = _dft_constants()
    tbl = jnp.zeros((8, LANES), F32).at[:DA_HEADS, :REL_BUCKETS].set(rel_bias.T.astype(F32))
    lam_rows = jnp.concatenate([lambda_q1, lambda_k1, lambda_q2, lambda_k2], axis=0).astype(F32)
    fn = row(final_norm)

    xt = x.reshape(N_TOK, D_MODEL)
    x1 = _ffn(xt, row(ffn1_norm[0]), bf(ffn1_wg[0]), bf(ffn1_wu[0]), bf(ffn1_wd[0]), fn, False)
    uf, q, k, v, ga, gb = _in_proj(x1, row(mix_norm[0]), bf(w_in[0]))
    ya = _fourier(uf, seq_mat, chan_mat, bf(w_fourier_out[0]))
    o = _attention(q, k, v, positions.astype(jnp.int32), tbl, lam_rows, row(subln_g[0]))
    x2 = _merge(x1, ya, o, ga, gb, bf(w_attn_out[0]), bf(w_out[0]))
    out = _ffn(x2, row(ffn2_norm[0]), bf(ffn2_wg[0]), bf(ffn2_wu[0]), bf(ffn2_wd[0]), fn, True)
    return out.reshape(BATCH, SEQ, D_MODEL)
```

```python
import functools
import math

import numpy as np
import jax
import jax.numpy as jnp
from jax import lax
from jax.experimental import pallas as pl
from jax.experimental.pallas import tpu as pltpu

D_MODEL = 1024
BATCH = 8
SEQ = 2048
D_FF = 2816
F_GROUPS = 4
F_GROUP_CH = 128
F_WIDTH = F_GROUPS * F_GROUP_CH
DA_HEADS = 4
DA_HEAD_DIM = 64
DA_V_DIM = 2 * DA_HEAD_DIM
QK_WIDTH = DA_HEADS * 2 * DA_HEAD_DIM
V_WIDTH = DA_HEADS * DA_V_DIM
IN_WIDTH = F_WIDTH + 2 * QK_WIDTH + V_WIDTH + 2 * D_MODEL
REL_BUCKETS = 32
REL_MAX_DIST = 128
NORM_EPS = 1e-6
SUBLN_EPS = 1e-5
LAMBDA_INIT = 0.8 - 0.6 * math.exp(-0.3 * 0)

N_TOK = BATCH * SEQ
LANES = 128
VMEM_LIMIT = 56 * 1024 * 1024

FFN_TM = 512
PROJ_TM = 512
MERGE_TM = 512
ATT_TQ = 256
ATT_TK = 512
ATT_SUB = 128

BF16 = jnp.bfloat16
F32 = jnp.float32


def _rms(x, g, eps):
    return x * lax.rsqrt(jnp.mean(x * x, axis=-1, keepdims=True) + eps) * g


def _resident(shape):
    return pl.BlockSpec(shape, lambda *_: (0,) * len(shape), pipeline_mode=pl.Buffered(1))


def _params(sem):
    return pltpu.CompilerParams(dimension_semantics=sem, vmem_limit_bytes=VMEM_LIMIT)


def _ffn_kernel(x_ref, g_ref, wg_ref, wu_ref, wd_ref, fn_ref, o_ref, *, final):
    x = x_ref[...]
    h = _rms(x, g_ref[...], NORM_EPS).astype(BF16)
    gate = jnp.dot(h, wg_ref[...], preferred_element_type=F32)
    up = jnp.dot(h, wu_ref[...], preferred_element_type=F32)
    a = (gate * jax.nn.sigmoid(gate) * up).astype(BF16)
    y = x + 0.5 * jnp.dot(a, wd_ref[...], preferred_element_type=F32)
    if final:
        y = _rms(y, fn_ref[...], NORM_EPS)
    o_ref[...] = y


def _ffn(x, g, wg, wu, wd, fn, final):
    tm = FFN_TM
    return pl.pallas_call(
        functools.partial(_ffn_kernel, final=final),
        grid=(N_TOK // tm,),
        in_specs=[
            pl.BlockSpec((tm, D_MODEL), lambda i: (i, 0)),
            _resident((1, D_MODEL)),
            _resident((D_MODEL, D_FF)),
            _resident((D_MODEL, D_FF)),
            _resident((D_FF, D_MODEL)),
            _resident((1, D_MODEL)),
        ],
        out_specs=pl.BlockSpec((tm, D_MODEL), lambda i: (i, 0)),
        out_shape=jax.ShapeDtypeStruct((N_TOK, D_MODEL), F32),
        compiler_params=_params(("arbitrary",)),
        name="ffn_final" if final else "ffn",
    )(x, g, wg, wu, wd, fn)


def _proj_kernel(x_ref, g_ref, w_ref, uf_ref, q_ref, k_ref, v_ref, ga_ref, gb_ref):
    h = _rms(x_ref[...], g_ref[...], NORM_EPS).astype(BF16)
    p = jnp.dot(h, w_ref[...], preferred_element_type=F32)
    c = 0
    uf_ref[...] = p[:, c:c + F_WIDTH].astype(BF16)
    c += F_WIDTH
    q_ref[...] = (p[:, c:c + QK_WIDTH] * (DA_HEAD_DIM ** -0.5 * math.log2(math.e))).astype(BF16)
    c += QK_WIDTH
    k_ref[...] = p[:, c:c + QK_WIDTH].astype(BF16)
    c += QK_WIDTH
    v_ref[...] = p[:, c:c + V_WIDTH].astype(BF16)
    c += V_WIDTH
    ga_ref[...] = jax.nn.sigmoid(p[:, c:c + D_MODEL]).astype(BF16)
    c += D_MODEL
    gb_ref[...] = jax.nn.sigmoid(p[:, c:c + D_MODEL]).astype(BF16)


def _in_proj(x, g, w_in):
    tm = PROJ_TM
    widths = (F_WIDTH, QK_WIDTH, QK_WIDTH, V_WIDTH, D_MODEL, D_MODEL)
    return pl.pallas_call(
        _proj_kernel,
        grid=(N_TOK // tm,),
        in_specs=[
            pl.BlockSpec((tm, D_MODEL), lambda i: (i, 0)),
            _resident((1, D_MODEL)),
            _resident((D_MODEL, IN_WIDTH)),
        ],
        out_specs=[pl.BlockSpec((tm, w), lambda i: (i, 0)) for w in widths],
        out_shape=[jax.ShapeDtypeStruct((N_TOK, w), BF16) for w in widths],
        compiler_params=_params(("arbitrary",)),
        name="in_proj",
    )(x, g, w_in)


def _dft_constants():
    n = np.arange(SEQ, dtype=np.int64)
    ang = 2.0 * np.pi * ((n[:, None] * n[None, :]) % SEQ) / SEQ
    seq_mat = np.concatenate([np.cos(ang), np.sin(ang)], axis=1)
    c = np.arange(F_GROUP_CH, dtype=np.int64)
    angc = 2.0 * np.pi * ((c[:, None] * c[None, :]) % F_GROUP_CH) / F_GROUP_CH
    scale = 1.0 / math.sqrt(SEQ * F_GROUP_CH)
    eye = np.eye(F_GROUPS)
    chan = np.concatenate([np.kron(eye, np.cos(angc)), -np.kron(eye, np.sin(angc))], axis=1) * scale
    return jnp.asarray(seq_mat, dtype=BF16), jnp.asarray(chan, dtype=BF16)


def _fourier_kernel(u_ref, seq_ref, chan_ref, w_ref, o_ref, z_ref):
    u = u_ref[...]
    z_ref[0:SEQ, :] = jnp.dot(u, chan_ref[:, 0:F_WIDTH],
                              preferred_element_type=F32).astype(BF16)
    z_ref[SEQ:2 * SEQ, :] = jnp.dot(u, chan_ref[:, F_WIDTH:2 * F_WIDTH],
                                    preferred_element_type=F32).astype(BF16)
    half = SEQ // 2
    for r in range(2):
        rows = pl.ds(r * half, half)
        y = jnp.dot(seq_ref[rows, :], z_ref[...], preferred_element_type=F32)
        o_ref[rows, :] = jnp.dot(y.astype(BF16), w_ref[...],
                                 preferred_element_type=F32).astype(BF16)


def _fourier(uf, seq_mat, chan_mat, w_fo):
    return pl.pallas_call(
        _fourier_kernel,
        grid=(BATCH,),
        in_specs=[
            pl.BlockSpec((SEQ, F_WIDTH), lambda b: (b, 0)),
            _resident((SEQ, 2 * SEQ)),
            _resident((F_WIDTH, 2 * F_WIDTH)),
            _resident((F_WIDTH, D_MODEL)),
        ],
        out_specs=pl.BlockSpec((SEQ, D_MODEL), lambda b: (b, 0)),
        out_shape=jax.ShapeDtypeStruct((N_TOK, D_MODEL), BF16),
        scratch_shapes=[pltpu.VMEM((2 * SEQ, F_WIDTH), BF16)],
        compiler_params=_params(("arbitrary",)),
        name="fourier",
    )(uf, seq_mat, chan_mat, w_fo)


def _bucket(rel):
    nb = REL_BUCKETS // 2
    max_exact = nb // 2
    n = jnp.minimum(jnp.abs(rel), REL_MAX_DIST)
    nf = n.astype(F32)
    expo = lax.shift_right_logical(lax.bitcast_convert_type(nf * nf, jnp.int32), 23) - 127
    large = jnp.minimum(expo + 2, nb - 1)
    return jnp.where(rel > 0, nb, 0) + jnp.where(n < max_exact, n, large)


def _attn_kernel(kmin_ref, kmax_ref, qmin_ref, qmax_ref,
                 q_ref, k_ref, v_ref, posk_ref, posq_ref, tbl_ref, far_ref, lam_ref, sg_ref,
                 o_ref, vt_ref, poskb_ref, bias_ref):
    b = pl.program_id(0)
    i = pl.program_id(1)
    tq, tk, sub = ATT_TQ, ATT_TK, ATT_SUB

    @pl.when(i == 0)
    def _():
        for h in range(DA_HEADS):
            for c in range(SEQ // tk):
                blk = v_ref[c * tk:(c + 1) * tk, h * DA_V_DIM:(h + 1) * DA_V_DIM]
                vt_ref[h * DA_V_DIM:(h + 1) * DA_V_DIM, c * tk:(c + 1) * tk] = (
                    blk.astype(F32).T.astype(BF16))
        poskb_ref[...] = jnp.broadcast_to(posk_ref[...], (SEQ, LANES))

    lq1, lk1, lq2, lk2 = (lam_ref[r:r + 1, :] for r in range(4))
    lam = (jnp.exp(jnp.sum(lq1 * lk1, axis=-1, keepdims=True))
           - jnp.exp(jnp.sum(lq2 * lk2, axis=-1, keepdims=True)) + LAMBDA_INIT)

    posq = posq_ref[0]
    qmin = qmin_ref[b, i]
    qmax = qmax_ref[b, i]
    lane = lax.broadcasted_iota(jnp.int32, (tq, DA_V_DIM), 1)
    first_map = lane < DA_HEAD_DIM
    nt = (((1,), (1,)), ((), ()))

    qz = []
    for h in range(DA_HEADS):
        qh = q_ref[:, h * DA_V_DIM:(h + 1) * DA_V_DIM]
        zero = jnp.zeros_like(qh)
        qz.append(jnp.concatenate([jnp.where(first_map, qh, zero),
                                   jnp.where(first_map, zero, qh)], axis=0))
    state = [None] * DA_HEADS

    for c in range(SEQ // tk):
        for r in range(tk // sub):
            blk = c * (tk // sub) + r
            rows = slice(r * sub, (r + 1) * sub)
            far_right = kmin_ref[b, blk] - qmax >= REL_MAX_DIST
            far_left = qmin - kmax_ref[b, blk] >= REL_MAX_DIST
            far = jnp.logical_or(far_right, far_left)

            @pl.when(far)
            def _(rows=rows, far_right=far_right):
                for h in range(DA_HEADS):
                    cst = jnp.where(far_right, far_ref[1, h], far_ref[0, h])
                    bias_ref[h, rows, :] = jnp.full((sub, tq), cst, F32)

            @pl.when(jnp.logical_not(far))
            def _(rows=rows, blk=blk):
                tbls = [jnp.broadcast_to(tbl_ref[h:h + 1, :], (sub, LANES)) for h in range(DA_HEADS)]
                for t in range(tq // LANES):
                    ln = slice(t * LANES, (t + 1) * LANES)
                    rel = poskb_ref[blk * sub:(blk + 1) * sub, :] - posq[:, ln]
                    bucket = _bucket(rel)
                    for h in range(DA_HEADS):
                        bias_ref[h, rows, ln] = jnp.take_along_axis(
                            tbls[h], bucket, axis=1, mode="promise_in_bounds")

        for h in range(DA_HEADS):
            cols = slice(h * DA_V_DIM, (h + 1) * DA_V_DIM)
            kc = k_ref[c * tk:(c + 1) * tk, cols]
            vt = vt_ref[cols, c * tk:(c + 1) * tk]
            bias = bias_ref[h]
            s = (lax.dot_general(kc, qz[h], nt, preferred_element_type=F32)
                 + jnp.concatenate([bias, bias], axis=1))
            m_c = jnp.max(s, axis=0, keepdims=True)
            if c == 0:
                m_new = m_c
            else:
                m_old, l_old, acc_old = state[h]
                m_new = jnp.maximum(m_old, m_c)
                alpha = jnp.exp2(m_old - m_new)
            p = jnp.exp2(s - m_new)
            l_new = jnp.sum(p, axis=0, keepdims=True)
            acc_new = jnp.dot(vt, p.astype(BF16), preferred_element_type=F32)
            if c > 0:
                l_new = alpha * l_old + l_new
                acc_new = alpha * acc_old + acc_new
            state[h] = (m_new, l_new, acc_new)

    for h in range(DA_HEADS):
        _, l_fin, acc = state[h]
        r1 = 1.0 / l_fin[:, :tq]
        r2 = lam / l_fin[:, tq:]
        o = (acc[:, :tq] * r1 - acc[:, tq:] * r2).T
        o = _rms(o, sg_ref[...], SUBLN_EPS) * (1.0 - LAMBDA_INIT)
        o_ref[:, h * DA_V_DIM:(h + 1) * DA_V_DIM] = o.astype(BF16)


def _attention(q, k, v, positions, rel_bias, lam_rows, subln_g):
    tq, sub = ATT_TQ, ATT_SUB
    nq = SEQ // tq
    nsub = SEQ // sub
    log2e = math.log2(math.e)
    tbl_t = rel_bias.T.astype(F32) * log2e
    tbl = jnp.zeros((8, LANES), F32).at[:DA_HEADS, :REL_BUCKETS].set(tbl_t)
    nb = REL_BUCKETS // 2
    far = jnp.stack([tbl_t[:, nb - 1], tbl_t[:, 2 * nb - 1]])
    pk = positions.reshape(BATCH, nsub, sub)
    pq = positions.reshape(BATCH, nq, tq)
    posk = positions.reshape(N_TOK, 1)
    posq = positions.reshape(BATCH * nq, 1, tq)
    smem = pl.BlockSpec(memory_space=pltpu.SMEM)
    grid_spec = pltpu.PrefetchScalarGridSpec(
        num_scalar_prefetch=4,
        grid=(BATCH, nq),
        in_specs=[
            pl.BlockSpec((tq, QK_WIDTH), lambda b, i, *_: (b * nq + i, 0)),
            pl.BlockSpec((SEQ, QK_WIDTH), lambda b, i, *_: (b, 0)),
            pl.BlockSpec((SEQ, V_WIDTH), lambda b, i, *_: (b, 0)),
            pl.BlockSpec((SEQ, 1), lambda b, i, *_: (b, 0)),
            pl.BlockSpec((1, 1, tq), lambda b, i, *_: (b * nq + i, 0, 0)),
            _resident((8, LANES)),
            smem,
            _resident((4, DA_HEAD_DIM)),
            _resident((1, DA_V_DIM)),
        ],
        out_specs=pl.BlockSpec((tq, V_WIDTH), lambda b, i, *_: (b * nq + i, 0)),
        scratch_shapes=[
            pltpu.VMEM((V_WIDTH, SEQ), BF16),
            pltpu.VMEM((SEQ, LANES), jnp.int32),
            pltpu.VMEM((DA_HEADS, ATT_TK, tq), F32),
        ],
    )
    return pl.pallas_call(
        _attn_kernel,
        grid_spec=grid_spec,
        out_shape=jax.ShapeDtypeStruct((N_TOK, V_WIDTH), BF16),
        compiler_params=_params(("arbitrary", "arbitrary")),
        name="diff_attn",
    )(pk.min(-1), pk.max(-1), pq.min(-1), pq.max(-1),
      q, k, v, posk, posq, tbl, far, lam_rows, subln_g)


def _merge_kernel(x_ref, ya_ref, o_ref, ga_ref, gb_ref, wa_ref, wo_ref, out_ref):
    yb = jnp.dot(o_ref[...], wa_ref[...], preferred_element_type=F32)
    merged = (ga_ref[...].astype(F32) * ya_ref[...].astype(F32)
              + gb_ref[...].astype(F32) * yb).astype(BF16)
    out_ref[...] = x_ref[...] + jnp.dot(merged, wo_ref[...], preferred_element_type=F32)


def _merge(x, ya, o, ga, gb, w_ao, w_out):
    tm = MERGE_TM
    row = lambda w: pl.BlockSpec((tm, w), lambda i: (i, 0))
    return pl.pallas_call(
        _merge_kernel,
        grid=(N_TOK // tm,),
        in_specs=[row(D_MODEL), row(D_MODEL), row(V_WIDTH), row(D_MODEL), row(D_MODEL),
                  _resident((V_WIDTH, D_MODEL)), _resident((D_MODEL, D_MODEL))],
        out_specs=row(D_MODEL),
        out_shape=jax.ShapeDtypeStruct((N_TOK, D_MODEL), F32),
        compiler_params=_params(("arbitrary",)),
        name="merge",
    )(x, ya, o, ga, gb, w_ao, w_out)


def kernel(x, positions, rel_bias, ffn1_norm, ffn1_wg, ffn1_wu, ffn1_wd, mix_norm, w_in,
           lambda_q1, lambda_k1, lambda_q2, lambda_k2, subln_g, w_fourier_out, w_attn_out,
           w_out, ffn2_norm, ffn2_wg, ffn2_wu, ffn2_wd, final_norm):
    assert x.shape == (BATCH, SEQ, D_MODEL) and positions.shape == (BATCH, SEQ)
    bf = lambda w: w.astype(BF16)
    row = lambda g: g.reshape(1, -1).astype(F32)
    seq_mat, chan_mat = _dft_constants()
    lam_rows = jnp.concatenate([lambda_q1, lambda_k1, lambda_q2, lambda_k2], axis=0).astype(F32)
    fn = row(final_norm)

    xt = x.reshape(N_TOK, D_MODEL)
    x1 = _ffn(xt, row(ffn1_norm[0]), bf(ffn1_wg[0]), bf(ffn1_wu[0]), bf(ffn1_wd[0]), fn, False)
    uf, q, k, v, ga, gb = _in_proj(x1, row(mix_norm[0]), bf(w_in[0]))
    ya = _fourier(uf, seq_mat, chan_mat, bf(w_fourier_out[0]))
    o = _attention(q, k, v, positions.astype(jnp.int32), rel_bias, lam_rows, row(subln_g[0]))
    x2 = _merge(x1, ya, o, ga, gb, bf(w_attn_out[0]), bf(w_out[0]))
    out = _ffn(x2, row(ffn2_norm[0]), bf(ffn2_wg[0]), bf(ffn2_wu[0]), bf(ffn2_wd[0]), fn, True)
    return out.reshape(BATCH, SEQ, D_MODEL)
```

```python
import functools
import math

import numpy as np
import jax
import jax.numpy as jnp
from jax import lax
from jax.experimental import pallas as pl
from jax.experimental.pallas import tpu as pltpu

D_MODEL = 1024
BATCH = 8
SEQ = 2048
D_FF = 2816
F_GROUPS = 4
F_GROUP_CH = 128
F_WIDTH = F_GROUPS * F_GROUP_CH
DA_HEADS = 4
DA_HEAD_DIM = 64
DA_V_DIM = 2 * DA_HEAD_DIM
QK_WIDTH = DA_HEADS * 2 * DA_HEAD_DIM
V_WIDTH = DA_HEADS * DA_V_DIM
IN_WIDTH = F_WIDTH + 2 * QK_WIDTH + V_WIDTH + 2 * D_MODEL
REL_BUCKETS = 32
REL_MAX_DIST = 128
NORM_EPS = 1e-6
SUBLN_EPS = 1e-5
LAMBDA_INIT = 0.8 - 0.6 * math.exp(-0.3 * 0)

N_TOK = BATCH * SEQ
LANES = 128
VMEM_LIMIT = 56 * 1024 * 1024

FFN_TM = 512
PROJ_TM = 512
MERGE_TM = 512
ATT_TQ = 256
ATT_TK = 512
ATT_SUB = 128
ATT_QK = 256
VT_ROWS = DA_V_DIM + 16

BF16 = jnp.bfloat16
F32 = jnp.float32


def _rms(x, g, eps):
    return x * lax.rsqrt(jnp.mean(x * x, axis=-1, keepdims=True) + eps) * g


def _resident(shape):
    return pl.BlockSpec(shape, lambda *_: (0,) * len(shape), pipeline_mode=pl.Buffered(1))


def _params(sem):
    return pltpu.CompilerParams(dimension_semantics=sem, vmem_limit_bytes=VMEM_LIMIT)


def _ffn_kernel(x_ref, g_ref, wg_ref, wu_ref, wd_ref, fn_ref, o_ref, *, final):
    x = x_ref[...]
    h = _rms(x, g_ref[...], NORM_EPS).astype(BF16)
    gate = jnp.dot(h, wg_ref[...], preferred_element_type=F32)
    up = jnp.dot(h, wu_ref[...], preferred_element_type=F32)
    a = (gate * jax.nn.sigmoid(gate) * up).astype(BF16)
    y = x + 0.5 * jnp.dot(a, wd_ref[...], preferred_element_type=F32)
    if final:
        y = _rms(y, fn_ref[...], NORM_EPS)
    o_ref[...] = y


def _ffn(x, g, wg, wu, wd, fn, final):
    tm = FFN_TM
    return pl.pallas_call(
        functools.partial(_ffn_kernel, final=final),
        grid=(N_TOK // tm,),
        in_specs=[
            pl.BlockSpec((tm, D_MODEL), lambda i: (i, 0)),
            _resident((1, D_MODEL)),
            _resident((D_MODEL, D_FF)),
            _resident((D_MODEL, D_FF)),
            _resident((D_FF, D_MODEL)),
            _resident((1, D_MODEL)),
        ],
        out_specs=pl.BlockSpec((tm, D_MODEL), lambda i: (i, 0)),
        out_shape=jax.ShapeDtypeStruct((N_TOK, D_MODEL), F32),
        compiler_params=_params(("arbitrary",)),
        name="ffn_final" if final else "ffn",
    )(x, g, wg, wu, wd, fn)


def _proj_kernel(x_ref, g_ref, w_ref, uf_ref, q_ref, k_ref, v_ref, ga_ref, gb_ref):
    h = _rms(x_ref[...], g_ref[...], NORM_EPS).astype(BF16)
    p = jnp.dot(h, w_ref[...], preferred_element_type=F32)
    c = 0
    uf_ref[...] = p[:, c:c + F_WIDTH].astype(BF16)
    c += F_WIDTH
    q_ref[...] = (p[:, c:c + QK_WIDTH] * (DA_HEAD_DIM ** -0.5 * math.log2(math.e))).astype(BF16)
    c += QK_WIDTH
    k_ref[...] = p[:, c:c + QK_WIDTH].astype(BF16)
    c += QK_WIDTH
    v_ref[...] = p[:, c:c + V_WIDTH].astype(BF16)
    c += V_WIDTH
    ga_ref[...] = jax.nn.sigmoid(p[:, c:c + D_MODEL]).astype(BF16)
    c += D_MODEL
    gb_ref[...] = jax.nn.sigmoid(p[:, c:c + D_MODEL]).astype(BF16)


def _in_proj(x, g, w_in):
    tm = PROJ_TM
    widths = (F_WIDTH, QK_WIDTH, QK_WIDTH, V_WIDTH, D_MODEL, D_MODEL)
    return pl.pallas_call(
        _proj_kernel,
        grid=(N_TOK // tm,),
        in_specs=[
            pl.BlockSpec((tm, D_MODEL), lambda i: (i, 0)),
            _resident((1, D_MODEL)),
            _resident((D_MODEL, IN_WIDTH)),
        ],
        out_specs=[pl.BlockSpec((tm, w), lambda i: (i, 0)) for w in widths],
        out_shape=[jax.ShapeDtypeStruct((N_TOK, w), BF16) for w in widths],
        compiler_params=_params(("arbitrary",)),
        name="in_proj",
    )(x, g, w_in)


def _dft_constants():
    n = np.arange(SEQ, dtype=np.int64)
    ang = 2.0 * np.pi * ((n[:, None] * n[None, :]) % SEQ) / SEQ
    seq_mat = np.concatenate([np.cos(ang), np.sin(ang)], axis=1)
    c = np.arange(F_GROUP_CH, dtype=np.int64)
    angc = 2.0 * np.pi * ((c[:, None] * c[None, :]) % F_GROUP_CH) / F_GROUP_CH
    scale = 1.0 / math.sqrt(SEQ * F_GROUP_CH)
    eye = np.eye(F_GROUPS)
    chan = np.concatenate([np.kron(eye, np.cos(angc)), -np.kron(eye, np.sin(angc))], axis=1) * scale
    return jnp.asarray(seq_mat, dtype=BF16), jnp.asarray(chan, dtype=BF16)


def _fourier_kernel(u_ref, seq_ref, chan_ref, w_ref, o_ref, z_ref):
    u = u_ref[...]
    z_ref[0:SEQ, :] = jnp.dot(u, chan_ref[:, 0:F_WIDTH],
                              preferred_element_type=F32).astype(BF16)
    z_ref[SEQ:2 * SEQ, :] = jnp.dot(u, chan_ref[:, F_WIDTH:2 * F_WIDTH],
                                    preferred_element_type=F32).astype(BF16)
    half = SEQ // 2
    for r in range(2):
        rows = pl.ds(r * half, half)
        y = jnp.dot(seq_ref[rows, :], z_ref[...], preferred_element_type=F32)
        o_ref[rows, :] = jnp.dot(y.astype(BF16), w_ref[...],
                                 preferred_element_type=F32).astype(BF16)


def _fourier(uf, seq_mat, chan_mat, w_fo):
    return pl.pallas_call(
        _fourier_kernel,
        grid=(BATCH,),
        in_specs=[
            pl.BlockSpec((SEQ, F_WIDTH), lambda b: (b, 0)),
            _resident((SEQ, 2 * SEQ)),
            _resident((F_WIDTH, 2 * F_WIDTH)),
            _resident((F_WIDTH, D_MODEL)),
        ],
        out_specs=pl.BlockSpec((SEQ, D_MODEL), lambda b: (b, 0)),
        out_shape=jax.ShapeDtypeStruct((N_TOK, D_MODEL), BF16),
        scratch_shapes=[pltpu.VMEM((2 * SEQ, F_WIDTH), BF16)],
        compiler_params=_params(("arbitrary",)),
        name="fourier",
    )(uf, seq_mat, chan_mat, w_fo)


def _bucket(rel):
    nb = REL_BUCKETS // 2
    max_exact = nb // 2
    n = jnp.minimum(jnp.abs(rel), REL_MAX_DIST)
    nf = n.astype(F32)
    expo = lax.shift_right_logical(lax.bitcast_convert_type(nf * nf, jnp.int32), 23) - 127
    large = jnp.minimum(expo + 2, nb - 1)
    return jnp.where(rel > 0, nb, 0) + jnp.where(n < max_exact, n, large)


def _attn_kernel(kmin_ref, kmax_ref, qmin_ref, qmax_ref,
                 q_ref, k_ref, v_ref, posk_ref, posq_ref, tbl_ref, far_ref, lam_ref, sg_ref,
                 o_ref, vt_ref, poskb_ref, bias_ref):
    b = pl.program_id(0)
    i = pl.program_id(1)
    tq, tk, sub = ATT_TQ, ATT_TK, ATT_SUB

    @pl.when(i == 0)
    def _():
        for h in range(DA_HEADS):
            for c in range(SEQ // tk):
                blk = v_ref[c * tk:(c + 1) * tk, h * DA_V_DIM:(h + 1) * DA_V_DIM]
                vt_ref[h * VT_ROWS:h * VT_ROWS + DA_V_DIM, c * tk:(c + 1) * tk] = (
                    blk.astype(F32).T.astype(BF16))
            vt_ref[h * VT_ROWS + DA_V_DIM:(h + 1) * VT_ROWS, :] = jnp.ones(
                (VT_ROWS - DA_V_DIM, SEQ), BF16)
        poskb_ref[...] = jnp.broadcast_to(posk_ref[...], (SEQ, LANES))

    lq1, lk1, lq2, lk2 = (lam_ref[r:r + 1, :] for r in range(4))
    lam = (jnp.exp(jnp.sum(lq1 * lk1, axis=-1, keepdims=True))
           - jnp.exp(jnp.sum(lq2 * lk2, axis=-1, keepdims=True)) + LAMBDA_INIT)

    posq = posq_ref[0]
    qmin = qmin_ref[b, i]
    qmax = qmax_ref[b, i]
    lane = lax.broadcasted_iota(jnp.int32, (tq, DA_V_DIM), 1)
    first_map = lane < DA_HEAD_DIM
    nt = (((1,), (1,)), ((), ()))

    qz = []
    for h in range(DA_HEADS):
        qh = q_ref[:, h * DA_V_DIM:(h + 1) * DA_V_DIM]
        zero = jnp.zeros_like(qh)
        qz.append(jnp.concatenate([jnp.where(first_map, qh, zero),
                                   jnp.where(first_map, zero, qh)], axis=0))
    state = [None] * DA_HEADS

    for blk in range(SEQ // sub):
        rows = slice(blk * sub, (blk + 1) * sub)
        far_right = kmin_ref[b, blk] - qmax >= REL_MAX_DIST
        far_left = qmin - kmax_ref[b, blk] >= REL_MAX_DIST
        far = jnp.logical_or(far_right, far_left)

        @pl.when(far)
        def _(rows=rows, far_right=far_right):
            for h in range(DA_HEADS):
                cst = jnp.where(far_right, far_ref[1, h], far_ref[0, h])
                bias_ref[h, rows, :] = jnp.full((sub, tq), cst, F32)

        @pl.when(jnp.logical_not(far))
        def _(rows=rows):
            tbls = [jnp.broadcast_to(tbl_ref[h:h + 1, :], (sub, LANES)) for h in range(DA_HEADS)]
            for t in range(tq // LANES):
                ln = slice(t * LANES, (t + 1) * LANES)
                rel = poskb_ref[rows, :] - posq[:, ln]
                bucket = _bucket(rel)
                for h in range(DA_HEADS):
                    bias_ref[h, rows, ln] = jnp.take_along_axis(
                        tbls[h], bucket, axis=1, mode="promise_in_bounds")

    def scores(c, h):
        parts, m_c = [], None
        for u in range(tk // ATT_QK):
            rows = slice(c * tk + u * ATT_QK, c * tk + (u + 1) * ATT_QK)
            bias = bias_ref[h, rows, :]
            s = (lax.dot_general(k_ref[rows, h * DA_V_DIM:(h + 1) * DA_V_DIM], qz[h], nt,
                                 preferred_element_type=F32)
                 + jnp.concatenate([bias, bias], axis=1))
            m_u = jnp.max(s, axis=0, keepdims=True)
            m_c = m_u if m_c is None else jnp.maximum(m_c, m_u)
            parts.append(s)
        return parts, m_c

    items = [(c, h) for c in range(SEQ // tk) for h in range(DA_HEADS)]
    ahead = scores(*items[0])
    for n, (c, h) in enumerate(items):
        s_parts, m_c = ahead
        if n + 1 < len(items):
            ahead = scores(*items[n + 1])
        vt = vt_ref[h * VT_ROWS:(h + 1) * VT_ROWS, c * tk:(c + 1) * tk]
        if c == 0:
            m_new = m_c
        else:
            m_old, acc_old = state[h]
            m_new = jnp.maximum(m_old, m_c)
            alpha = jnp.exp2(m_old - m_new)
        p = jnp.concatenate([jnp.exp2(s - m_new).astype(BF16) for s in s_parts], axis=0)
        acc_new = jnp.dot(vt, p, preferred_element_type=F32)
        if c > 0:
            acc_new = alpha * acc_old + acc_new
        state[h] = (m_new, acc_new)

    for h in range(DA_HEADS):
        _, acc = state[h]
        l_fin = acc[DA_V_DIM:DA_V_DIM + 1, :]
        acc = acc[:DA_V_DIM, :]
        r1 = 1.0 / l_fin[:, :tq]
        r2 = lam / l_fin[:, tq:]
        o = (acc[:, :tq] * r1 - acc[:, tq:] * r2).T
        o = _rms(o, sg_ref[...], SUBLN_EPS) * (1.0 - LAMBDA_INIT)
        o_ref[:, h * DA_V_DIM:(h + 1) * DA_V_DIM] = o.astype(BF16)


def _attention(q, k, v, positions, rel_bias, lam_rows, subln_g):
    tq, sub = ATT_TQ, ATT_SUB
    nq = SEQ // tq
    nsub = SEQ // sub
    log2e = math.log2(math.e)
    tbl_t = rel_bias.T.astype(F32) * log2e
    tbl = jnp.zeros((8, LANES), F32).at[:DA_HEADS, :REL_BUCKETS].set(tbl_t)
    nb = REL_BUCKETS // 2
    far = jnp.stack([tbl_t[:, nb - 1], tbl_t[:, 2 * nb - 1]])
    pk = positions.reshape(BATCH, nsub, sub)
    pq = positions.reshape(BATCH, nq, tq)
    posk = positions.reshape(N_TOK, 1)
    posq = positions.reshape(BATCH * nq, 1, tq)
    smem = pl.BlockSpec(memory_space=pltpu.SMEM)
    grid_spec = pltpu.PrefetchScalarGridSpec(
        num_scalar_prefetch=4,
        grid=(BATCH, nq),
        in_specs=[
            pl.BlockSpec((tq, QK_WIDTH), lambda b, i, *_: (b * nq + i, 0)),
            pl.BlockSpec((SEQ, QK_WIDTH), lambda b, i, *_: (b, 0)),
            pl.BlockSpec((SEQ, V_WIDTH), lambda b, i, *_: (b, 0)),
            pl.BlockSpec((SEQ, 1), lambda b, i, *_: (b, 0)),
            pl.BlockSpec((1, 1, tq), lambda b, i, *_: (b * nq + i, 0, 0)),
            _resident((8, LANES)),
            smem,
            _resident((4, DA_HEAD_DIM)),
            _resident((1, DA_V_DIM)),
        ],
        out_specs=pl.BlockSpec((tq, V_WIDTH), lambda b, i, *_: (b * nq + i, 0)),
        scratch_shapes=[
            pltpu.VMEM((DA_HEADS * VT_ROWS, SEQ), BF16),
            pltpu.VMEM((SEQ, LANES), jnp.int32),
            pltpu.VMEM((DA_HEADS, SEQ, tq), F32),
        ],
    )
    return pl.pallas_call(
        _attn_kernel,
        grid_spec=grid_spec,
        out_shape=jax.ShapeDtypeStruct((N_TOK, V_WIDTH), BF16),
        compiler_params=_params(("arbitrary", "arbitrary")),
        name="diff_attn",
    )(pk.min(-1), pk.max(-1), pq.min(-1), pq.max(-1),
      q, k, v, posk, posq, tbl, far, lam_rows, subln_g)


def _merge_kernel(x_ref, ya_ref, o_ref, ga_ref, gb_ref, wa_ref, wo_ref, out_ref):
    yb = jnp.dot(o_ref[...], wa_ref[...], preferred_element_type=F32)
    merged = (ga_ref[...].astype(F32) * ya_ref[...].astype(F32)
              + gb_ref[...].astype(F32) * yb).astype(BF16)
    out_ref[...] = x_ref[...] + jnp.dot(merged, wo_ref[...], preferred_element_type=F32)


def _merge(x, ya, o, ga, gb, w_ao, w_out):
    tm = MERGE_TM
    row = lambda w: pl.BlockSpec((tm, w), lambda i: (i, 0))
    return pl.pallas_call(
        _merge_kernel,
        grid=(N_TOK // tm,),
        in_specs=[row(D_MODEL), row(D_MODEL), row(V_WIDTH), row(D_MODEL), row(D_MODEL),
                  _resident((V_WIDTH, D_MODEL)), _resident((D_MODEL, D_MODEL))],
        out_specs=row(D_MODEL),
        out_shape=jax.ShapeDtypeStruct((N_TOK, D_MODEL), F32),
        compiler_params=_params(("arbitrary",)),
        name="merge",
    )(x, ya, o, ga, gb, w_ao, w_out)


def kernel(x, positions, rel_bias, ffn1_norm, ffn1_wg, ffn1_wu, ffn1_wd, mix_norm, w_in,
           lambda_q1, lambda_k1, lambda_q2, lambda_k2, subln_g, w_fourier_out, w_attn_out,
           w_out, ffn2_norm, ffn2_wg, ffn2_wu, ffn2_wd, final_norm):
    assert x.shape == (BATCH, SEQ, D_MODEL) and positions.shape == (BATCH, SEQ)
    bf = lambda w: w.astype(BF16)
    row = lambda g: g.reshape(1, -1).astype(F32)
    seq_mat, chan_mat = _dft_constants()
    lam_rows = jnp.concatenate([lambda_q1, lambda_k1, lambda_q2, lambda_k2], axis=0).astype(F32)
    fn = row(final_norm)

    xt = x.reshape(N_TOK, D_MODEL)
    x1 = _ffn(xt, row(ffn1_norm[0]), bf(ffn1_wg[0]), bf(ffn1_wu[0]), bf(ffn1_wd[0]), fn, False)
    uf, q, k, v, ga, gb = _in_proj(x1, row(mix_norm[0]), bf(w_in[0]))
    ya = _fourier(uf, seq_mat, chan_mat, bf(w_fourier_out[0]))
    o = _attention(q, k, v, positions.astype(jnp.int32), rel_bias, lam_rows, row(subln_g[0]))
    x2 = _merge(x1, ya, o, ga, gb, bf(w_attn_out[0]), bf(w_out[0]))
    out = _ffn(x2, row(ffn2_norm[0]), bf(ffn2_wg[0]), bf(ffn2_wu[0]), bf(ffn2_wd[0]), fn, True)
    return out.reshape(BATCH, SEQ, D_MODEL)
```

```python
import math

import numpy as np
import jax
import jax.numpy as jnp
from jax import lax
from jax.experimental import pallas as pl
from jax.experimental.pallas import tpu as pltpu

D_MODEL = 1024
BATCH = 8
SEQ = 2048
D_FF = 2816
F_GROUPS = 4
F_GROUP_CH = 128
F_WIDTH = F_GROUPS * F_GROUP_CH
DA_HEADS = 4
DA_HEAD_DIM = 64
DA_V_DIM = 2 * DA_HEAD_DIM
QK_WIDTH = DA_HEADS * 2 * DA_HEAD_DIM
V_WIDTH = DA_HEADS * DA_V_DIM
IN_WIDTH = F_WIDTH + 2 * QK_WIDTH + V_WIDTH + 2 * D_MODEL
REL_BUCKETS = 32
REL_MAX_DIST = 128
NORM_EPS = 1e-6
SUBLN_EPS = 1e-5
LAMBDA_INIT = 0.8 - 0.6 * math.exp(-0.3 * 0)

N_TOK = BATCH * SEQ
LANES = 128
VMEM_LIMIT = 56 * 1024 * 1024

FFN_TM = 512
ATT_TQ = 256
ATT_TK = 1024
ATT_SUB = 128
ATT_QK = 256
VT_ROWS = DA_V_DIM + 16
FILL_NONE, FILL_LEFT, FILL_RIGHT = 0, 1, 2

BF16 = jnp.bfloat16
F32 = jnp.float32


def _rms(x, g, eps):
    return x * lax.rsqrt(jnp.mean(x * x, axis=-1, keepdims=True) + eps) * g


def _resident(shape):
    return pl.BlockSpec(shape, lambda *_: (0,) * len(shape), pipeline_mode=pl.Buffered(1))


def _params(sem):
    return pltpu.CompilerParams(dimension_semantics=sem, vmem_limit_bytes=VMEM_LIMIT)


def _half_swiglu_residual(x, g, wg_ref, wu_ref, wd_ref):
    h = _rms(x, g, NORM_EPS).astype(BF16)
    gate = jnp.dot(h, wg_ref[...], preferred_element_type=F32)
    up = jnp.dot(h, wu_ref[...], preferred_element_type=F32)
    a = (gate * jax.nn.sigmoid(gate) * up).astype(BF16)
    return x + 0.5 * jnp.dot(a, wd_ref[...], preferred_element_type=F32)


def _ffn_specs():
    return [_resident((1, D_MODEL)), _resident((D_MODEL, D_FF)), _resident((D_MODEL, D_FF)),
            _resident((D_FF, D_MODEL))]


def _ffn_proj_kernel(x_ref, g_ref, wg_ref, wu_ref, wd_ref, gm_ref, w_ref,
                     x1_ref, uf_ref, q_ref, k_ref, v_ref, ga_ref, gb_ref):
    x1 = _half_swiglu_residual(x_ref[...], g_ref[...], wg_ref, wu_ref, wd_ref)
    x1_ref[...] = x1
    h = _rms(x1, gm_ref[...], NORM_EPS).astype(BF16)
    p = jnp.dot(h, w_ref[...], preferred_element_type=F32)
    c = 0
    uf_ref[...] = p[:, c:c + F_WIDTH].astype(BF16)
    c += F_WIDTH
    q_ref[...] = (p[:, c:c + QK_WIDTH] * (DA_HEAD_DIM ** -0.5 * math.log2(math.e))).astype(BF16)
    c += QK_WIDTH
    k_ref[...] = p[:, c:c + QK_WIDTH].astype(BF16)
    c += QK_WIDTH
    v_ref[...] = p[:, c:c + V_WIDTH].astype(BF16)
    c += V_WIDTH
    ga_ref[...] = jax.nn.sigmoid(p[:, c:c + D_MODEL]).astype(BF16)
    c += D_MODEL
    gb_ref[...] = jax.nn.sigmoid(p[:, c:c + D_MODEL]).astype(BF16)


def _ffn_proj(x, g, wg, wu, wd, g_mix, w_in):
    tm = FFN_TM
    row = lambda w: pl.BlockSpec((tm, w), lambda i: (i, 0))
    widths = (F_WIDTH, QK_WIDTH, QK_WIDTH, V_WIDTH, D_MODEL, D_MODEL)
    return pl.pallas_call(
        _ffn_proj_kernel,
        grid=(N_TOK // tm,),
        in_specs=[row(D_MODEL)] + _ffn_specs() + [_resident((1, D_MODEL)),
                                                  _resident((D_MODEL, IN_WIDTH))],
        out_specs=[row(D_MODEL)] + [row(w) for w in widths],
        out_shape=[jax.ShapeDtypeStruct((N_TOK, D_MODEL), F32)]
                  + [jax.ShapeDtypeStruct((N_TOK, w), BF16) for w in widths],
        compiler_params=_params(("arbitrary",)),
        name="ffn_proj",
    )(x, g, wg, wu, wd, g_mix, w_in)


def _dft_constants():
    n = np.arange(SEQ, dtype=np.int64)
    ang = 2.0 * np.pi * ((n[:, None] * n[None, :]) % SEQ) / SEQ
    seq_mat = np.concatenate([np.cos(ang), np.sin(ang)], axis=1)
    c = np.arange(F_GROUP_CH, dtype=np.int64)
    angc = 2.0 * np.pi * ((c[:, None] * c[None, :]) % F_GROUP_CH) / F_GROUP_CH
    scale = 1.0 / math.sqrt(SEQ * F_GROUP_CH)
    eye = np.eye(F_GROUPS)
    chan = np.concatenate([np.kron(eye, np.cos(angc)), -np.kron(eye, np.sin(angc))], axis=1) * scale
    return jnp.asarray(seq_mat, dtype=BF16), jnp.asarray(chan, dtype=BF16)


def _fourier_kernel(u_ref, seq_ref, chan_ref, w_ref, o_ref, z_ref):
    u = u_ref[...]
    z_ref[0:SEQ, :] = jnp.dot(u, chan_ref[:, 0:F_WIDTH],
                              preferred_element_type=F32).astype(BF16)
    z_ref[SEQ:2 * SEQ, :] = jnp.dot(u, chan_ref[:, F_WIDTH:2 * F_WIDTH],
                                    preferred_element_type=F32).astype(BF16)
    half = SEQ // 2
    for r in range(2):
        rows = pl.ds(r * half, half)
        y = jnp.dot(seq_ref[rows, :], z_ref[...], preferred_element_type=F32)
        o_ref[rows, :] = jnp.dot(y.astype(BF16), w_ref[...],
                                 preferred_element_type=F32).astype(BF16)


def _fourier(uf, seq_mat, chan_mat, w_fo):
    return pl.pallas_call(
        _fourier_kernel,
        grid=(BATCH,),
        in_specs=[
            pl.BlockSpec((SEQ, F_WIDTH), lambda b: (b, 0)),
            _resident((SEQ, 2 * SEQ)),
            _resident((F_WIDTH, 2 * F_WIDTH)),
            _resident((F_WIDTH, D_MODEL)),
        ],
        out_specs=pl.BlockSpec((SEQ, D_MODEL), lambda b: (b, 0)),
        out_shape=jax.ShapeDtypeStruct((N_TOK, D_MODEL), BF16),
        scratch_shapes=[pltpu.VMEM((2 * SEQ, F_WIDTH), BF16)],
        compiler_params=_params(("arbitrary",)),
        name="fourier",
    )(uf, seq_mat, chan_mat, w_fo)


def _bucket(rel):
    nb = REL_BUCKETS // 2
    max_exact = nb // 2
    n = jnp.minimum(jnp.abs(rel), REL_MAX_DIST)
    nf = n.astype(F32)
    expo = lax.shift_right_logical(lax.bitcast_convert_type(nf * nf, jnp.int32), 23) - 127
    large = jnp.minimum(expo + 2, nb - 1)
    return jnp.where(rel > 0, nb, 0) + jnp.where(n < max_exact, n, large)


def _attn_kernel(kmin_ref, kmax_ref, qmin_ref, qmax_ref,
                 q_ref, k_ref, v_ref, posk_ref, posq_ref, tbl_ref, far_ref, lam_ref, sg_ref,
                 o_ref, vt_ref, poskb_ref, bias_ref, fill_ref):
    b = pl.program_id(0)
    i = pl.program_id(1)
    tq, tk, sub = ATT_TQ, ATT_TK, ATT_SUB

    @pl.when(i == 0)
    def _():
        for h in range(DA_HEADS):
            for c in range(SEQ // tk):
                blk = v_ref[c * tk:(c + 1) * tk, h * DA_V_DIM:(h + 1) * DA_V_DIM]
                vt_ref[h * VT_ROWS:h * VT_ROWS + DA_V_DIM, c * tk:(c + 1) * tk] = (
                    blk.astype(F32).T.astype(BF16))
            vt_ref[h * VT_ROWS + DA_V_DIM:(h + 1) * VT_ROWS, :] = jnp.ones(
                (VT_ROWS - DA_V_DIM, SEQ), BF16)
        poskb_ref[...] = jnp.broadcast_to(posk_ref[...], (SEQ, LANES))

    @pl.when(jnp.logical_and(b == 0, i == 0))
    def _():
        for blk in range(SEQ // sub):
            fill_ref[blk] = FILL_NONE

    lq1, lk1, lq2, lk2 = (lam_ref[r:r + 1, :] for r in range(4))
    lam = (jnp.exp(jnp.sum(lq1 * lk1, axis=-1, keepdims=True))
           - jnp.exp(jnp.sum(lq2 * lk2, axis=-1, keepdims=True)) + LAMBDA_INIT)

    posq = posq_ref[0]
    qmin = qmin_ref[b, i]
    qmax = qmax_ref[b, i]
    lane = lax.broadcasted_iota(jnp.int32, (tq, DA_V_DIM), 1)
    first_map = lane < DA_HEAD_DIM
    nt = (((1,), (1,)), ((), ()))

    qz = []
    for h in range(DA_HEADS):
        qh = q_ref[:, h * DA_V_DIM:(h + 1) * DA_V_DIM]
        zero = jnp.zeros_like(qh)
        qz.append(jnp.concatenate([jnp.where(first_map, qh, zero),
                                   jnp.where(first_map, zero, qh)], axis=0))
    state = [None] * DA_HEADS

    for blk in range(SEQ // sub):
        rows = slice(blk * sub, (blk + 1) * sub)
        far_right = kmin_ref[b, blk] - qmax >= REL_MAX_DIST
        far_left = qmin - kmax_ref[b, blk] >= REL_MAX_DIST
        far = jnp.logical_or(far_right, far_left)

        code = jnp.where(far_right, FILL_RIGHT, FILL_LEFT)

        @pl.when(jnp.logical_and(far, fill_ref[blk] != code))
        def _(rows=rows, blk=blk, far_right=far_right, code=code):
            for h in range(DA_HEADS):
                cst = jnp.where(far_right, far_ref[1, h], far_ref[0, h])
                bias_ref[h, rows, :] = jnp.full((sub, tq), cst, F32)
            fill_ref[blk] = code

        @pl.when(jnp.logical_not(far))
        def _(rows=rows, blk=blk):
            fill_ref[blk] = FILL_NONE
            tbls = [jnp.broadcast_to(tbl_ref[h:h + 1, :], (sub, LANES)) for h in range(DA_HEADS)]
            for t in range(tq // LANES):
                ln = slice(t * LANES, (t + 1) * LANES)
                rel = poskb_ref[rows, :] - posq[:, ln]
                bucket = _bucket(rel)
                for h in range(DA_HEADS):
                    bias_ref[h, rows, ln] = jnp.take_along_axis(
                        tbls[h], bucket, axis=1, mode="promise_in_bounds")

    def scores(c, h):
        parts, m_c = [], None
        for u in range(tk // ATT_QK):
            rows = slice(c * tk + u * ATT_QK, c * tk + (u + 1) * ATT_QK)
            bias = bias_ref[h, rows, :]
            s = (lax.dot_general(k_ref[rows, h * DA_V_DIM:(h + 1) * DA_V_DIM], qz[h], nt,
                                 preferred_element_type=F32)
                 + jnp.concatenate([bias, bias], axis=1))
            m_u = jnp.max(s, axis=0, keepdims=True)
            m_c = m_u if m_c is None else jnp.maximum(m_c, m_u)
            parts.append(s)
        return parts, m_c

    items = [(c, h) for c in range(SEQ // tk) for h in range(DA_HEADS)]
    ahead = scores(*items[0])
    for n, (c, h) in enumerate(items):
        s_parts, m_c = ahead
        if n + 1 < len(items):
            ahead = scores(*items[n + 1])
        vt = vt_ref[h * VT_ROWS:(h + 1) * VT_ROWS, c * tk:(c + 1) * tk]
        if c == 0:
            m_new = m_c
        else:
            m_old, acc_old = state[h]
            m_new = jnp.maximum(m_old, m_c)
            alpha = jnp.exp2(m_old - m_new)
        p = jnp.concatenate([jnp.exp2(s - m_new).astype(BF16) for s in s_parts], axis=0)
        acc_new = jnp.dot(vt, p, preferred_element_type=F32)
        if c > 0:
            acc_new = alpha * acc_old + acc_new
        state[h] = (m_new, acc_new)

    for h in range(DA_HEADS):
        _, acc = state[h]
        l_fin = acc[DA_V_DIM:DA_V_DIM + 1, :]
        acc = acc[:DA_V_DIM, :]
        r1 = 1.0 / l_fin[:, :tq]
        r2 = lam / l_fin[:, tq:]
        o = (acc[:, :tq] * r1 - acc[:, tq:] * r2).T
        o = _rms(o, sg_ref[...], SUBLN_EPS) * (1.0 - LAMBDA_INIT)
        o_ref[:, h * DA_V_DIM:(h + 1) * DA_V_DIM] = o.astype(BF16)


def _attention(q, k, v, positions, rel_bias, lam_rows, subln_g):
    tq, sub = ATT_TQ, ATT_SUB
    nq = SEQ // tq
    nsub = SEQ // sub
    log2e = math.log2(math.e)
    tbl_t = rel_bias.T.astype(F32) * log2e
    tbl = jnp.zeros((8, LANES), F32).at[:DA_HEADS, :REL_BUCKETS].set(tbl_t)
    nb = REL_BUCKETS // 2
    far = jnp.stack([tbl_t[:, nb - 1], tbl_t[:, 2 * nb - 1]])
    pk = positions.reshape(BATCH, nsub, sub)
    pq = positions.reshape(BATCH, nq, tq)
    posk = positions.reshape(N_TOK, 1)
    posq = positions.reshape(BATCH * nq, 1, tq)
    smem = pl.BlockSpec(memory_space=pltpu.SMEM)
    grid_spec = pltpu.PrefetchScalarGridSpec(
        num_scalar_prefetch=4,
        grid=(BATCH, nq),
        in_specs=[
            pl.BlockSpec((tq, QK_WIDTH), lambda b, i, *_: (b * nq + i, 0)),
            pl.BlockSpec((SEQ, QK_WIDTH), lambda b, i, *_: (b, 0)),
            pl.BlockSpec((SEQ, V_WIDTH), lambda b, i, *_: (b, 0)),
            pl.BlockSpec((SEQ, 1), lambda b, i, *_: (b, 0)),
            pl.BlockSpec((1, 1, tq), lambda b, i, *_: (b * nq + i, 0, 0)),
            _resident((8, LANES)),
            smem,
            _resident((4, DA_HEAD_DIM)),
            _resident((1, DA_V_DIM)),
        ],
        out_specs=pl.BlockSpec((tq, V_WIDTH), lambda b, i, *_: (b * nq + i, 0)),
        scratch_shapes=[
            pltpu.VMEM((DA_HEADS * VT_ROWS, SEQ), BF16),
            pltpu.VMEM((SEQ, LANES), jnp.int32),
            pltpu.VMEM((DA_HEADS, SEQ, tq), F32),
            pltpu.SMEM((SEQ // ATT_SUB,), jnp.int32),
        ],
    )
    return pl.pallas_call(
        _attn_kernel,
        grid_spec=grid_spec,
        out_shape=jax.ShapeDtypeStruct((N_TOK, V_WIDTH), BF16),
        compiler_params=_params(("arbitrary", "arbitrary")),
        name="diff_attn",
    )(pk.min(-1), pk.max(-1), pq.min(-1), pq.max(-1),
      q, k, v, posk, posq, tbl, far, lam_rows, subln_g)


def _mix_ffn_kernel(x_ref, ya_ref, o_ref, ga_ref, gb_ref, wa_ref, wo_ref,
                    g_ref, wg_ref, wu_ref, wd_ref, fn_ref, out_ref):
    yb = jnp.dot(o_ref[...], wa_ref[...], preferred_element_type=F32)
    merged = (ga_ref[...].astype(F32) * ya_ref[...].astype(F32)
              + gb_ref[...].astype(F32) * yb).astype(BF16)
    x2 = x_ref[...] + jnp.dot(merged, wo_ref[...], preferred_element_type=F32)
    y = _half_swiglu_residual(x2, g_ref[...], wg_ref, wu_ref, wd_ref)
    out_ref[...] = _rms(y, fn_ref[...], NORM_EPS)


def _mix_ffn(x, ya, o, ga, gb, w_ao, w_out, g, wg, wu, wd, fn):
    tm = FFN_TM
    row = lambda w: pl.BlockSpec((tm, w), lambda i: (i, 0))
    return pl.pallas_call(
        _mix_ffn_kernel,
        grid=(N_TOK // tm,),
        in_specs=[row(D_MODEL), row(D_MODEL), row(V_WIDTH), row(D_MODEL), row(D_MODEL),
                  _resident((V_WIDTH, D_MODEL)), _resident((D_MODEL, D_MODEL))]
                 + _ffn_specs() + [_resident((1, D_MODEL))],
        out_specs=row(D_MODEL),
        out_shape=jax.ShapeDtypeStruct((N_TOK, D_MODEL), F32),
        compiler_params=_params(("arbitrary",)),
        name="mix_ffn",
    )(x, ya, o, ga, gb, w_ao, w_out, g, wg, wu, wd, fn)


def kernel(x, positions, rel_bias, ffn1_norm, ffn1_wg, ffn1_wu, ffn1_wd, mix_norm, w_in,
           lambda_q1, lambda_k1, lambda_q2, lambda_k2, subln_g, w_fourier_out, w_attn_out,
           w_out, ffn2_norm, ffn2_wg, ffn2_wu, ffn2_wd, final_norm):
    assert x.shape == (BATCH, SEQ, D_MODEL) and positions.shape == (BATCH, SEQ)
    bf = lambda w: w.astype(BF16)
    row = lambda g: g.reshape(1, -1).astype(F32)
    seq_mat, chan_mat = _dft_constants()
    lam_rows = jnp.concatenate([lambda_q1, lambda_k1, lambda_q2, lambda_k2], axis=0).astype(F32)

    xt = x.reshape(N_TOK, D_MODEL)
    x1, uf, q, k, v, ga, gb = _ffn_proj(xt, row(ffn1_norm[0]), bf(ffn1_wg[0]), bf(ffn1_wu[0]),
                                        bf(ffn1_wd[0]), row(mix_norm[0]), bf(w_in[0]))
    ya = _fourier(uf, seq_mat, chan_mat, bf(w_fourier_out[0]))
    o = _attention(q, k, v, positions.astype(jnp.int32), rel_bias, lam_rows, row(subln_g[0]))
    out = _mix_ffn(x1, ya, o, ga, gb, bf(w_attn_out[0]), bf(w_out[0]),
                   row(ffn2_norm[0]), bf(ffn2_wg[0]), bf(ffn2_wu[0]), bf(ffn2_wd[0]),
                   row(final_norm))
    return out.reshape(BATCH, SEQ, D_MODEL)
```

```python
import math

import numpy as np
import jax
import jax.numpy as jnp
from jax import lax
from jax.experimental import pallas as pl
from jax.experimental.pallas import tpu as pltpu

D_MODEL = 1024
BATCH = 8
SEQ = 2048
D_FF = 2816
F_GROUPS = 4
F_GROUP_CH = 128
F_WIDTH = F_GROUPS * F_GROUP_CH
DA_HEADS = 4
DA_HEAD_DIM = 64
DA_V_DIM = 2 * DA_HEAD_DIM
QK_WIDTH = DA_HEADS * 2 * DA_HEAD_DIM
V_WIDTH = DA_HEADS * DA_V_DIM
IN_WIDTH = F_WIDTH + 2 * QK_WIDTH + V_WIDTH + 2 * D_MODEL
REL_BUCKETS = 32
REL_MAX_DIST = 128
NORM_EPS = 1e-6
SUBLN_EPS = 1e-5
LAMBDA_INIT = 0.8 - 0.6 * math.exp(-0.3 * 0)

N_TOK = BATCH * SEQ
LANES = 128
VMEM_LIMIT = 56 * 1024 * 1024

FFN_TM = 512
SEQ_HALF_ROWS = SEQ // 2 + 16
ATT_TQ = 256
ATT_TK = 1024
ATT_SUB = 128
ATT_QK = 256
VT_ROWS = DA_V_DIM + 16
FILL_NONE, FILL_LEFT, FILL_RIGHT = 0, 1, 2
CAST_SLABS = 16

BF16 = jnp.bfloat16
F32 = jnp.float32


def _rms(x, g, eps):
    return x * lax.rsqrt(jnp.mean(x * x, axis=-1, keepdims=True) + eps) * g


def _resident(shape):
    return pl.BlockSpec(shape, lambda *_: (0,) * len(shape), pipeline_mode=pl.Buffered(1))


def _params(sem):
    return pltpu.CompilerParams(dimension_semantics=sem, vmem_limit_bytes=VMEM_LIMIT)


def _half_swiglu_residual(x, g, wg_ref, wu_ref, wd_ref):
    h = _rms(x, g, NORM_EPS).astype(BF16)
    gate = jnp.dot(h, wg_ref[...], preferred_element_type=F32)
    up = jnp.dot(h, wu_ref[...], preferred_element_type=F32)
    a = (gate * jax.nn.sigmoid(gate) * up).astype(BF16)
    return x + 0.5 * jnp.dot(a, wd_ref[...], preferred_element_type=F32)


def _ffn_specs():
    return [_resident((1, D_MODEL)), _resident((D_MODEL, D_FF)), _resident((D_MODEL, D_FF)),
            _resident((D_FF, D_MODEL))]


def _ffn_proj_kernel(x_ref, g_ref, wg_ref, wu_ref, wd_ref, gm_ref, w_ref,
                     x1_ref, uf_ref, q_ref, k_ref, v_ref, ga_ref, gb_ref):
    x1 = _half_swiglu_residual(x_ref[...], g_ref[...], wg_ref, wu_ref, wd_ref)
    x1_ref[...] = x1
    h = _rms(x1, gm_ref[...], NORM_EPS).astype(BF16)
    p = jnp.dot(h, w_ref[...], preferred_element_type=F32)
    c = 0
    uf_ref[...] = p[:, c:c + F_WIDTH].astype(BF16)
    c += F_WIDTH
    q_ref[...] = (p[:, c:c + QK_WIDTH] * (DA_HEAD_DIM ** -0.5 * math.log2(math.e))).astype(BF16)
    c += QK_WIDTH
    k_ref[...] = p[:, c:c + QK_WIDTH].astype(BF16)
    c += QK_WIDTH
    v_ref[...] = p[:, c:c + V_WIDTH].astype(BF16)
    c += V_WIDTH
    ga_ref[...] = jax.nn.sigmoid(p[:, c:c + D_MODEL]).astype(BF16)
    c += D_MODEL
    gb_ref[...] = jax.nn.sigmoid(p[:, c:c + D_MODEL]).astype(BF16)


def _ffn_proj(x, g, wg, wu, wd, g_mix, w_in):
    tm = FFN_TM
    row = lambda w: pl.BlockSpec((tm, w), lambda i: (i, 0))
    widths = (F_WIDTH, QK_WIDTH, QK_WIDTH, V_WIDTH, D_MODEL, D_MODEL)
    return pl.pallas_call(
        _ffn_proj_kernel,
        grid=(N_TOK // tm,),
        in_specs=[row(D_MODEL)] + _ffn_specs() + [_resident((1, D_MODEL)),
                                                  _resident((D_MODEL, IN_WIDTH))],
        out_specs=[row(D_MODEL)] + [row(w) for w in widths],
        out_shape=[jax.ShapeDtypeStruct((N_TOK, D_MODEL), F32)]
                  + [jax.ShapeDtypeStruct((N_TOK, w), BF16) for w in widths],
        compiler_params=_params(("arbitrary",)),
        name="ffn_proj",
    )(x, g, wg, wu, wd, g_mix, w_in)


def _dft_constants():
    k = np.arange(SEQ_HALF_ROWS, dtype=np.int64)[:, None]
    n = np.arange(SEQ, dtype=np.int64)[None, :]
    phase = (k * n) % SEQ
    live = (k <= SEQ // 2)
    cos_h = np.where(live, np.cos(2.0 * np.pi * phase / SEQ), 0.0)
    sin_h = np.where(live & (phase % (SEQ // 2) != 0), np.sin(2.0 * np.pi * phase / SEQ), 0.0)
    seq_mat = np.concatenate([cos_h, sin_h], axis=1)
    c = np.arange(F_GROUP_CH, dtype=np.int64)
    angc = 2.0 * np.pi * ((c[:, None] * c[None, :]) % F_GROUP_CH) / F_GROUP_CH
    scale = 1.0 / math.sqrt(SEQ * F_GROUP_CH)
    eye = np.eye(F_GROUPS)
    chan = np.concatenate([np.kron(eye, np.cos(angc)), -np.kron(eye, np.sin(angc))], axis=1) * scale
    return jnp.asarray(seq_mat, dtype=BF16), jnp.asarray(chan, dtype=BF16)


def _flip_rows(x):
    rows, cols = x.shape
    idx = 7 - lax.broadcasted_iota(jnp.int32, (8, cols), 0)
    groups = [jnp.take_along_axis(x[g * 8:(g + 1) * 8, :], idx, axis=0)
              for g in reversed(range(rows // 8))]
    return jnp.concatenate(groups, axis=0)


def _fourier_kernel(u_ref, seq_ref, chan_ref, w_ref, o_ref, z_ref, d_ref):
    u = u_ref[...]
    z_ref[0:SEQ, :] = jnp.dot(u, chan_ref[:, 0:F_WIDTH],
                              preferred_element_type=F32).astype(BF16)
    z_ref[SEQ:2 * SEQ, :] = jnp.dot(u, chan_ref[:, F_WIDTH:2 * F_WIDTH],
                                    preferred_element_type=F32).astype(BF16)
    half = SEQ // 2
    pc = jnp.dot(seq_ref[:, 0:SEQ], z_ref[0:SEQ, :], preferred_element_type=F32)
    ps = jnp.dot(seq_ref[:, SEQ:2 * SEQ], z_ref[SEQ:2 * SEQ, :], preferred_element_type=F32)
    top = (pc + ps)[0:half, :]
    d_ref[...] = pc - ps
    bottom = _flip_rows(d_ref[1:half + 1, :])
    o_ref[0:half, :] = jnp.dot(top.astype(BF16), w_ref[...],
                               preferred_element_type=F32).astype(BF16)
    o_ref[half:SEQ, :] = jnp.dot(bottom.astype(BF16), w_ref[...],
                                 preferred_element_type=F32).astype(BF16)


def _fourier(uf, seq_mat, chan_mat, w_fo):
    return pl.pallas_call(
        _fourier_kernel,
        grid=(BATCH,),
        in_specs=[
            pl.BlockSpec((SEQ, F_WIDTH), lambda b: (b, 0)),
            _resident((SEQ_HALF_ROWS, 2 * SEQ)),
            _resident((F_WIDTH, 2 * F_WIDTH)),
            _resident((F_WIDTH, D_MODEL)),
        ],
        out_specs=pl.BlockSpec((SEQ, D_MODEL), lambda b: (b, 0)),
        out_shape=jax.ShapeDtypeStruct((N_TOK, D_MODEL), BF16),
        scratch_shapes=[pltpu.VMEM((2 * SEQ, F_WIDTH), BF16),
                        pltpu.VMEM((SEQ_HALF_ROWS, F_WIDTH), F32)],
        compiler_params=_params(("arbitrary",)),
        name="fourier",
    )(uf, seq_mat, chan_mat, w_fo)


def _bucket(rel):
    nb = REL_BUCKETS // 2
    max_exact = nb // 2
    n = jnp.minimum(jnp.abs(rel), REL_MAX_DIST)
    nf = n.astype(F32)
    expo = lax.shift_right_logical(lax.bitcast_convert_type(nf * nf, jnp.int32), 23) - 127
    large = jnp.minimum(expo + 2, nb - 1)
    return jnp.where(rel > 0, nb, 0) + jnp.where(n < max_exact, n, large)


def _attn_kernel(kmin_ref, kmax_ref, qmin_ref, qmax_ref,
                 q_ref, k_ref, v_ref, posk_ref, posq_ref, tbl_ref, far_ref, lam_ref, sg_ref,
                 w32_0, w32_1, w32_2, w32_3, w32_4,
                 o_ref, w16_0, w16_1, w16_2, w16_3, w16_4,
                 vt_ref, poskb_ref, bias_ref, fill_ref):
    b = pl.program_id(0)
    i = pl.program_id(1)
    tq, tk, sub = ATT_TQ, ATT_TK, ATT_SUB

    for src, dst in ((w32_0, w16_0), (w32_1, w16_1), (w32_2, w16_2), (w32_3, w16_3), (w32_4, w16_4)):
        dst[...] = src[...].astype(BF16)

    @pl.when(i == 0)
    def _():
        for h in range(DA_HEADS):
            for c in range(SEQ // tk):
                blk = v_ref[c * tk:(c + 1) * tk, h * DA_V_DIM:(h + 1) * DA_V_DIM]
                vt_ref[h * VT_ROWS:h * VT_ROWS + DA_V_DIM, c * tk:(c + 1) * tk] = (
                    blk.astype(F32).T.astype(BF16))
            vt_ref[h * VT_ROWS + DA_V_DIM:(h + 1) * VT_ROWS, :] = jnp.ones(
                (VT_ROWS - DA_V_DIM, SEQ), BF16)
        poskb_ref[...] = jnp.broadcast_to(posk_ref[...], (SEQ, LANES))

    @pl.when(jnp.logical_and(b == 0, i == 0))
    def _():
        for blk in range(SEQ // sub):
            fill_ref[blk] = FILL_NONE

    lq1, lk1, lq2, lk2 = (lam_ref[r:r + 1, :] for r in range(4))
    lam = (jnp.exp(jnp.sum(lq1 * lk1, axis=-1, keepdims=True))
           - jnp.exp(jnp.sum(lq2 * lk2, axis=-1, keepdims=True)) + LAMBDA_INIT)

    posq = posq_ref[0]
    qmin = qmin_ref[b, i]
    qmax = qmax_ref[b, i]
    lane = lax.broadcasted_iota(jnp.int32, (tq, DA_V_DIM), 1)
    first_map = lane < DA_HEAD_DIM
    nt = (((1,), (1,)), ((), ()))

    qz = []
    for h in range(DA_HEADS):
        qh = q_ref[:, h * DA_V_DIM:(h + 1) * DA_V_DIM]
        zero = jnp.zeros_like(qh)
        qz.append(jnp.concatenate([jnp.where(first_map, qh, zero),
                                   jnp.where(first_map, zero, qh)], axis=0))
    state = [None] * DA_HEADS

    for blk in range(SEQ // sub):
        rows = slice(blk * sub, (blk + 1) * sub)
        far_right = kmin_ref[b, blk] - qmax >= REL_MAX_DIST
        far_left = qmin - kmax_ref[b, blk] >= REL_MAX_DIST
        far = jnp.logical_or(far_right, far_left)

        code = jnp.where(far_right, FILL_RIGHT, FILL_LEFT)

        @pl.when(jnp.logical_and(far, fill_ref[blk] != code))
        def _(rows=rows, blk=blk, far_right=far_right, code=code):
            for h in range(DA_HEADS):
                cst = jnp.where(far_right, far_ref[1, h], far_ref[0, h])
                bias_ref[h, rows, :] = jnp.full((sub, tq), cst, F32)
            fill_ref[blk] = code

        @pl.when(jnp.logical_not(far))
        def _(rows=rows, blk=blk):
            fill_ref[blk] = FILL_NONE
            tbls = [jnp.broadcast_to(tbl_ref[h:h + 1, :], (sub, LANES)) for h in range(DA_HEADS)]
            for t in range(tq // LANES):
                ln = slice(t * LANES, (t + 1) * LANES)
                rel = poskb_ref[rows, :] - posq[:, ln]
                bucket = _bucket(rel)
                for h in range(DA_HEADS):
                    bias_ref[h, rows, ln] = jnp.take_along_axis(
                        tbls[h], bucket, axis=1, mode="promise_in_bounds")

    def scores(c, h):
        parts, m_c = [], None
        for u in range(tk // ATT_QK):
            rows = slice(c * tk + u * ATT_QK, c * tk + (u + 1) * ATT_QK)
            bias = bias_ref[h, rows, :]
            s = (lax.dot_general(k_ref[rows, h * DA_V_DIM:(h + 1) * DA_V_DIM], qz[h], nt,
                                 preferred_element_type=F32)
                 + jnp.concatenate([bias, bias], axis=1))
            m_u = jnp.max(s, axis=0, keepdims=True)
            m_c = m_u if m_c is None else jnp.maximum(m_c, m_u)
            parts.append(s)
        return parts, m_c

    items = [(c, h) for c in range(SEQ // tk) for h in range(DA_HEADS)]
    ahead = scores(*items[0])
    for n, (c, h) in enumerate(items):
        s_parts, m_c = ahead
        if n + 1 < len(items):
            ahead = scores(*items[n + 1])
        vt = vt_ref[h * VT_ROWS:(h + 1) * VT_ROWS, c * tk:(c + 1) * tk]
        if c == 0:
            m_new = m_c
        else:
            m_old, acc_old = state[h]
            m_new = jnp.maximum(m_old, m_c)
            alpha = jnp.exp2(m_old - m_new)
        p = jnp.concatenate([jnp.exp2(s - m_new).astype(BF16) for s in s_parts], axis=0)
        acc_new = jnp.dot(vt, p, preferred_element_type=F32)
        if c > 0:
            acc_new = alpha * acc_old + acc_new
        state[h] = (m_new, acc_new)

    for h in range(DA_HEADS):
        _, acc = state[h]
        l_fin = acc[DA_V_DIM:DA_V_DIM + 1, :]
        acc = acc[:DA_V_DIM, :]
        r1 = 1.0 / l_fin[:, :tq]
        r2 = lam / l_fin[:, tq:]
        o = (acc[:, :tq] * r1 - acc[:, tq:] * r2).T
        o = _rms(o, sg_ref[...], SUBLN_EPS) * (1.0 - LAMBDA_INIT)
        o_ref[:, h * DA_V_DIM:(h + 1) * DA_V_DIM] = o.astype(BF16)


def _attention(q, k, v, positions, rel_bias, lam_rows, subln_g, later_weights):
    tq, sub = ATT_TQ, ATT_SUB
    nq = SEQ // tq
    nsub = SEQ // sub
    steps_per_slab = BATCH * nq // CAST_SLABS
    slab_specs = [pl.BlockSpec((w.shape[0] // CAST_SLABS, w.shape[1]),
                               lambda b, i, *_: ((b * nq + i) // steps_per_slab, 0))
                  for w in later_weights]
    log2e = math.log2(math.e)
    tbl_t = rel_bias.T.astype(F32) * log2e
    tbl = jnp.zeros((8, LANES), F32).at[:DA_HEADS, :REL_BUCKETS].set(tbl_t)
    nb = REL_BUCKETS // 2
    far = jnp.stack([tbl_t[:, nb - 1], tbl_t[:, 2 * nb - 1]])
    pk = positions.reshape(BATCH, nsub, sub)
    pq = positions.reshape(BATCH, nq, tq)
    posk = positions.reshape(N_TOK, 1)
    posq = positions.reshape(BATCH * nq, 1, tq)
    smem = pl.BlockSpec(memory_space=pltpu.SMEM)
    grid_spec = pltpu.PrefetchScalarGridSpec(
        num_scalar_prefetch=4,
        grid=(BATCH, nq),
        in_specs=[
            pl.BlockSpec((tq, QK_WIDTH), lambda b, i, *_: (b * nq + i, 0)),
            pl.BlockSpec((SEQ, QK_WIDTH), lambda b, i, *_: (b, 0)),
            pl.BlockSpec((SEQ, V_WIDTH), lambda b, i, *_: (b, 0)),
            pl.BlockSpec((SEQ, 1), lambda b, i, *_: (b, 0)),
            pl.BlockSpec((1, 1, tq), lambda b, i, *_: (b * nq + i, 0, 0)),
            _resident((8, LANES)),
            smem,
            _resident((4, DA_HEAD_DIM)),
            _resident((1, DA_V_DIM)),
        ] + slab_specs,
        out_specs=[pl.BlockSpec((tq, V_WIDTH), lambda b, i, *_: (b * nq + i, 0))] + slab_specs,
        scratch_shapes=[
            pltpu.VMEM((DA_HEADS * VT_ROWS, SEQ), BF16),
            pltpu.VMEM((SEQ, LANES), jnp.int32),
            pltpu.VMEM((DA_HEADS, SEQ, tq), F32),
            pltpu.SMEM((SEQ // ATT_SUB,), jnp.int32),
        ],
    )
    return pl.pallas_call(
        _attn_kernel,
        grid_spec=grid_spec,
        out_shape=[jax.ShapeDtypeStruct((N_TOK, V_WIDTH), BF16)]
                  + [jax.ShapeDtypeStruct(w.shape, BF16) for w in later_weights],
        compiler_params=_params(("arbitrary", "arbitrary")),
        name="diff_attn",
    )(pk.min(-1), pk.max(-1), pq.min(-1), pq.max(-1),
      q, k, v, posk, posq, tbl, far, lam_rows, subln_g, *later_weights)


def _mix_ffn_kernel(x_ref, ya_ref, o_ref, ga_ref, gb_ref, wa_ref, wo_ref,
                    g_ref, wg_ref, wu_ref, wd_ref, fn_ref, out_ref):
    yb = jnp.dot(o_ref[...], wa_ref[...], preferred_element_type=F32)
    merged = (ga_ref[...].astype(F32) * ya_ref[...].astype(F32)
              + gb_ref[...].astype(F32) * yb).astype(BF16)
    x2 = x_ref[...] + jnp.dot(merged, wo_ref[...], preferred_element_type=F32)
    y = _half_swiglu_residual(x2, g_ref[...], wg_ref, wu_ref, wd_ref)
    out_ref[...] = _rms(y, fn_ref[...], NORM_EPS)


def _mix_ffn(x, ya, o, ga, gb, w_ao, w_out, g, wg, wu, wd, fn):
    tm = FFN_TM
    row = lambda w: pl.BlockSpec((tm, w), lambda i: (i, 0))
    return pl.pallas_call(
        _mix_ffn_kernel,
        grid=(N_TOK // tm,),
        in_specs=[row(D_MODEL), row(D_MODEL), row(V_WIDTH), row(D_MODEL), row(D_MODEL),
                  _resident((V_WIDTH, D_MODEL)), _resident((D_MODEL, D_MODEL))]
                 + _ffn_specs() + [_resident((1, D_MODEL))],
        out_specs=row(D_MODEL),
        out_shape=jax.ShapeDtypeStruct((N_TOK, D_MODEL), F32),
        compiler_params=_params(("arbitrary",)),
        name="mix_ffn",
    )(x, ya, o, ga, gb, w_ao, w_out, g, wg, wu, wd, fn)


def kernel(x, positions, rel_bias, ffn1_norm, ffn1_wg, ffn1_wu, ffn1_wd, mix_norm, w_in,
           lambda_q1, lambda_k1, lambda_q2, lambda_k2, subln_g, w_fourier_out, w_attn_out,
           w_out, ffn2_norm, ffn2_wg, ffn2_wu, ffn2_wd, final_norm):
    assert x.shape == (BATCH, SEQ, D_MODEL) and positions.shape == (BATCH, SEQ)
    bf = lambda w: w.astype(BF16)
    row = lambda g: g.reshape(1, -1).astype(F32)
    seq_mat, chan_mat = _dft_constants()
    lam_rows = jnp.concatenate([lambda_q1, lambda_k1, lambda_q2, lambda_k2], axis=0).astype(F32)

    xt = x.reshape(N_TOK, D_MODEL)
    x1, uf, q, k, v, ga, gb = _ffn_proj(xt, row(ffn1_norm[0]), bf(ffn1_wg[0]), bf(ffn1_wu[0]),
                                        bf(ffn1_wd[0]), row(mix_norm[0]), bf(w_in[0]))
    ya = _fourier(uf, seq_mat, chan_mat, bf(w_fourier_out[0]))
    later = [w[0].astype(F32) for w in (ffn2_wg, ffn2_wu, ffn2_wd, w_out, w_attn_out)]
    o, wg2, wu2, wd2, wo, wa = _attention(q, k, v, positions.astype(jnp.int32), rel_bias, lam_rows,
                                          row(subln_g[0]), later)
    out = _mix_ffn(x1, ya, o, ga, gb, wa, wo, row(ffn2_norm[0]), wg2, wu2, wd2, row(final_norm))
    return out.reshape(BATCH, SEQ, D_MODEL)
```

```python
import math

import numpy as np
import jax
import jax.numpy as jnp
from jax import lax
from jax.experimental import pallas as pl
from jax.experimental.pallas import tpu as pltpu

D_MODEL = 1024
BATCH = 8
SEQ = 2048
D_FF = 2816
F_GROUPS = 4
F_GROUP_CH = 128
F_WIDTH = F_GROUPS * F_GROUP_CH
DA_HEADS = 4
DA_HEAD_DIM = 64
DA_V_DIM = 2 * DA_HEAD_DIM
QK_WIDTH = DA_HEADS * 2 * DA_HEAD_DIM
V_WIDTH = DA_HEADS * DA_V_DIM
IN_WIDTH = F_WIDTH + 2 * QK_WIDTH + V_WIDTH + 2 * D_MODEL
REL_BUCKETS = 32
REL_MAX_DIST = 128
NORM_EPS = 1e-6
SUBLN_EPS = 1e-5
LAMBDA_INIT = 0.8 - 0.6 * math.exp(-0.3 * 0)

N_TOK = BATCH * SEQ
LANES = 128
VMEM_LIMIT = 56 * 1024 * 1024

FFN_TM = 512
SEQ_HALF_ROWS = SEQ // 2 + 16
ATT_TQ = 256
ATT_TK = 1024
ATT_SUB = 128
ATT_QK = 256
VT_ROWS = DA_V_DIM + 16
FILL_NONE, FILL_LEFT, FILL_RIGHT = 0, 1, 2
CAST_SLABS = 16
NEAR_SLOTS = (ATT_TQ + 2 * REL_MAX_DIST) // ATT_SUB

BF16 = jnp.bfloat16
F32 = jnp.float32


def _rms(x, g, eps):
    return x * lax.rsqrt(jnp.mean(x * x, axis=-1, keepdims=True) + eps) * g


def _resident(shape):
    return pl.BlockSpec(shape, lambda *_: (0,) * len(shape), pipeline_mode=pl.Buffered(1))


def _params(sem):
    return pltpu.CompilerParams(dimension_semantics=sem, vmem_limit_bytes=VMEM_LIMIT)


def _half_swiglu_residual(x, g, wg_ref, wu_ref, wd_ref):
    h = _rms(x, g, NORM_EPS).astype(BF16)
    gate = jnp.dot(h, wg_ref[...], preferred_element_type=F32)
    up = jnp.dot(h, wu_ref[...], preferred_element_type=F32)
    a = (gate * jax.nn.sigmoid(gate) * up).astype(BF16)
    return x + 0.5 * jnp.dot(a, wd_ref[...], preferred_element_type=F32)


def _ffn_specs():
    return [_resident((1, D_MODEL)), _resident((D_MODEL, D_FF)), _resident((D_MODEL, D_FF)),
            _resident((D_FF, D_MODEL))]


def _ffn_proj_kernel(x_ref, g_ref, wg_ref, wu_ref, wd_ref, gm_ref, w_ref,
                     x1_ref, uf_ref, q_ref, k_ref, v_ref, ga_ref, gb_ref):
    x1 = _half_swiglu_residual(x_ref[...], g_ref[...], wg_ref, wu_ref, wd_ref)
    x1_ref[...] = x1
    h = _rms(x1, gm_ref[...], NORM_EPS).astype(BF16)
    p = jnp.dot(h, w_ref[...], preferred_element_type=F32)
    c = 0
    uf_ref[...] = p[:, c:c + F_WIDTH].astype(BF16)
    c += F_WIDTH
    q_ref[...] = (p[:, c:c + QK_WIDTH] * (DA_HEAD_DIM ** -0.5 * math.log2(math.e))).astype(BF16)
    c += QK_WIDTH
    k_ref[...] = p[:, c:c + QK_WIDTH].astype(BF16)
    c += QK_WIDTH
    v_ref[...] = p[:, c:c + V_WIDTH].astype(BF16)
    c += V_WIDTH
    ga_ref[...] = jax.nn.sigmoid(p[:, c:c + D_MODEL]).astype(BF16)
    c += D_MODEL
    gb_ref[...] = jax.nn.sigmoid(p[:, c:c + D_MODEL]).astype(BF16)


def _ffn_proj(x, g, wg, wu, wd, g_mix, w_in):
    tm = FFN_TM
    row = lambda w: pl.BlockSpec((tm, w), lambda i: (i, 0))
    widths = (F_WIDTH, QK_WIDTH, QK_WIDTH, V_WIDTH, D_MODEL, D_MODEL)
    return pl.pallas_call(
        _ffn_proj_kernel,
        grid=(N_TOK // tm,),
        in_specs=[row(D_MODEL)] + _ffn_specs() + [_resident((1, D_MODEL)),
                                                  _resident((D_MODEL, IN_WIDTH))],
        out_specs=[row(D_MODEL)] + [row(w) for w in widths],
        out_shape=[jax.ShapeDtypeStruct((N_TOK, D_MODEL), F32)]
                  + [jax.ShapeDtypeStruct((N_TOK, w), BF16) for w in widths],
        compiler_params=_params(("arbitrary",)),
        name="ffn_proj",
    )(x, g, wg, wu, wd, g_mix, w_in)


def _dft_constants():
    k = np.arange(SEQ_HALF_ROWS, dtype=np.int64)[:, None]
    n = np.arange(SEQ, dtype=np.int64)[None, :]
    phase = (k * n) % SEQ
    live = (k <= SEQ // 2)
    cos_h = np.where(live, np.cos(2.0 * np.pi * phase / SEQ), 0.0)
    sin_h = np.where(live & (phase % (SEQ // 2) != 0), np.sin(2.0 * np.pi * phase / SEQ), 0.0)
    seq_mat = np.concatenate([cos_h, sin_h], axis=1)
    c = np.arange(F_GROUP_CH, dtype=np.int64)
    angc = 2.0 * np.pi * ((c[:, None] * c[None, :]) % F_GROUP_CH) / F_GROUP_CH
    scale = 1.0 / math.sqrt(SEQ * F_GROUP_CH)
    eye = np.eye(F_GROUPS)
    chan = np.concatenate([np.kron(eye, np.cos(angc)), -np.kron(eye, np.sin(angc))], axis=1) * scale
    return jnp.asarray(seq_mat, dtype=BF16), jnp.asarray(chan, dtype=BF16)


def _flip_rows(x):
    rows, cols = x.shape
    idx = 7 - lax.broadcasted_iota(jnp.int32, (8, cols), 0)
    groups = [jnp.take_along_axis(x[g * 8:(g + 1) * 8, :], idx, axis=0)
              for g in reversed(range(rows // 8))]
    return jnp.concatenate(groups, axis=0)


def _fourier_kernel(u_ref, seq_ref, chan_ref, w_ref, o_ref, z_ref, d_ref):
    u = u_ref[...]
    z_ref[0:SEQ, :] = jnp.dot(u, chan_ref[:, 0:F_WIDTH],
                              preferred_element_type=F32).astype(BF16)
    z_ref[SEQ:2 * SEQ, :] = jnp.dot(u, chan_ref[:, F_WIDTH:2 * F_WIDTH],
                                    preferred_element_type=F32).astype(BF16)
    half = SEQ // 2
    pc = jnp.dot(seq_ref[:, 0:SEQ], z_ref[0:SEQ, :], preferred_element_type=F32)
    ps = jnp.dot(seq_ref[:, SEQ:2 * SEQ], z_ref[SEQ:2 * SEQ, :], preferred_element_type=F32)
    top = (pc + ps)[0:half, :]
    d_ref[...] = pc - ps
    bottom = _flip_rows(d_ref[1:half + 1, :])
    o_ref[0:half, :] = jnp.dot(top.astype(BF16), w_ref[...],
                               preferred_element_type=F32).astype(BF16)
    o_ref[half:SEQ, :] = jnp.dot(bottom.astype(BF16), w_ref[...],
                                 preferred_element_type=F32).astype(BF16)


def _fourier(uf, seq_mat, chan_mat, w_fo):
    return pl.pallas_call(
        _fourier_kernel,
        grid=(BATCH,),
        in_specs=[
            pl.BlockSpec((SEQ, F_WIDTH), lambda b: (b, 0)),
            _resident((SEQ_HALF_ROWS, 2 * SEQ)),
            _resident((F_WIDTH, 2 * F_WIDTH)),
            _resident((F_WIDTH, D_MODEL)),
        ],
        out_specs=pl.BlockSpec((SEQ, D_MODEL), lambda b: (b, 0)),
        out_shape=jax.ShapeDtypeStruct((N_TOK, D_MODEL), BF16),
        scratch_shapes=[pltpu.VMEM((2 * SEQ, F_WIDTH), BF16),
                        pltpu.VMEM((SEQ_HALF_ROWS, F_WIDTH), F32)],
        compiler_params=_params(("arbitrary",)),
        name="fourier",
    )(uf, seq_mat, chan_mat, w_fo)


def _bucket(rel):
    nb = REL_BUCKETS // 2
    max_exact = nb // 2
    n = jnp.minimum(jnp.abs(rel), REL_MAX_DIST)
    nf = n.astype(F32)
    expo = lax.shift_right_logical(lax.bitcast_convert_type(nf * nf, jnp.int32), 23) - 127
    large = jnp.minimum(expo + 2, nb - 1)
    return jnp.where(rel > 0, nb, 0) + jnp.where(n < max_exact, n, large)


def _attn_kernel(kmin_ref, kmax_ref, qmin_ref, qmax_ref, same_ref,
                 q_ref, k_ref, v_ref, posk_ref, posq_ref, tbl_ref, far_ref, lam_ref, sg_ref,
                 w32_0, w32_1, w32_2, w32_3, w32_4,
                 o_ref, w16_0, w16_1, w16_2, w16_3, w16_4,
                 vt_ref, poskb_ref, bias_ref, fill_ref, cache_ref, cblk_ref):
    b = pl.program_id(0)
    i = pl.program_id(1)
    tq, tk, sub = ATT_TQ, ATT_TK, ATT_SUB

    for src, dst in ((w32_0, w16_0), (w32_1, w16_1), (w32_2, w16_2), (w32_3, w16_3), (w32_4, w16_4)):
        dst[...] = src[...].astype(BF16)

    @pl.when(i == 0)
    def _():
        for h in range(DA_HEADS):
            for c in range(SEQ // tk):
                blk = v_ref[c * tk:(c + 1) * tk, h * DA_V_DIM:(h + 1) * DA_V_DIM]
                vt_ref[h * VT_ROWS:h * VT_ROWS + DA_V_DIM, c * tk:(c + 1) * tk] = (
                    blk.astype(F32).T.astype(BF16))
            vt_ref[h * VT_ROWS + DA_V_DIM:(h + 1) * VT_ROWS, :] = jnp.ones(
                (VT_ROWS - DA_V_DIM, SEQ), BF16)
        poskb_ref[...] = jnp.broadcast_to(posk_ref[...], (SEQ, LANES))

    @pl.when(jnp.logical_and(b == 0, i == 0))
    def _():
        for blk in range(SEQ // sub):
            fill_ref[blk] = FILL_NONE

    lq1, lk1, lq2, lk2 = (lam_ref[r:r + 1, :] for r in range(4))
    lam = (jnp.exp(jnp.sum(lq1 * lk1, axis=-1, keepdims=True))
           - jnp.exp(jnp.sum(lq2 * lk2, axis=-1, keepdims=True)) + LAMBDA_INIT)

    posq = posq_ref[0]
    qmin = qmin_ref[b, i]
    qmax = qmax_ref[b, i]
    lane = lax.broadcasted_iota(jnp.int32, (tq, DA_V_DIM), 1)
    first_map = lane < DA_HEAD_DIM
    nt = (((1,), (1,)), ((), ()))

    qz = []
    for h in range(DA_HEADS):
        qh = q_ref[:, h * DA_V_DIM:(h + 1) * DA_V_DIM]
        zero = jnp.zeros_like(qh)
        qz.append(jnp.concatenate([jnp.where(first_map, qh, zero),
                                   jnp.where(first_map, zero, qh)], axis=0))
    state = [None] * DA_HEADS

    @pl.when(same_ref[b] == 0)
    def _():
        for j in range(NEAR_SLOTS):
            cblk_ref[i * NEAR_SLOTS + j] = -1

    slot = jnp.int32(0)
    for blk in range(SEQ // sub):
        rows = slice(blk * sub, (blk + 1) * sub)
        far_right = kmin_ref[b, blk] - qmax >= REL_MAX_DIST
        far_left = qmin - kmax_ref[b, blk] >= REL_MAX_DIST
        far = jnp.logical_or(far_right, far_left)
        near = jnp.logical_not(far)
        cacheable = slot < NEAR_SLOTS
        entry = i * NEAR_SLOTS + jnp.minimum(slot, NEAR_SLOTS - 1)
        hit = jnp.logical_and(jnp.logical_and(near, cacheable), cblk_ref[entry] == blk)

        code = jnp.where(far_right, FILL_RIGHT, FILL_LEFT)

        @pl.when(jnp.logical_and(far, fill_ref[blk] != code))
        def _(rows=rows, blk=blk, far_right=far_right, code=code):
            for h in range(DA_HEADS):
                cst = jnp.where(far_right, far_ref[1, h], far_ref[0, h])
                bias_ref[h, rows, :] = jnp.full((sub, tq), cst, F32)
            fill_ref[blk] = code

        @pl.when(hit)
        def _(rows=rows, blk=blk, entry=entry):
            fill_ref[blk] = FILL_NONE
            bias_ref[:, rows, :] = cache_ref[entry]

        @pl.when(jnp.logical_and(near, jnp.logical_not(hit)))
        def _(rows=rows, blk=blk, entry=entry, cacheable=cacheable):
            fill_ref[blk] = FILL_NONE
            tbls = [jnp.broadcast_to(tbl_ref[h:h + 1, :], (sub, LANES)) for h in range(DA_HEADS)]
            for t in range(tq // LANES):
                ln = slice(t * LANES, (t + 1) * LANES)
                rel = poskb_ref[rows, :] - posq[:, ln]
                bucket = _bucket(rel)
                for h in range(DA_HEADS):
                    bias_ref[h, rows, ln] = jnp.take_along_axis(
                        tbls[h], bucket, axis=1, mode="promise_in_bounds")

            @pl.when(cacheable)
            def _():
                cache_ref[entry] = bias_ref[:, rows, :]
                cblk_ref[entry] = blk

        slot = slot + near.astype(jnp.int32)

    def scores(c, h):
        parts, m_c = [], None
        for u in range(tk // ATT_QK):
            rows = slice(c * tk + u * ATT_QK, c * tk + (u + 1) * ATT_QK)
            bias = bias_ref[h, rows, :]
            s = (lax.dot_general(k_ref[rows, h * DA_V_DIM:(h + 1) * DA_V_DIM], qz[h], nt,
                                 preferred_element_type=F32)
                 + jnp.concatenate([bias, bias], axis=1))
            m_u = jnp.max(s, axis=0, keepdims=True)
            m_c = m_u if m_c is None else jnp.maximum(m_c, m_u)
            parts.append(s)
        return parts, m_c

    items = [(c, h) for c in range(SEQ // tk) for h in range(DA_HEADS)]
    ahead = scores(*items[0])
    for n, (c, h) in enumerate(items):
        s_parts, m_c = ahead
        if n + 1 < len(items):
            ahead = scores(*items[n + 1])
        vt = vt_ref[h * VT_ROWS:(h + 1) * VT_ROWS, c * tk:(c + 1) * tk]
        if c == 0:
            m_new = m_c
        else:
            m_old, acc_old = state[h]
            m_new = jnp.maximum(m_old, m_c)
            alpha = jnp.exp2(m_old - m_new)
        p = jnp.concatenate([jnp.exp2(s - m_new).astype(BF16) for s in s_parts], axis=0)
        acc_new = jnp.dot(vt, p, preferred_element_type=F32)
        if c > 0:
            acc_new = alpha * acc_old + acc_new
        state[h] = (m_new, acc_new)

    for h in range(DA_HEADS):
        _, acc = state[h]
        l_fin = acc[DA_V_DIM:DA_V_DIM + 1, :]
        acc = acc[:DA_V_DIM, :]
        r1 = 1.0 / l_fin[:, :tq]
        r2 = lam / l_fin[:, tq:]
        o = (acc[:, :tq] * r1 - acc[:, tq:] * r2).T
        o = _rms(o, sg_ref[...], SUBLN_EPS) * (1.0 - LAMBDA_INIT)
        o_ref[:, h * DA_V_DIM:(h + 1) * DA_V_DIM] = o.astype(BF16)


def _attention(q, k, v, positions, rel_bias, lam_rows, subln_g, later_weights):
    tq, sub = ATT_TQ, ATT_SUB
    nq = SEQ // tq
    nsub = SEQ // sub
    steps_per_slab = BATCH * nq // CAST_SLABS
    slab_specs = [pl.BlockSpec((w.shape[0] // CAST_SLABS, w.shape[1]),
                               lambda b, i, *_: ((b * nq + i) // steps_per_slab, 0))
                  for w in later_weights]
    log2e = math.log2(math.e)
    tbl_t = rel_bias.T.astype(F32) * log2e
    tbl = jnp.zeros((8, LANES), F32).at[:DA_HEADS, :REL_BUCKETS].set(tbl_t)
    nb = REL_BUCKETS // 2
    far = jnp.stack([tbl_t[:, nb - 1], tbl_t[:, 2 * nb - 1]])
    pk = positions.reshape(BATCH, nsub, sub)
    pq = positions.reshape(BATCH, nq, tq)
    posk = positions.reshape(N_TOK, 1)
    posq = positions.reshape(BATCH * nq, 1, tq)
    same_as_prev = jnp.concatenate([
        jnp.zeros((1,), jnp.int32),
        jnp.all(positions[1:] == positions[:-1], axis=1).astype(jnp.int32)])
    smem = pl.BlockSpec(memory_space=pltpu.SMEM)
    grid_spec = pltpu.PrefetchScalarGridSpec(
        num_scalar_prefetch=5,
        grid=(BATCH, nq),
        in_specs=[
            pl.BlockSpec((tq, QK_WIDTH), lambda b, i, *_: (b * nq + i, 0)),
            pl.BlockSpec((SEQ, QK_WIDTH), lambda b, i, *_: (b, 0)),
            pl.BlockSpec((SEQ, V_WIDTH), lambda b, i, *_: (b, 0)),
            pl.BlockSpec((SEQ, 1), lambda b, i, *_: (b, 0)),
            pl.BlockSpec((1, 1, tq), lambda b, i, *_: (b * nq + i, 0, 0)),
            _resident((8, LANES)),
            smem,
            _resident((4, DA_HEAD_DIM)),
            _resident((1, DA_V_DIM)),
        ] + slab_specs,
        out_specs=[pl.BlockSpec((tq, V_WIDTH), lambda b, i, *_: (b * nq + i, 0))] + slab_specs,
        scratch_shapes=[
            pltpu.VMEM((DA_HEADS * VT_ROWS, SEQ), BF16),
            pltpu.VMEM((SEQ, LANES), jnp.int32),
            pltpu.VMEM((DA_HEADS, SEQ, tq), F32),
            pltpu.SMEM((SEQ // ATT_SUB,), jnp.int32),
            pltpu.VMEM((nq * NEAR_SLOTS, DA_HEADS, sub, tq), F32),
            pltpu.SMEM((nq * NEAR_SLOTS,), jnp.int32),
        ],
    )
    return pl.pallas_call(
        _attn_kernel,
        grid_spec=grid_spec,
        out_shape=[jax.ShapeDtypeStruct((N_TOK, V_WIDTH), BF16)]
                  + [jax.ShapeDtypeStruct(w.shape, BF16) for w in later_weights],
        compiler_params=_params(("arbitrary", "arbitrary")),
        name="diff_attn",
    )(pk.min(-1), pk.max(-1), pq.min(-1), pq.max(-1), same_as_prev,
      q, k, v, posk, posq, tbl, far, lam_rows, subln_g, *later_weights)


def _mix_ffn_kernel(x_ref, ya_ref, o_ref, ga_ref, gb_ref, wa_ref, wo_ref,
                    g_ref, wg_ref, wu_ref, wd_ref, fn_ref, out_ref):
    yb = jnp.dot(o_ref[...], wa_ref[...], preferred_element_type=F32)
    merged = (ga_ref[...].astype(F32) * ya_ref[...].astype(F32)
              + gb_ref[...].astype(F32) * yb).astype(BF16)
    x2 = x_ref[...] + jnp.dot(merged, wo_ref[...], preferred_element_type=F32)
    y = _half_swiglu_residual(x2, g_ref[...], wg_ref, wu_ref, wd_ref)
    out_ref[...] = _rms(y, fn_ref[...], NORM_EPS)


def _mix_ffn(x, ya, o, ga, gb, w_ao, w_out, g, wg, wu, wd, fn):
    tm = FFN_TM
    row = lambda w: pl.BlockSpec((tm, w), lambda i: (i, 0))
    return pl.pallas_call(
        _mix_ffn_kernel,
        grid=(N_TOK // tm,),
        in_specs=[row(D_MODEL), row(D_MODEL), row(V_WIDTH), row(D_MODEL), row(D_MODEL),
                  _resident((V_WIDTH, D_MODEL)), _resident((D_MODEL, D_MODEL))]
                 + _ffn_specs() + [_resident((1, D_MODEL))],
        out_specs=row(D_MODEL),
        out_shape=jax.ShapeDtypeStruct((N_TOK, D_MODEL), F32),
        compiler_params=_params(("arbitrary",)),
        name="mix_ffn",
    )(x, ya, o, ga, gb, w_ao, w_out, g, wg, wu, wd, fn)


def kernel(x, positions, rel_bias, ffn1_norm, ffn1_wg, ffn1_wu, ffn1_wd, mix_norm, w_in,
           lambda_q1, lambda_k1, lambda_q2, lambda_k2, subln_g, w_fourier_out, w_attn_out,
           w_out, ffn2_norm, ffn2_wg, ffn2_wu, ffn2_wd, final_norm):
    assert x.shape == (BATCH, SEQ, D_MODEL) and positions.shape == (BATCH, SEQ)
    bf = lambda w: w.astype(BF16)
    row = lambda g: g.reshape(1, -1).astype(F32)
    seq_mat, chan_mat = _dft_constants()
    lam_rows = jnp.concatenate([lambda_q1, lambda_k1, lambda_q2, lambda_k2], axis=0).astype(F32)

    xt = x.reshape(N_TOK, D_MODEL)
    x1, uf, q, k, v, ga, gb = _ffn_proj(xt, row(ffn1_norm[0]), bf(ffn1_wg[0]), bf(ffn1_wu[0]),
                                        bf(ffn1_wd[0]), row(mix_norm[0]), bf(w_in[0]))
    ya = _fourier(uf, seq_mat, chan_mat, bf(w_fourier_out[0]))
    later = [w[0].astype(F32) for w in (ffn2_wg, ffn2_wu, ffn2_wd, w_out, w_attn_out)]
    o, wg2, wu2, wd2, wo, wa = _attention(q, k, v, positions.astype(jnp.int32), rel_bias, lam_rows,
                                          row(subln_g[0]), later)
    out = _mix_ffn(x1, ya, o, ga, gb, wa, wo, row(ffn2_norm[0]), wg2, wu2, wd2, row(final_norm))
    return out.reshape(BATCH, SEQ, D_MODEL)
```

```python
import math

import numpy as np
import jax
import jax.numpy as jnp
from jax import lax
from jax.experimental import pallas as pl
from jax.experimental.pallas import tpu as pltpu

D_MODEL = 1024
BATCH = 8
SEQ = 2048
D_FF = 2816
F_GROUPS = 4
F_GROUP_CH = 128
F_WIDTH = F_GROUPS * F_GROUP_CH
DA_HEADS = 4
DA_HEAD_DIM = 64
DA_V_DIM = 2 * DA_HEAD_DIM
QK_WIDTH = DA_HEADS * 2 * DA_HEAD_DIM
V_WIDTH = DA_HEADS * DA_V_DIM
IN_WIDTH = F_WIDTH + 2 * QK_WIDTH + V_WIDTH + 2 * D_MODEL
REL_BUCKETS = 32
REL_MAX_DIST = 128
NORM_EPS = 1e-6
SUBLN_EPS = 1e-5
LAMBDA_INIT = 0.8 - 0.6 * math.exp(-0.3 * 0)

N_TOK = BATCH * SEQ
LANES = 128
VMEM_LIMIT = 56 * 1024 * 1024

FFN_TM = 512
SEQ_HALF_ROWS = SEQ // 2 + 16
ATT_TQ = 256
ATT_TK = 1024
ATT_SUB = 128
ATT_QK = 256
VT_ROWS = DA_V_DIM + 16
FILL_NONE, FILL_LEFT, FILL_RIGHT = 0, 1, 2
CAST_SLABS = 16
NEAR_SLOTS = (ATT_TQ + 2 * REL_MAX_DIST) // ATT_SUB

BF16 = jnp.bfloat16
F32 = jnp.float32


def _rms(x, g, eps):
    return x * lax.rsqrt(jnp.mean(x * x, axis=-1, keepdims=True) + eps) * g


def _resident(shape):
    return pl.BlockSpec(shape, lambda *_: (0,) * len(shape), pipeline_mode=pl.Buffered(1))


def _params(sem):
    return pltpu.CompilerParams(dimension_semantics=sem, vmem_limit_bytes=VMEM_LIMIT)


ROW_HALVES = [slice(r * (FFN_TM // 2), (r + 1) * (FFN_TM // 2)) for r in range(2)]


def _gate_up(x, g, wg_ref, wu_ref):
    h = _rms(x, g, NORM_EPS).astype(BF16)
    return (jnp.dot(h, wg_ref[...], preferred_element_type=F32),
            jnp.dot(h, wu_ref[...], preferred_element_type=F32))


def _down_residual(x, gate, up, wd_ref):
    a = (gate * jax.nn.sigmoid(gate) * up).astype(BF16)
    return x + 0.5 * jnp.dot(a, wd_ref[...], preferred_element_type=F32)


def _ffn_specs():
    return [_resident((1, D_MODEL)), _resident((D_MODEL, D_FF)), _resident((D_MODEL, D_FF)),
            _resident((D_FF, D_MODEL))]


def _ffn_proj_kernel(x_ref, g_ref, wg_ref, wu_ref, wd_ref, gm_ref, w_ref,
                     x1_ref, uf_ref, q_ref, k_ref, v_ref, ga_ref, gb_ref):
    def proj(rows, x1):
        x1_ref[rows, :] = x1
        h = _rms(x1, gm_ref[...], NORM_EPS).astype(BF16)
        p = jnp.dot(h, w_ref[...], preferred_element_type=F32)
        c = 0
        uf_ref[rows, :] = p[:, c:c + F_WIDTH].astype(BF16)
        c += F_WIDTH
        q_ref[rows, :] = (p[:, c:c + QK_WIDTH]
                          * (DA_HEAD_DIM ** -0.5 * math.log2(math.e))).astype(BF16)
        c += QK_WIDTH
        k_ref[rows, :] = p[:, c:c + QK_WIDTH].astype(BF16)
        c += QK_WIDTH
        v_ref[rows, :] = p[:, c:c + V_WIDTH].astype(BF16)
        c += V_WIDTH
        ga_ref[rows, :] = jax.nn.sigmoid(p[:, c:c + D_MODEL]).astype(BF16)
        c += D_MODEL
        gb_ref[rows, :] = jax.nn.sigmoid(p[:, c:c + D_MODEL]).astype(BF16)

    xs = [x_ref[rows, :] for rows in ROW_HALVES]
    gu = [_gate_up(x, g_ref[...], wg_ref, wu_ref) for x in xs]
    x1 = [_down_residual(x, gate, up, wd_ref) for x, (gate, up) in zip(xs, gu)]
    for rows, x in zip(ROW_HALVES, x1):
        proj(rows, x)


def _ffn_proj(x, g, wg, wu, wd, g_mix, w_in):
    tm = FFN_TM
    row = lambda w: pl.BlockSpec((tm, w), lambda i: (i, 0))
    widths = (F_WIDTH, QK_WIDTH, QK_WIDTH, V_WIDTH, D_MODEL, D_MODEL)
    return pl.pallas_call(
        _ffn_proj_kernel,
        grid=(N_TOK // tm,),
        in_specs=[row(D_MODEL)] + _ffn_specs() + [_resident((1, D_MODEL)),
                                                  _resident((D_MODEL, IN_WIDTH))],
        out_specs=[row(D_MODEL)] + [row(w) for w in widths],
        out_shape=[jax.ShapeDtypeStruct((N_TOK, D_MODEL), F32)]
                  + [jax.ShapeDtypeStruct((N_TOK, w), BF16) for w in widths],
        compiler_params=_params(("arbitrary",)),
        name="ffn_proj",
    )(x, g, wg, wu, wd, g_mix, w_in)


def _dft_constants():
    k = np.arange(SEQ_HALF_ROWS, dtype=np.int64)[:, None]
    n = np.arange(SEQ, dtype=np.int64)[None, :]
    phase = (k * n) % SEQ
    live = (k <= SEQ // 2)
    cos_h = np.where(live, np.cos(2.0 * np.pi * phase / SEQ), 0.0)
    sin_h = np.where(live & (phase % (SEQ // 2) != 0), np.sin(2.0 * np.pi * phase / SEQ), 0.0)
    seq_mat = np.concatenate([cos_h, sin_h], axis=1)
    c = np.arange(F_GROUP_CH, dtype=np.int64)
    angc = 2.0 * np.pi * ((c[:, None] * c[None, :]) % F_GROUP_CH) / F_GROUP_CH
    scale = 1.0 / math.sqrt(SEQ * F_GROUP_CH)
    eye = np.eye(F_GROUPS)
    chan = np.concatenate([np.kron(eye, np.cos(angc)), -np.kron(eye, np.sin(angc))], axis=1) * scale
    return jnp.asarray(seq_mat, dtype=BF16), jnp.asarray(chan, dtype=BF16)


def _flip_rows(x):
    rows, cols = x.shape
    idx = 7 - lax.broadcasted_iota(jnp.int32, (8, cols), 0)
    groups = [jnp.take_along_axis(x[g * 8:(g + 1) * 8, :], idx, axis=0)
              for g in reversed(range(rows // 8))]
    return jnp.concatenate(groups, axis=0)


def _fourier_kernel(u_ref, seq_ref, chan_ref, w_ref, o_ref, z_ref, d_ref):
    u = u_ref[...]
    for g in range(F_GROUPS):
        cols = slice(g * F_GROUP_CH, (g + 1) * F_GROUP_CH)
        sin_cols = slice(F_WIDTH + g * F_GROUP_CH, F_WIDTH + (g + 1) * F_GROUP_CH)
        z_ref[0:SEQ, cols] = jnp.dot(u[:, cols], chan_ref[cols, cols],
                                     preferred_element_type=F32).astype(BF16)
        z_ref[SEQ:2 * SEQ, cols] = jnp.dot(u[:, cols], chan_ref[cols, sin_cols],
                                           preferred_element_type=F32).astype(BF16)
    half = SEQ // 2
    pc = jnp.dot(seq_ref[:, 0:SEQ], z_ref[0:SEQ, :], preferred_element_type=F32)
    ps = jnp.dot(seq_ref[:, SEQ:2 * SEQ], z_ref[SEQ:2 * SEQ, :], preferred_element_type=F32)
    top = (pc + ps)[0:half, :]
    d_ref[...] = pc - ps
    bottom = _flip_rows(d_ref[1:half + 1, :])
    o_ref[0:half, :] = jnp.dot(top.astype(BF16), w_ref[...],
                               preferred_element_type=F32).astype(BF16)
    o_ref[half:SEQ, :] = jnp.dot(bottom.astype(BF16), w_ref[...],
                                 preferred_element_type=F32).astype(BF16)


def _fourier(uf, seq_mat, chan_mat, w_fo):
    return pl.pallas_call(
        _fourier_kernel,
        grid=(BATCH,),
        in_specs=[
            pl.BlockSpec((SEQ, F_WIDTH), lambda b: (b, 0)),
            _resident((SEQ_HALF_ROWS, 2 * SEQ)),
            _resident((F_WIDTH, 2 * F_WIDTH)),
            _resident((F_WIDTH, D_MODEL)),
        ],
        out_specs=pl.BlockSpec((SEQ, D_MODEL), lambda b: (b, 0)),
        out_shape=jax.ShapeDtypeStruct((N_TOK, D_MODEL), BF16),
        scratch_shapes=[pltpu.VMEM((2 * SEQ, F_WIDTH), BF16),
                        pltpu.VMEM((SEQ_HALF_ROWS, F_WIDTH), F32)],
        compiler_params=_params(("arbitrary",)),
        name="fourier",
    )(uf, seq_mat, chan_mat, w_fo)


def _bucket(rel):
    nb = REL_BUCKETS // 2
    max_exact = nb // 2
    n = jnp.minimum(jnp.abs(rel), REL_MAX_DIST)
    nf = n.astype(F32)
    expo = lax.shift_right_logical(lax.bitcast_convert_type(nf * nf, jnp.int32), 23) - 127
    large = jnp.minimum(expo + 2, nb - 1)
    return jnp.where(rel > 0, nb, 0) + jnp.where(n < max_exact, n, large)


def _attn_kernel(kmin_ref, kmax_ref, qmin_ref, qmax_ref, same_ref,
                 q_ref, k_ref, v_ref, posk_ref, posq_ref, tbl_ref, far_ref, lam_ref, sg_ref,
                 w32_0, w32_1, w32_2, w32_3, w32_4,
                 o_ref, w16_0, w16_1, w16_2, w16_3, w16_4,
                 vt_ref, poskb_ref, bias_ref, fill_ref, cache_ref, cblk_ref):
    b = pl.program_id(0)
    i = pl.program_id(1)
    tq, tk, sub = ATT_TQ, ATT_TK, ATT_SUB

    @pl.when(i == 0)
    def _():
        for h in range(DA_HEADS):
            for c in range(SEQ // tk):
                blk = v_ref[c * tk:(c + 1) * tk, h * DA_V_DIM:(h + 1) * DA_V_DIM]
                vt_ref[h * VT_ROWS:h * VT_ROWS + DA_V_DIM, c * tk:(c + 1) * tk] = (
                    blk.astype(F32).T.astype(BF16))
            vt_ref[h * VT_ROWS + DA_V_DIM:(h + 1) * VT_ROWS, :] = jnp.ones(
                (VT_ROWS - DA_V_DIM, SEQ), BF16)
        poskb_ref[...] = jnp.broadcast_to(posk_ref[...], (SEQ, LANES))

    @pl.when(jnp.logical_and(b == 0, i == 0))
    def _():
        for blk in range(SEQ // sub):
            fill_ref[blk] = FILL_NONE

    lq1, lk1, lq2, lk2 = (lam_ref[r:r + 1, :] for r in range(4))
    lam = (jnp.exp(jnp.sum(lq1 * lk1, axis=-1, keepdims=True))
           - jnp.exp(jnp.sum(lq2 * lk2, axis=-1, keepdims=True)) + LAMBDA_INIT)

    posq = posq_ref[0]
    qmin = qmin_ref[b, i]
    qmax = qmax_ref[b, i]
    first_map = lax.broadcasted_iota(jnp.int32, (DA_V_DIM, tq), 0) < DA_HEAD_DIM

    qzt = []
    for h in range(DA_HEADS):
        qt = q_ref[:, h * DA_V_DIM:(h + 1) * DA_V_DIM].astype(F32).T
        zero = jnp.zeros_like(qt)
        qzt.append(jnp.concatenate([jnp.where(first_map, qt, zero),
                                    jnp.where(first_map, zero, qt)], axis=1).astype(BF16))
    state = [None] * DA_HEADS

    @pl.when(same_ref[b] == 0)
    def _():
        for j in range(NEAR_SLOTS):
            cblk_ref[i * NEAR_SLOTS + j] = -1

    slot = jnp.int32(0)
    for blk in range(SEQ // sub):
        rows = slice(blk * sub, (blk + 1) * sub)
        far_right = kmin_ref[b, blk] - qmax >= REL_MAX_DIST
        far_left = qmin - kmax_ref[b, blk] >= REL_MAX_DIST
        far = jnp.logical_or(far_right, far_left)
        near = jnp.logical_not(far)
        cacheable = slot < NEAR_SLOTS
        entry = i * NEAR_SLOTS + jnp.minimum(slot, NEAR_SLOTS - 1)
        hit = jnp.logical_and(jnp.logical_and(near, cacheable), cblk_ref[entry] == blk)

        code = jnp.where(far_right, FILL_RIGHT, FILL_LEFT)

        @pl.when(jnp.logical_and(far, fill_ref[blk] != code))
        def _(rows=rows, blk=blk, far_right=far_right, code=code):
            for h in range(DA_HEADS):
                cst = jnp.where(far_right, far_ref[1, h], far_ref[0, h])
                bias_ref[h, rows, :] = jnp.full((sub, tq), cst, F32)
            fill_ref[blk] = code

        @pl.when(hit)
        def _(rows=rows, blk=blk, entry=entry):
            fill_ref[blk] = FILL_NONE
            bias_ref[:, rows, :] = cache_ref[entry]

        @pl.when(jnp.logical_and(near, jnp.logical_not(hit)))
        def _(rows=rows, blk=blk, entry=entry, cacheable=cacheable):
            fill_ref[blk] = FILL_NONE
            tbls = [jnp.broadcast_to(tbl_ref[h:h + 1, :], (sub, LANES)) for h in range(DA_HEADS)]
            for t in range(tq // LANES):
                ln = slice(t * LANES, (t + 1) * LANES)
                rel = poskb_ref[rows, :] - posq[:, ln]
                bucket = _bucket(rel)
                for h in range(DA_HEADS):
                    bias_ref[h, rows, ln] = jnp.take_along_axis(
                        tbls[h], bucket, axis=1, mode="promise_in_bounds")

            @pl.when(cacheable)
            def _():
                cache_ref[entry] = bias_ref[:, rows, :]
                cblk_ref[entry] = blk

        slot = slot + near.astype(jnp.int32)

    def scores(c, h):
        parts, m_c = [], None
        for u in range(tk // ATT_QK):
            rows = slice(c * tk + u * ATT_QK, c * tk + (u + 1) * ATT_QK)
            bias = bias_ref[h, rows, :]
            s = (jnp.dot(k_ref[rows, h * DA_V_DIM:(h + 1) * DA_V_DIM], qzt[h],
                         preferred_element_type=F32)
                 + jnp.concatenate([bias, bias], axis=1))
            m_u = jnp.max(s, axis=0, keepdims=True)
            m_c = m_u if m_c is None else jnp.maximum(m_c, m_u)
            parts.append(s)
        return parts, m_c

    for src, dst in ((w32_0, w16_0), (w32_1, w16_1), (w32_2, w16_2), (w32_3, w16_3), (w32_4, w16_4)):
        dst[...] = src[...].astype(BF16)

    items = [(c, h) for c in range(SEQ // tk) for h in range(DA_HEADS)]
    ahead = scores(*items[0])
    for n, (c, h) in enumerate(items):
        s_parts, m_c = ahead
        if n + 1 < len(items):
            ahead = scores(*items[n + 1])
        vt = vt_ref[h * VT_ROWS:(h + 1) * VT_ROWS, c * tk:(c + 1) * tk]
        if c == 0:
            m_new = m_c
        else:
            m_old, acc_old = state[h]
            m_new = jnp.maximum(m_old, m_c)
            alpha = jnp.exp2(m_old - m_new)
        p = jnp.concatenate([jnp.exp2(s - m_new).astype(BF16) for s in s_parts], axis=0)
        acc_new = jnp.dot(vt, p, preferred_element_type=F32)
        if c > 0:
            acc_new = alpha * acc_old + acc_new
        state[h] = (m_new, acc_new)
        if c == SEQ // tk - 1:
            l_fin = acc_new[DA_V_DIM:DA_V_DIM + 1, :]
            acc = acc_new[:DA_V_DIM, :]
            r1 = 1.0 / l_fin[:, :tq]
            r2 = lam / l_fin[:, tq:]
            o = (acc[:, :tq] * r1 - acc[:, tq:] * r2).T
            o = _rms(o, sg_ref[...], SUBLN_EPS) * (1.0 - LAMBDA_INIT)
            o_ref[:, h * DA_V_DIM:(h + 1) * DA_V_DIM] = o.astype(BF16)


def _attention(q, k, v, positions, rel_bias, lam_rows, subln_g, later_weights):
    tq, sub = ATT_TQ, ATT_SUB
    nq = SEQ // tq
    nsub = SEQ // sub
    steps_per_slab = BATCH * nq // CAST_SLABS
    slab_specs = [pl.BlockSpec((w.shape[0] // CAST_SLABS, w.shape[1]),
                               lambda b, i, *_: ((b * nq + i) // steps_per_slab, 0))
                  for w in later_weights]
    log2e = math.log2(math.e)
    tbl_t = rel_bias.T.astype(F32) * log2e
    tbl = jnp.zeros((8, LANES), F32).at[:DA_HEADS, :REL_BUCKETS].set(tbl_t)
    nb = REL_BUCKETS // 2
    far = jnp.stack([tbl_t[:, nb - 1], tbl_t[:, 2 * nb - 1]])
    pk = positions.reshape(BATCH, nsub, sub)
    pq = positions.reshape(BATCH, nq, tq)
    posk = positions.reshape(N_TOK, 1)
    posq = positions.reshape(BATCH * nq, 1, tq)
    same_as_prev = jnp.concatenate([
        jnp.zeros((1,), jnp.int32),
        jnp.all(positions[1:] == positions[:-1], axis=1).astype(jnp.int32)])
    smem = pl.BlockSpec(memory_space=pltpu.SMEM)
    grid_spec = pltpu.PrefetchScalarGridSpec(
        num_scalar_prefetch=5,
        grid=(BATCH, nq),
        in_specs=[
            pl.BlockSpec((tq, QK_WIDTH), lambda b, i, *_: (b * nq + i, 0)),
            pl.BlockSpec((SEQ, QK_WIDTH), lambda b, i, *_: (b, 0)),
            pl.BlockSpec((SEQ, V_WIDTH), lambda b, i, *_: (b, 0)),
            pl.BlockSpec((SEQ, 1), lambda b, i, *_: (b, 0)),
            pl.BlockSpec((1, 1, tq), lambda b, i, *_: (b * nq + i, 0, 0)),
            _resident((8, LANES)),
            smem,
            _resident((4, DA_HEAD_DIM)),
            _resident((1, DA_V_DIM)),
        ] + slab_specs,
        out_specs=[pl.BlockSpec((tq, V_WIDTH), lambda b, i, *_: (b * nq + i, 0))] + slab_specs,
        scratch_shapes=[
            pltpu.VMEM((DA_HEADS * VT_ROWS, SEQ), BF16),
            pltpu.VMEM((SEQ, LANES), jnp.int32),
            pltpu.VMEM((DA_HEADS, SEQ, tq), F32),
            pltpu.SMEM((SEQ // ATT_SUB,), jnp.int32),
            pltpu.VMEM((nq * NEAR_SLOTS, DA_HEADS, sub, tq), F32),
            pltpu.SMEM((nq * NEAR_SLOTS,), jnp.int32),
        ],
    )
    return pl.pallas_call(
        _attn_kernel,
        grid_spec=grid_spec,
        out_shape=[jax.ShapeDtypeStruct((N_TOK, V_WIDTH), BF16)]
                  + [jax.ShapeDtypeStruct(w.shape, BF16) for w in later_weights],
        compiler_params=_params(("arbitrary", "arbitrary")),
        name="diff_attn",
    )(pk.min(-1), pk.max(-1), pq.min(-1), pq.max(-1), same_as_prev,
      q, k, v, posk, posq, tbl, far, lam_rows, subln_g, *later_weights)


def _mix_ffn_kernel(x_ref, ya_ref, o_ref, ga_ref, gb_ref, wa_ref, wo_ref,
                    g_ref, wg_ref, wu_ref, wd_ref, fn_ref, out_ref):
    def mix(rows, yb):
        merged = (ga_ref[rows, :].astype(F32) * ya_ref[rows, :].astype(F32)
                  + gb_ref[rows, :].astype(F32) * yb).astype(BF16)
        return x_ref[rows, :] + jnp.dot(merged, wo_ref[...], preferred_element_type=F32)

    yb = [jnp.dot(o_ref[rows, :], wa_ref[...], preferred_element_type=F32) for rows in ROW_HALVES]
    x2 = [mix(rows, y) for rows, y in zip(ROW_HALVES, yb)]
    gu = [_gate_up(x, g_ref[...], wg_ref, wu_ref) for x in x2]
    for rows, x, (gate, up) in zip(ROW_HALVES, x2, gu):
        y = _down_residual(x, gate, up, wd_ref)
        out_ref[rows, :] = _rms(y, fn_ref[...], NORM_EPS)


def _mix_ffn(x, ya, o, ga, gb, w_ao, w_out, g, wg, wu, wd, fn):
    tm = FFN_TM
    row = lambda w: pl.BlockSpec((tm, w), lambda i: (i, 0))
    return pl.pallas_call(
        _mix_ffn_kernel,
        grid=(N_TOK // tm,),
        in_specs=[row(D_MODEL), row(D_MODEL), row(V_WIDTH), row(D_MODEL), row(D_MODEL),
                  _resident((V_WIDTH, D_MODEL)), _resident((D_MODEL, D_MODEL))]
                 + _ffn_specs() + [_resident((1, D_MODEL))],
        out_specs=row(D_MODEL),
        out_shape=jax.ShapeDtypeStruct((N_TOK, D_MODEL), F32),
        compiler_params=_params(("arbitrary",)),
        name="mix_ffn",
    )(x, ya, o, ga, gb, w_ao, w_out, g, wg, wu, wd, fn)


def kernel(x, positions, rel_bias, ffn1_norm, ffn1_wg, ffn1_wu, ffn1_wd, mix_norm, w_in,
           lambda_q1, lambda_k1, lambda_q2, lambda_k2, subln_g, w_fourier_out, w_attn_out,
           w_out, ffn2_norm, ffn2_wg, ffn2_wu, ffn2_wd, final_norm):
    assert x.shape == (BATCH, SEQ, D_MODEL) and positions.shape == (BATCH, SEQ)
    bf = lambda w: w.astype(BF16)
    row = lambda g: g.reshape(1, -1).astype(F32)
    seq_mat, chan_mat = _dft_constants()
    lam_rows = jnp.concatenate([lambda_q1, lambda_k1, lambda_q2, lambda_k2], axis=0).astype(F32)

    xt = x.reshape(N_TOK, D_MODEL)
    x1, uf, q, k, v, ga, gb = _ffn_proj(xt, row(ffn1_norm[0]), bf(ffn1_wg[0]), bf(ffn1_wu[0]),
                                        bf(ffn1_wd[0]), row(mix_norm[0]), bf(w_in[0]))
    ya = _fourier(uf, seq_mat, chan_mat, bf(w_fourier_out[0]))
    later = [w[0].astype(F32) for w in (ffn2_wg, ffn2_wu, ffn2_wd, w_out, w_attn_out)]
    o, wg2, wu2, wd2, wo, wa = _attention(q, k, v, positions.astype(jnp.int32), rel_bias, lam_rows,
                                          row(subln_g[0]), later)
    out = _mix_ffn(x1, ya, o, ga, gb, wa, wo, row(ffn2_norm[0]), wg2, wu2, wd2, row(final_norm))
    return out.reshape(BATCH, SEQ, D_MODEL)
```

```python
import math

import numpy as np
import jax
import jax.numpy as jnp
from jax import lax
from jax.experimental import pallas as pl
from jax.experimental.pallas import tpu as pltpu

D_MODEL = 1024
BATCH = 8
SEQ = 2048
D_FF = 2816
F_GROUPS = 4
F_GROUP_CH = 128
F_WIDTH = F_GROUPS * F_GROUP_CH
DA_HEADS = 4
DA_HEAD_DIM = 64
DA_V_DIM = 2 * DA_HEAD_DIM
QK_WIDTH = DA_HEADS * 2 * DA_HEAD_DIM
V_WIDTH = DA_HEADS * DA_V_DIM
IN_WIDTH = F_WIDTH + 2 * QK_WIDTH + V_WIDTH + 2 * D_MODEL
REL_BUCKETS = 32
REL_MAX_DIST = 128
NORM_EPS = 1e-6
SUBLN_EPS = 1e-5
LAMBDA_INIT = 0.8 - 0.6 * math.exp(-0.3 * 0)

N_TOK = BATCH * SEQ
LANES = 128
VMEM_LIMIT = 56 * 1024 * 1024

FFN_TM = 512
SEQ_HALF_ROWS = SEQ // 2 + 16
ATT_TQ = 256
ATT_TILES = 2
ATT_TK = 1024
ATT_SUB = 128
ATT_QK = 256
VT_ROWS = DA_V_DIM + 16
FILL_NONE, FILL_LEFT, FILL_RIGHT = 0, 1, 2
CAST_SLABS = 16
NEAR_SLOTS = (ATT_TQ + 2 * REL_MAX_DIST) // ATT_SUB

BF16 = jnp.bfloat16
F32 = jnp.float32


def _rms(x, g, eps):
    return x * lax.rsqrt(jnp.mean(x * x, axis=-1, keepdims=True) + eps) * g


def _resident(shape):
    return pl.BlockSpec(shape, lambda *_: (0,) * len(shape), pipeline_mode=pl.Buffered(1))


def _params(sem):
    return pltpu.CompilerParams(dimension_semantics=sem, vmem_limit_bytes=VMEM_LIMIT)


ROW_HALVES = [slice(r * (FFN_TM // 2), (r + 1) * (FFN_TM // 2)) for r in range(2)]


def _gate_up(x, g, wg_ref, wu_ref):
    h = _rms(x, g, NORM_EPS).astype(BF16)
    return (jnp.dot(h, wg_ref[...], preferred_element_type=F32),
            jnp.dot(h, wu_ref[...], preferred_element_type=F32))


def _down_residual(x, gate, up, wd_ref):
    a = (gate * jax.nn.sigmoid(gate) * up).astype(BF16)
    return x + 0.5 * jnp.dot(a, wd_ref[...], preferred_element_type=F32)


def _ffn_specs():
    return [_resident((1, D_MODEL)), _resident((D_MODEL, D_FF)), _resident((D_MODEL, D_FF)),
            _resident((D_FF, D_MODEL))]


def _ffn_proj_kernel(x_ref, g_ref, wg_ref, wu_ref, wd_ref, gm_ref, w_ref,
                     x1_ref, uf_ref, q_ref, k_ref, v_ref, ga_ref, gb_ref):
    def proj(rows, x1):
        x1_ref[rows, :] = x1
        h = _rms(x1, gm_ref[...], NORM_EPS).astype(BF16)
        p = jnp.dot(h, w_ref[...], preferred_element_type=F32)
        c = 0
        uf_ref[rows, :] = p[:, c:c + F_WIDTH].astype(BF16)
        c += F_WIDTH
        q_ref[rows, :] = (p[:, c:c + QK_WIDTH]
                          * (DA_HEAD_DIM ** -0.5 * math.log2(math.e))).astype(BF16)
        c += QK_WIDTH
        k_ref[rows, :] = p[:, c:c + QK_WIDTH].astype(BF16)
        c += QK_WIDTH
        v_ref[rows, :] = p[:, c:c + V_WIDTH].astype(BF16)
        c += V_WIDTH
        ga_ref[rows, :] = jax.nn.sigmoid(p[:, c:c + D_MODEL]).astype(BF16)
        c += D_MODEL
        gb_ref[rows, :] = jax.nn.sigmoid(p[:, c:c + D_MODEL]).astype(BF16)

    xs = [x_ref[rows, :] for rows in ROW_HALVES]
    gu = [_gate_up(x, g_ref[...], wg_ref, wu_ref) for x in xs]
    x1 = [_down_residual(x, gate, up, wd_ref) for x, (gate, up) in zip(xs, gu)]
    for rows, x in zip(ROW_HALVES, x1):
        proj(rows, x)


def _ffn_proj(x, g, wg, wu, wd, g_mix, w_in):
    tm = FFN_TM
    row = lambda w: pl.BlockSpec((tm, w), lambda i: (i, 0))
    widths = (F_WIDTH, QK_WIDTH, QK_WIDTH, V_WIDTH, D_MODEL, D_MODEL)
    return pl.pallas_call(
        _ffn_proj_kernel,
        grid=(N_TOK // tm,),
        in_specs=[row(D_MODEL)] + _ffn_specs() + [_resident((1, D_MODEL)),
                                                  _resident((D_MODEL, IN_WIDTH))],
        out_specs=[row(D_MODEL)] + [row(w) for w in widths],
        out_shape=[jax.ShapeDtypeStruct((N_TOK, D_MODEL), F32)]
                  + [jax.ShapeDtypeStruct((N_TOK, w), BF16) for w in widths],
        compiler_params=_params(("arbitrary",)),
        name="ffn_proj",
    )(x, g, wg, wu, wd, g_mix, w_in)


def _dft_constants():
    k = np.arange(SEQ_HALF_ROWS, dtype=np.int64)[:, None]
    n = np.arange(SEQ, dtype=np.int64)[None, :]
    phase = (k * n) % SEQ
    live = (k <= SEQ // 2)
    cos_h = np.where(live, np.cos(2.0 * np.pi * phase / SEQ), 0.0)
    sin_h = np.where(live & (phase % (SEQ // 2) != 0), np.sin(2.0 * np.pi * phase / SEQ), 0.0)
    seq_mat = np.concatenate([cos_h, sin_h], axis=1)
    c = np.arange(F_GROUP_CH, dtype=np.int64)
    angc = 2.0 * np.pi * ((c[:, None] * c[None, :]) % F_GROUP_CH) / F_GROUP_CH
    scale = 1.0 / math.sqrt(SEQ * F_GROUP_CH)
    eye = np.eye(F_GROUPS)
    chan = np.concatenate([np.kron(eye, np.cos(angc)), -np.kron(eye, np.sin(angc))], axis=1) * scale
    return jnp.asarray(seq_mat, dtype=BF16), jnp.asarray(chan, dtype=BF16)


def _flip_rows(x):
    rows, cols = x.shape
    idx = 7 - lax.broadcasted_iota(jnp.int32, (8, cols), 0)
    groups = [jnp.take_along_axis(x[g * 8:(g + 1) * 8, :], idx, axis=0)
              for g in reversed(range(rows // 8))]
    return jnp.concatenate(groups, axis=0)


def _fourier_kernel(u_ref, seq_ref, chan_ref, w_ref, o_ref, z_ref, d_ref):
    u = u_ref[...]
    for g in range(F_GROUPS):
        cols = slice(g * F_GROUP_CH, (g + 1) * F_GROUP_CH)
        sin_cols = slice(F_WIDTH + g * F_GROUP_CH, F_WIDTH + (g + 1) * F_GROUP_CH)
        z_ref[0:SEQ, cols] = jnp.dot(u[:, cols], chan_ref[cols, cols],
                                     preferred_element_type=F32).astype(BF16)
        z_ref[SEQ:2 * SEQ, cols] = jnp.dot(u[:, cols], chan_ref[cols, sin_cols],
                                           preferred_element_type=F32).astype(BF16)
    half = SEQ // 2
    pc = jnp.dot(seq_ref[:, 0:SEQ], z_ref[0:SEQ, :], preferred_element_type=F32)
    ps = jnp.dot(seq_ref[:, SEQ:2 * SEQ], z_ref[SEQ:2 * SEQ, :], preferred_element_type=F32)
    top = (pc + ps)[0:half, :]
    d_ref[...] = pc - ps
    bottom = _flip_rows(d_ref[1:half + 1, :])
    o_ref[0:half, :] = jnp.dot(top.astype(BF16), w_ref[...],
                               preferred_element_type=F32).astype(BF16)
    o_ref[half:SEQ, :] = jnp.dot(bottom.astype(BF16), w_ref[...],
                                 preferred_element_type=F32).astype(BF16)


def _fourier(uf, seq_mat, chan_mat, w_fo):
    return pl.pallas_call(
        _fourier_kernel,
        grid=(BATCH,),
        in_specs=[
            pl.BlockSpec((SEQ, F_WIDTH), lambda b: (b, 0)),
            _resident((SEQ_HALF_ROWS, 2 * SEQ)),
            _resident((F_WIDTH, 2 * F_WIDTH)),
            _resident((F_WIDTH, D_MODEL)),
        ],
        out_specs=pl.BlockSpec((SEQ, D_MODEL), lambda b: (b, 0)),
        out_shape=jax.ShapeDtypeStruct((N_TOK, D_MODEL), BF16),
        scratch_shapes=[pltpu.VMEM((2 * SEQ, F_WIDTH), BF16),
                        pltpu.VMEM((SEQ_HALF_ROWS, F_WIDTH), F32)],
        compiler_params=_params(("arbitrary",)),
        name="fourier",
    )(uf, seq_mat, chan_mat, w_fo)


def _bucket(rel):
    nb = REL_BUCKETS // 2
    max_exact = nb // 2
    n = jnp.minimum(jnp.abs(rel), REL_MAX_DIST)
    nf = n.astype(F32)
    expo = lax.shift_right_logical(lax.bitcast_convert_type(nf * nf, jnp.int32), 23) - 127
    large = jnp.minimum(expo + 2, nb - 1)
    return jnp.where(rel > 0, nb, 0) + jnp.where(n < max_exact, n, large)


def _attn_kernel(kmin_ref, kmax_ref, qmin_ref, qmax_ref, same_ref,
                 q_ref, k_ref, v_ref, posk_ref, posq_ref, tbl_ref, far_ref, lam_ref, sg_ref,
                 w32_0, w32_1, w32_2, w32_3, w32_4,
                 o_ref, w16_0, w16_1, w16_2, w16_3, w16_4,
                 vt_ref, poskb_ref, bias_ref, fill_ref, cache_ref, cblk_ref):
    b = pl.program_id(0)
    i = pl.program_id(1)
    tq, tk, sub = ATT_TQ, ATT_TK, ATT_SUB
    nblk = SEQ // sub

    @pl.when(i == 0)
    def _():
        for h in range(DA_HEADS):
            for c in range(SEQ // tk):
                blk = v_ref[c * tk:(c + 1) * tk, h * DA_V_DIM:(h + 1) * DA_V_DIM]
                vt_ref[h * VT_ROWS:h * VT_ROWS + DA_V_DIM, c * tk:(c + 1) * tk] = (
                    blk.astype(F32).T.astype(BF16))
            vt_ref[h * VT_ROWS + DA_V_DIM:(h + 1) * VT_ROWS, :] = jnp.ones(
                (VT_ROWS - DA_V_DIM, SEQ), BF16)
        for c in range(SEQ // LANES):
            row = posk_ref[0, :, c * LANES:(c + 1) * LANES]
            halves = [jnp.broadcast_to(part.astype(F32), (LANES, LANES)).T.astype(jnp.int32)
                      for part in (lax.shift_right_arithmetic(row, 16), row & 0xFFFF)]
            poskb_ref[c * LANES:(c + 1) * LANES, :] = lax.shift_left(halves[0], 16) | halves[1]

    @pl.when(jnp.logical_and(b == 0, i == 0))
    def _():
        for e in range(ATT_TILES * nblk):
            fill_ref[e] = FILL_NONE

    lq1, lk1, lq2, lk2 = (lam_ref[r:r + 1, :] for r in range(4))
    lam = (jnp.exp(jnp.sum(lq1 * lk1, axis=-1, keepdims=True))
           - jnp.exp(jnp.sum(lq2 * lk2, axis=-1, keepdims=True)) + LAMBDA_INIT)

    lane = lax.broadcasted_iota(jnp.int32, (tq, DA_V_DIM), 1)
    first_map = lane < DA_HEAD_DIM
    nt = (((1,), (1,)), ((), ()))

    qz = {}
    for j in range(ATT_TILES):
        for h in range(DA_HEADS):
            qh = q_ref[j * tq:(j + 1) * tq, h * DA_V_DIM:(h + 1) * DA_V_DIM]
            zero = jnp.zeros_like(qh)
            qz[j, h] = jnp.concatenate([jnp.where(first_map, qh, zero),
                                        jnp.where(first_map, zero, qh)], axis=0)
    state = {}

    @pl.when(same_ref[b] == 0)
    def _():
        for e in range(ATT_TILES * NEAR_SLOTS):
            cblk_ref[i * ATT_TILES * NEAR_SLOTS + e] = -1

    for j in range(ATT_TILES):
        tile = i * ATT_TILES + j
        posq = posq_ref[j]
        qmin = qmin_ref[b, tile]
        qmax = qmax_ref[b, tile]
        slot = jnp.int32(0)
        for blk in range(nblk):
            rows = slice(blk * sub, (blk + 1) * sub)
            far_right = kmin_ref[b, blk] - qmax >= REL_MAX_DIST
            far_left = qmin - kmax_ref[b, blk] >= REL_MAX_DIST
            far = jnp.logical_or(far_right, far_left)
            near = jnp.logical_not(far)
            cacheable = slot < NEAR_SLOTS
            entry = tile * NEAR_SLOTS + jnp.minimum(slot, NEAR_SLOTS - 1)
            hit = jnp.logical_and(jnp.logical_and(near, cacheable), cblk_ref[entry] == blk)

            code = jnp.where(far_right, FILL_RIGHT, FILL_LEFT)
            fidx = j * nblk + blk

            @pl.when(jnp.logical_and(far, fill_ref[fidx] != code))
            def _(rows=rows, fidx=fidx, far_right=far_right, code=code, j=j):
                for h in range(DA_HEADS):
                    cst = jnp.where(far_right, far_ref[1, h], far_ref[0, h])
                    bias_ref[j, h, rows, :] = jnp.full((sub, tq), cst, F32)
                fill_ref[fidx] = code

            @pl.when(hit)
            def _(rows=rows, fidx=fidx, entry=entry, j=j):
                fill_ref[fidx] = FILL_NONE
                bias_ref[j, :, rows, :] = cache_ref[entry]

            @pl.when(jnp.logical_and(near, jnp.logical_not(hit)))
            def _(rows=rows, blk=blk, fidx=fidx, entry=entry, cacheable=cacheable, j=j, posq=posq):
                fill_ref[fidx] = FILL_NONE
                tbls = [jnp.broadcast_to(tbl_ref[h:h + 1, :], (sub, LANES)) for h in range(DA_HEADS)]
                for t in range(tq // LANES):
                    ln = slice(t * LANES, (t + 1) * LANES)
                    rel = poskb_ref[rows, :] - posq[:, ln]
                    bucket = _bucket(rel)
                    for h in range(DA_HEADS):
                        bias_ref[j, h, rows, ln] = jnp.take_along_axis(
                            tbls[h], bucket, axis=1, mode="promise_in_bounds")

                @pl.when(cacheable)
                def _():
                    cache_ref[entry] = bias_ref[j, :, rows, :]
                    cblk_ref[entry] = blk

            slot = slot + near.astype(jnp.int32)

    def scores(j, c, h):
        parts, m_c = [], None
        for u in range(tk // ATT_QK):
            rows = slice(c * tk + u * ATT_QK, c * tk + (u + 1) * ATT_QK)
            bias = bias_ref[j, h, rows, :]
            s = (lax.dot_general(k_ref[rows, h * DA_V_DIM:(h + 1) * DA_V_DIM], qz[j, h], nt,
                                 preferred_element_type=F32)
                 + jnp.concatenate([bias, bias], axis=1))
            m_u = jnp.max(s, axis=0, keepdims=True)
            m_c = m_u if m_c is None else jnp.maximum(m_c, m_u)
            parts.append(s)
        return parts, m_c

    for src, dst in ((w32_0, w16_0), (w32_1, w16_1), (w32_2, w16_2), (w32_3, w16_3), (w32_4, w16_4)):
        dst[...] = src[...].astype(BF16)

    items = [(j, c, h) for c in range(SEQ // tk) for j in range(ATT_TILES) for h in range(DA_HEADS)]
    ahead = scores(*items[0])
    for n, (j, c, h) in enumerate(items):
        s_parts, m_c = ahead
        if n + 1 < len(items):
            ahead = scores(*items[n + 1])
        vt = vt_ref[h * VT_ROWS:(h + 1) * VT_ROWS, c * tk:(c + 1) * tk]
        if c == 0:
            m_new = m_c
        else:
            m_old, acc_old = state[j, h]
            m_new = jnp.maximum(m_old, m_c)
            alpha = jnp.exp2(m_old - m_new)
        p = jnp.concatenate([jnp.exp2(s - m_new).astype(BF16) for s in s_parts], axis=0)
        acc_new = jnp.dot(vt, p, preferred_element_type=F32)
        if c > 0:
            acc_new = alpha * acc_old + acc_new
        state[j, h] = (m_new, acc_new)
        if c == SEQ // tk - 1:
            l_fin = acc_new[DA_V_DIM:DA_V_DIM + 1, :]
            acc = acc_new[:DA_V_DIM, :]
            r1 = 1.0 / l_fin[:, :tq]
            r2 = lam / l_fin[:, tq:]
            o = (acc[:, :tq] * r1 - acc[:, tq:] * r2).T
            o = _rms(o, sg_ref[...], SUBLN_EPS) * (1.0 - LAMBDA_INIT)
            o_ref[j * tq:(j + 1) * tq, h * DA_V_DIM:(h + 1) * DA_V_DIM] = o.astype(BF16)


def _attention(q, k, v, positions, rel_bias, lam_rows, subln_g, later_weights):
    tq, sub = ATT_TQ, ATT_SUB
    nq = SEQ // tq
    ns = nq // ATT_TILES
    nsub = SEQ // sub
    steps_per_slab = BATCH * ns // CAST_SLABS
    slab_specs = [pl.BlockSpec((w.shape[0] // CAST_SLABS, w.shape[1]),
                               lambda b, i, *_: ((b * ns + i) // steps_per_slab, 0))
                  for w in later_weights]
    log2e = math.log2(math.e)
    tbl_t = rel_bias.T.astype(F32) * log2e
    tbl = jnp.zeros((8, LANES), F32).at[:DA_HEADS, :REL_BUCKETS].set(tbl_t)
    nb = REL_BUCKETS // 2
    far = jnp.stack([tbl_t[:, nb - 1], tbl_t[:, 2 * nb - 1]])
    pk = positions.reshape(BATCH, nsub, sub)
    pq = positions.reshape(BATCH, nq, tq)
    posk = positions.reshape(BATCH, 1, SEQ)
    posq = positions.reshape(BATCH * nq, 1, tq)
    same_as_prev = jnp.concatenate([
        jnp.zeros((1,), jnp.int32),
        jnp.all(positions[1:] == positions[:-1], axis=1).astype(jnp.int32)])
    smem = pl.BlockSpec(memory_space=pltpu.SMEM)
    once_per_row = dict(pipeline_mode=pl.Buffered(1))
    grid_spec = pltpu.PrefetchScalarGridSpec(
        num_scalar_prefetch=5,
        grid=(BATCH, ns),
        in_specs=[
            pl.BlockSpec((ATT_TILES * tq, QK_WIDTH), lambda b, i, *_: (b * ns + i, 0)),
            pl.BlockSpec((SEQ, QK_WIDTH), lambda b, i, *_: (b, 0), **once_per_row),
            pl.BlockSpec((SEQ, V_WIDTH), lambda b, i, *_: (b, 0), **once_per_row),
            pl.BlockSpec((1, 1, SEQ), lambda b, i, *_: (b, 0, 0)),
            pl.BlockSpec((ATT_TILES, 1, tq), lambda b, i, *_: (b * ns + i, 0, 0)),
            _resident((8, LANES)),
            smem,
            _resident((4, DA_HEAD_DIM)),
            _resident((1, DA_V_DIM)),
        ] + slab_specs,
        out_specs=[pl.BlockSpec((ATT_TILES * tq, V_WIDTH), lambda b, i, *_: (b * ns + i, 0))] + slab_specs,
        scratch_shapes=[
            pltpu.VMEM((DA_HEADS * VT_ROWS, SEQ), BF16),
            pltpu.VMEM((SEQ, LANES), jnp.int32),
            pltpu.VMEM((ATT_TILES, DA_HEADS, SEQ, tq), F32),
            pltpu.SMEM((ATT_TILES * nsub,), jnp.int32),
            pltpu.VMEM((nq * NEAR_SLOTS, DA_HEADS, sub, tq), F32),
            pltpu.SMEM((nq * NEAR_SLOTS,), jnp.int32),
        ],
    )
    return pl.pallas_call(
        _attn_kernel,
        grid_spec=grid_spec,
        out_shape=[jax.ShapeDtypeStruct((N_TOK, V_WIDTH), BF16)]
                  + [jax.ShapeDtypeStruct(w.shape, BF16) for w in later_weights],
        compiler_params=_params(("arbitrary", "arbitrary")),
        name="diff_attn",
    )(pk.min(-1), pk.max(-1), pq.min(-1), pq.max(-1), same_as_prev,
      q, k, v, posk, posq, tbl, far, lam_rows, subln_g, *later_weights)


def _mix_ffn_kernel(x_ref, ya_ref, o_ref, ga_ref, gb_ref, wa_ref, wo_ref,
                    g_ref, wg_ref, wu_ref, wd_ref, fn_ref, out_ref):
    def mix(rows, yb):
        merged = (ga_ref[rows, :].astype(F32) * ya_ref[rows, :].astype(F32)
                  + gb_ref[rows, :].astype(F32) * yb).astype(BF16)
        return x_ref[rows, :] + jnp.dot(merged, wo_ref[...], preferred_element_type=F32)

    yb = [jnp.dot(o_ref[rows, :], wa_ref[...], preferred_element_type=F32) for rows in ROW_HALVES]
    x2 = [mix(rows, y) for rows, y in zip(ROW_HALVES, yb)]
    gu = [_gate_up(x, g_ref[...], wg_ref, wu_ref) for x in x2]
    for rows, x, (gate, up) in zip(ROW_HALVES, x2, gu):
        y = _down_residual(x, gate, up, wd_ref)
        out_ref[rows, :] = _rms(y, fn_ref[...], NORM_EPS)


def _mix_ffn(x, ya, o, ga, gb, w_ao, w_out, g, wg, wu, wd, fn):
    tm = FFN_TM
    row = lambda w: pl.BlockSpec((tm, w), lambda i: (i, 0))
    return pl.pallas_call(
        _mix_ffn_kernel,
        grid=(N_TOK // tm,),
        in_specs=[row(D_MODEL), row(D_MODEL), row(V_WIDTH), row(D_MODEL), row(D_MODEL),
                  _resident((V_WIDTH, D_MODEL)), _resident((D_MODEL, D_MODEL))]
                 + _ffn_specs() + [_resident((1, D_MODEL))],
        out_specs=row(D_MODEL),
        out_shape=jax.ShapeDtypeStruct((N_TOK, D_MODEL), F32),
        compiler_params=_params(("arbitrary",)),
        name="mix_ffn",
    )(x, ya, o, ga, gb, w_ao, w_out, g, wg, wu, wd, fn)


def kernel(x, positions, rel_bias, ffn1_norm, ffn1_wg, ffn1_wu, ffn1_wd, mix_norm, w_in,
           lambda_q1, lambda_k1, lambda_q2, lambda_k2, subln_g, w_fourier_out, w_attn_out,
           w_out, ffn2_norm, ffn2_wg, ffn2_wu, ffn2_wd, final_norm):
    assert x.shape == (BATCH, SEQ, D_MODEL) and positions.shape == (BATCH, SEQ)
    bf = lambda w: w.astype(BF16)
    row = lambda g: g.reshape(1, -1).astype(F32)
    seq_mat, chan_mat = _dft_constants()
    lam_rows = jnp.concatenate([lambda_q1, lambda_k1, lambda_q2, lambda_k2], axis=0).astype(F32)

    xt = x.reshape(N_TOK, D_MODEL)
    x1, uf, q, k, v, ga, gb = _ffn_proj(xt, row(ffn1_norm[0]), bf(ffn1_wg[0]), bf(ffn1_wu[0]),
                                        bf(ffn1_wd[0]), row(mix_norm[0]), bf(w_in[0]))
    ya = _fourier(uf, seq_mat, chan_mat, bf(w_fourier_out[0]))
    later = [w[0].astype(F32) for w in (ffn2_wg, ffn2_wu, ffn2_wd, w_out, w_attn_out)]
    o, wg2, wu2, wd2, wo, wa = _attention(q, k, v, positions.astype(jnp.int32), rel_bias, lam_rows,
                                          row(subln_g[0]), later)
    out = _mix_ffn(x1, ya, o, ga, gb, wa, wo, row(ffn2_norm[0]), wg2, wu2, wd2, row(final_norm))
    return out.reshape(BATCH, SEQ, D_MODEL)
```

```python
import math

import numpy as np
import jax
import jax.numpy as jnp
from jax import lax
from jax.experimental import pallas as pl
from jax.experimental.pallas import tpu as pltpu

D_MODEL = 1024
BATCH = 8
SEQ = 2048
D_FF = 2816
F_GROUPS = 4
F_GROUP_CH = 128
F_WIDTH = F_GROUPS * F_GROUP_CH
DA_HEADS = 4
DA_HEAD_DIM = 64
DA_V_DIM = 2 * DA_HEAD_DIM
QK_WIDTH = DA_HEADS * 2 * DA_HEAD_DIM
V_WIDTH = DA_HEADS * DA_V_DIM
IN_WIDTH = F_WIDTH + 2 * QK_WIDTH + V_WIDTH + 2 * D_MODEL
REL_BUCKETS = 32
REL_MAX_DIST = 128
NORM_EPS = 1e-6
SUBLN_EPS = 1e-5
LAMBDA_INIT = 0.8 - 0.6 * math.exp(-0.3 * 0)

N_TOK = BATCH * SEQ
LANES = 128
VMEM_LIMIT = 56 * 1024 * 1024

FFN_TM = 512
SEQ_HALF_ROWS = SEQ // 2 + 16
ATT_TQ = 256
ATT_TILES = 2
ATT_TK = 1024
ATT_SUB = 128
ATT_QK = 256
VT_ROWS = DA_V_DIM + 16
FILL_NONE, FILL_LEFT, FILL_RIGHT = 0, 1, 2
CAST_SLABS = 32
NEAR_SLOTS = (ATT_TQ + 2 * REL_MAX_DIST) // ATT_SUB

BF16 = jnp.bfloat16
F32 = jnp.float32


def _rms(x, g, eps):
    return x * lax.rsqrt(jnp.mean(x * x, axis=-1, keepdims=True) + eps) * g


def _resident(shape):
    return pl.BlockSpec(shape, lambda *_: (0,) * len(shape), pipeline_mode=pl.Buffered(1))


def _params(sem):
    return pltpu.CompilerParams(dimension_semantics=sem, vmem_limit_bytes=VMEM_LIMIT)


ROW_HALVES = [slice(r * (FFN_TM // 2), (r + 1) * (FFN_TM // 2)) for r in range(2)]


def _gate_up(x, g, wg_ref, wu_ref):
    h = _rms(x, g, NORM_EPS).astype(BF16)
    return (jnp.dot(h, wg_ref[...], preferred_element_type=F32),
            jnp.dot(h, wu_ref[...], preferred_element_type=F32))


def _down_residual(x, gate, up, wd_ref):
    a = (gate * jax.nn.sigmoid(gate) * up).astype(BF16)
    return x + 0.5 * jnp.dot(a, wd_ref[...], preferred_element_type=F32)


def _ffn_specs():
    return [_resident((1, D_MODEL)), _resident((D_MODEL, D_FF)), _resident((D_MODEL, D_FF)),
            _resident((D_FF, D_MODEL))]


def _ffn_proj_kernel(x_ref, g_ref, wg_ref, wu_ref, wd_ref, gm_ref, w_ref,
                     x1_ref, uf_ref, q_ref, k_ref, v_ref, ga_ref, gb_ref):
    def proj(rows, x1):
        x1_ref[rows, :] = x1
        h = _rms(x1, gm_ref[...], NORM_EPS).astype(BF16)
        p = jnp.dot(h, w_ref[...], preferred_element_type=F32)
        c = 0
        uf_ref[rows, :] = p[:, c:c + F_WIDTH].astype(BF16)
        c += F_WIDTH
        q_ref[rows, :] = (p[:, c:c + QK_WIDTH]
                          * (DA_HEAD_DIM ** -0.5 * math.log2(math.e))).astype(BF16)
        c += QK_WIDTH
        k_ref[rows, :] = p[:, c:c + QK_WIDTH].astype(BF16)
        c += QK_WIDTH
        v_ref[rows, :] = p[:, c:c + V_WIDTH].astype(BF16)
        c += V_WIDTH
        ga_ref[rows, :] = jax.nn.sigmoid(p[:, c:c + D_MODEL]).astype(BF16)
        c += D_MODEL
        gb_ref[rows, :] = jax.nn.sigmoid(p[:, c:c + D_MODEL]).astype(BF16)

    xs = [x_ref[rows, :] for rows in ROW_HALVES]
    gu = [_gate_up(x, g_ref[...], wg_ref, wu_ref) for x in xs]
    x1 = [_down_residual(x, gate, up, wd_ref) for x, (gate, up) in zip(xs, gu)]
    for rows, x in zip(ROW_HALVES, x1):
        proj(rows, x)


def _ffn_proj(x, g, wg, wu, wd, g_mix, w_in):
    tm = FFN_TM
    row = lambda w: pl.BlockSpec((tm, w), lambda i: (i, 0))
    widths = (F_WIDTH, QK_WIDTH, QK_WIDTH, V_WIDTH, D_MODEL, D_MODEL)
    return pl.pallas_call(
        _ffn_proj_kernel,
        grid=(N_TOK // tm,),
        in_specs=[row(D_MODEL)] + _ffn_specs() + [_resident((1, D_MODEL)),
                                                  _resident((D_MODEL, IN_WIDTH))],
        out_specs=[row(D_MODEL)] + [row(w) for w in widths],
        out_shape=[jax.ShapeDtypeStruct((N_TOK, D_MODEL), F32)]
                  + [jax.ShapeDtypeStruct((N_TOK, w), BF16) for w in widths],
        compiler_params=_params(("arbitrary",)),
        name="ffn_proj",
    )(x, g, wg, wu, wd, g_mix, w_in)


def _dft_constants():
    k = np.arange(SEQ_HALF_ROWS, dtype=np.int64)[:, None]
    n = np.arange(SEQ, dtype=np.int64)[None, :]
    phase = (k * n) % SEQ
    live = (k <= SEQ // 2)
    cos_h = np.where(live, np.cos(2.0 * np.pi * phase / SEQ), 0.0)
    sin_h = np.where(live & (phase % (SEQ // 2) != 0), np.sin(2.0 * np.pi * phase / SEQ), 0.0)
    seq_mat = np.concatenate([cos_h, sin_h], axis=1)
    c = np.arange(F_GROUP_CH, dtype=np.int64)
    angc = 2.0 * np.pi * ((c[:, None] * c[None, :]) % F_GROUP_CH) / F_GROUP_CH
    scale = 1.0 / math.sqrt(SEQ * F_GROUP_CH)
    eye = np.eye(F_GROUPS)
    chan = np.concatenate([np.kron(eye, np.cos(angc)), -np.kron(eye, np.sin(angc))], axis=1) * scale
    return jnp.asarray(seq_mat, dtype=BF16), jnp.asarray(chan, dtype=BF16)


def _flip_rows(x):
    rows, cols = x.shape
    idx = 7 - lax.broadcasted_iota(jnp.int32, (8, cols), 0)
    groups = [jnp.take_along_axis(x[g * 8:(g + 1) * 8, :], idx, axis=0)
              for g in reversed(range(rows // 8))]
    return jnp.concatenate(groups, axis=0)


def _fourier_kernel(u_ref, seq_ref, chan_ref, w_ref, o_ref, z_ref, d_ref):
    u = u_ref[...]
    for g in range(F_GROUPS):
        cols = slice(g * F_GROUP_CH, (g + 1) * F_GROUP_CH)
        sin_cols = slice(F_WIDTH + g * F_GROUP_CH, F_WIDTH + (g + 1) * F_GROUP_CH)
        z_ref[0:SEQ, cols] = jnp.dot(u[:, cols], chan_ref[cols, cols],
                                     preferred_element_type=F32).astype(BF16)
        z_ref[SEQ:2 * SEQ, cols] = jnp.dot(u[:, cols], chan_ref[cols, sin_cols],
                                           preferred_element_type=F32).astype(BF16)
    half = SEQ // 2
    pc = jnp.dot(seq_ref[:, 0:SEQ], z_ref[0:SEQ, :], preferred_element_type=F32)
    ps = jnp.dot(seq_ref[:, SEQ:2 * SEQ], z_ref[SEQ:2 * SEQ, :], preferred_element_type=F32)
    top = (pc + ps)[0:half, :]
    d_ref[...] = pc - ps
    bottom = _flip_rows(d_ref[1:half + 1, :])
    o_ref[0:half, :] = jnp.dot(top.astype(BF16), w_ref[...],
                               preferred_element_type=F32).astype(BF16)
    o_ref[half:SEQ, :] = jnp.dot(bottom.astype(BF16), w_ref[...],
                                 preferred_element_type=F32).astype(BF16)


def _fourier(uf, seq_mat, chan_mat, w_fo):
    return pl.pallas_call(
        _fourier_kernel,
        grid=(BATCH,),
        in_specs=[
            pl.BlockSpec((SEQ, F_WIDTH), lambda b: (b, 0)),
            _resident((SEQ_HALF_ROWS, 2 * SEQ)),
            _resident((F_WIDTH, 2 * F_WIDTH)),
            _resident((F_WIDTH, D_MODEL)),
        ],
        out_specs=pl.BlockSpec((SEQ, D_MODEL), lambda b: (b, 0)),
        out_shape=jax.ShapeDtypeStruct((N_TOK, D_MODEL), BF16),
        scratch_shapes=[pltpu.VMEM((2 * SEQ, F_WIDTH), BF16),
                        pltpu.VMEM((SEQ_HALF_ROWS, F_WIDTH), F32)],
        compiler_params=_params(("arbitrary",)),
        name="fourier",
    )(uf, seq_mat, chan_mat, w_fo)


def _bucket(rel):
    nb = REL_BUCKETS // 2
    max_exact = nb // 2
    n = jnp.minimum(jnp.abs(rel), REL_MAX_DIST)
    nf = n.astype(F32)
    expo = lax.shift_right_logical(lax.bitcast_convert_type(nf * nf, jnp.int32), 23) - 127
    large = jnp.minimum(expo + 2, nb - 1)
    return jnp.where(rel > 0, nb, 0) + jnp.where(n < max_exact, n, large)


def _attn_kernel(kmin_ref, kmax_ref, qmin_ref, qmax_ref, same_ref,
                 q_ref, k_ref, v_ref, posk_ref, posq_ref, tbl_ref, far_ref, lam_ref, sg_ref,
                 w32_0, w32_1, w32_2, w32_3, w32_4,
                 o_ref, w16_0, w16_1, w16_2, w16_3, w16_4,
                 vt_ref, poskb_ref, bias_ref, fill_ref, cache_ref, cblk_ref):
    b = pl.program_id(0)
    i = pl.program_id(1)
    tq, tk, sub = ATT_TQ, ATT_TK, ATT_SUB
    nblk = SEQ // sub

    @pl.when(i == 0)
    def _():
        for h in range(DA_HEADS):
            for c in range(SEQ // tk):
                blk = v_ref[c * tk:(c + 1) * tk, h * DA_V_DIM:(h + 1) * DA_V_DIM]
                vt_ref[h * VT_ROWS:h * VT_ROWS + DA_V_DIM, c * tk:(c + 1) * tk] = (
                    blk.astype(F32).T.astype(BF16))
            vt_ref[h * VT_ROWS + DA_V_DIM:(h + 1) * VT_ROWS, :] = jnp.ones(
                (VT_ROWS - DA_V_DIM, SEQ), BF16)
        for c in range(SEQ // LANES):
            row = posk_ref[0, :, c * LANES:(c + 1) * LANES]
            halves = [jnp.broadcast_to(part.astype(F32), (LANES, LANES)).T.astype(jnp.int32)
                      for part in (lax.shift_right_arithmetic(row, 16), row & 0xFFFF)]
            poskb_ref[c * LANES:(c + 1) * LANES, :] = lax.shift_left(halves[0], 16) | halves[1]

    @pl.when(jnp.logical_and(b == 0, i == 0))
    def _():
        for e in range(ATT_TILES * nblk):
            fill_ref[e] = FILL_NONE

    lq1, lk1, lq2, lk2 = (lam_ref[r:r + 1, :] for r in range(4))
    lam = (jnp.exp(jnp.sum(lq1 * lk1, axis=-1, keepdims=True))
           - jnp.exp(jnp.sum(lq2 * lk2, axis=-1, keepdims=True)) + LAMBDA_INIT)

    lane = lax.broadcasted_iota(jnp.int32, (tq, DA_V_DIM), 1)
    first_map = lane < DA_HEAD_DIM
    nt = (((1,), (1,)), ((), ()))

    qz = {}
    for j in range(ATT_TILES):
        for h in range(DA_HEADS):
            qh = q_ref[j * tq:(j + 1) * tq, h * DA_V_DIM:(h + 1) * DA_V_DIM]
            zero = jnp.zeros_like(qh)
            qz[j, h] = jnp.concatenate([jnp.where(first_map, qh, zero),
                                        jnp.where(first_map, zero, qh)], axis=0)
    state = {}

    @pl.when(same_ref[b] == 0)
    def _():
        for e in range(ATT_TILES * NEAR_SLOTS):
            cblk_ref[i * ATT_TILES * NEAR_SLOTS + e] = -1

    for j in range(ATT_TILES):
        tile = i * ATT_TILES + j
        posq = posq_ref[j]
        qmin = qmin_ref[b, tile]
        qmax = qmax_ref[b, tile]
        slot = jnp.int32(0)
        for blk in range(nblk):
            rows = slice(blk * sub, (blk + 1) * sub)
            far_right = kmin_ref[b, blk] - qmax >= REL_MAX_DIST
            far_left = qmin - kmax_ref[b, blk] >= REL_MAX_DIST
            far = jnp.logical_or(far_right, far_left)
            near = jnp.logical_not(far)
            cacheable = slot < NEAR_SLOTS
            entry = tile * NEAR_SLOTS + jnp.minimum(slot, NEAR_SLOTS - 1)
            hit = jnp.logical_and(jnp.logical_and(near, cacheable), cblk_ref[entry] == blk)

            code = jnp.where(far_right, FILL_RIGHT, FILL_LEFT)
            fidx = j * nblk + blk

            @pl.when(jnp.logical_and(far, fill_ref[fidx] != code))
            def _(rows=rows, fidx=fidx, far_right=far_right, code=code, j=j):
                for h in range(DA_HEADS):
                    cst = jnp.where(far_right, far_ref[1, h], far_ref[0, h])
                    bias_ref[j, h, rows, :] = jnp.full((sub, tq), cst, F32)
                fill_ref[fidx] = code

            @pl.when(hit)
            def _(rows=rows, fidx=fidx, entry=entry, j=j):
                fill_ref[fidx] = FILL_NONE
                bias_ref[j, :, rows, :] = cache_ref[entry]

            @pl.when(jnp.logical_and(near, jnp.logical_not(hit)))
            def _(rows=rows, blk=blk, fidx=fidx, entry=entry, cacheable=cacheable, j=j, posq=posq):
                fill_ref[fidx] = FILL_NONE
                tbls = [jnp.broadcast_to(tbl_ref[h:h + 1, :], (sub, LANES)) for h in range(DA_HEADS)]
                for t in range(tq // LANES):
                    ln = slice(t * LANES, (t + 1) * LANES)
                    rel = poskb_ref[rows, :] - posq[:, ln]
                    bucket = _bucket(rel)
                    for h in range(DA_HEADS):
                        bias_ref[j, h, rows, ln] = jnp.take_along_axis(
                            tbls[h], bucket, axis=1, mode="promise_in_bounds")

                @pl.when(cacheable)
                def _():
                    cache_ref[entry] = bias_ref[j, :, rows, :]
                    cblk_ref[entry] = blk

            slot = slot + near.astype(jnp.int32)

    def scores(j, c, h):
        parts, m_c = [], None
        for u in range(tk // ATT_QK):
            rows = slice(c * tk + u * ATT_QK, c * tk + (u + 1) * ATT_QK)
            bias = bias_ref[j, h, rows, :]
            s = (lax.dot_general(k_ref[rows, h * DA_V_DIM:(h + 1) * DA_V_DIM], qz[j, h], nt,
                                 preferred_element_type=F32)
                 + jnp.concatenate([bias, bias], axis=1))
            m_u = jnp.max(s, axis=0, keepdims=True)
            m_c = m_u if m_c is None else jnp.maximum(m_c, m_u)
            parts.append(s)
        return parts, m_c

    for src, dst in ((w32_0, w16_0), (w32_1, w16_1), (w32_2, w16_2), (w32_3, w16_3), (w32_4, w16_4)):
        dst[...] = src[...].astype(BF16)

    items = [(j, c, h) for c in range(SEQ // tk) for j in range(ATT_TILES) for h in range(DA_HEADS)]
    ahead = scores(*items[0])
    for n, (j, c, h) in enumerate(items):
        s_parts, m_c = ahead
        if n + 1 < len(items):
            ahead = scores(*items[n + 1])
        vt = vt_ref[h * VT_ROWS:(h + 1) * VT_ROWS, c * tk:(c + 1) * tk]
        if c == 0:
            m_new = m_c
        else:
            m_old, acc_old = state[j, h]
            m_new = jnp.maximum(m_old, m_c)
            alpha = jnp.exp2(m_old - m_new)
        p = jnp.concatenate([jnp.exp2(s - m_new).astype(BF16) for s in s_parts], axis=0)
        acc_new = jnp.dot(vt, p, preferred_element_type=F32)
        if c > 0:
            acc_new = alpha * acc_old + acc_new
        state[j, h] = (m_new, acc_new)
        if c == SEQ // tk - 1:
            l_fin = acc_new[DA_V_DIM:DA_V_DIM + 1, :]
            acc = acc_new[:DA_V_DIM, :]
            r1 = 1.0 / l_fin[:, :tq]
            r2 = lam / l_fin[:, tq:]
            o = (acc[:, :tq] * r1 - acc[:, tq:] * r2).T
            o = _rms(o, sg_ref[...], SUBLN_EPS) * (1.0 - LAMBDA_INIT)
            o_ref[j * tq:(j + 1) * tq, h * DA_V_DIM:(h + 1) * DA_V_DIM] = o.astype(BF16)


def _attention(q, k, v, positions, rel_bias, lam_rows, subln_g, later_weights):
    tq, sub = ATT_TQ, ATT_SUB
    nq = SEQ // tq
    ns = nq // ATT_TILES
    nsub = SEQ // sub
    steps_per_slab = BATCH * ns // CAST_SLABS
    slab_specs = [pl.BlockSpec((w.shape[0] // CAST_SLABS, w.shape[1]),
                               lambda b, i, *_: ((b * ns + i) // steps_per_slab, 0))
                  for w in later_weights]
    log2e = math.log2(math.e)
    tbl_t = rel_bias.T.astype(F32) * log2e
    tbl = jnp.zeros((8, LANES), F32).at[:DA_HEADS, :REL_BUCKETS].set(tbl_t)
    nb = REL_BUCKETS // 2
    far = jnp.stack([tbl_t[:, nb - 1], tbl_t[:, 2 * nb - 1]])
    pk = positions.reshape(BATCH, nsub, sub)
    pq = positions.reshape(BATCH, nq, tq)
    posk = positions.reshape(BATCH, 1, SEQ)
    posq = positions.reshape(BATCH * nq, 1, tq)
    same_as_prev = jnp.concatenate([
        jnp.zeros((1,), jnp.int32),
        jnp.all(positions[1:] == positions[:-1], axis=1).astype(jnp.int32)])
    smem = pl.BlockSpec(memory_space=pltpu.SMEM)
    grid_spec = pltpu.PrefetchScalarGridSpec(
        num_scalar_prefetch=5,
        grid=(BATCH, ns),
        in_specs=[
            pl.BlockSpec((ATT_TILES * tq, QK_WIDTH), lambda b, i, *_: (b * ns + i, 0)),
            pl.BlockSpec((SEQ, QK_WIDTH), lambda b, i, *_: (b, 0)),
            pl.BlockSpec((SEQ, V_WIDTH), lambda b, i, *_: (b, 0)),
            pl.BlockSpec((1, 1, SEQ), lambda b, i, *_: (b, 0, 0)),
            pl.BlockSpec((ATT_TILES, 1, tq), lambda b, i, *_: (b * ns + i, 0, 0)),
            _resident((8, LANES)),
            smem,
            _resident((4, DA_HEAD_DIM)),
            _resident((1, DA_V_DIM)),
        ] + slab_specs,
        out_specs=[pl.BlockSpec((ATT_TILES * tq, V_WIDTH), lambda b, i, *_: (b * ns + i, 0))] + slab_specs,
        scratch_shapes=[
            pltpu.VMEM((DA_HEADS * VT_ROWS, SEQ), BF16),
            pltpu.VMEM((SEQ, LANES), jnp.int32),
            pltpu.VMEM((ATT_TILES, DA_HEADS, SEQ, tq), F32),
            pltpu.SMEM((ATT_TILES * nsub,), jnp.int32),
            pltpu.VMEM((nq * NEAR_SLOTS, DA_HEADS, sub, tq), F32),
            pltpu.SMEM((nq * NEAR_SLOTS,), jnp.int32),
        ],
    )
    return pl.pallas_call(
        _attn_kernel,
        grid_spec=grid_spec,
        out_shape=[jax.ShapeDtypeStruct((N_TOK, V_WIDTH), BF16)]
                  + [jax.ShapeDtypeStruct(w.shape, BF16) for w in later_weights],
        compiler_params=_params(("arbitrary", "arbitrary")),
        name="diff_attn",
    )(pk.min(-1), pk.max(-1), pq.min(-1), pq.max(-1), same_as_prev,
      q, k, v, posk, posq, tbl, far, lam_rows, subln_g, *later_weights)


def _mix_ffn_kernel(x_ref, ya_ref, o_ref, ga_ref, gb_ref, wa_ref, wo_ref,
                    g_ref, wg_ref, wu_ref, wd_ref, fn_ref, out_ref):
    def mix(rows, yb):
        merged = (ga_ref[rows, :].astype(F32) * ya_ref[rows, :].astype(F32)
                  + gb_ref[rows, :].astype(F32) * yb).astype(BF16)
        return x_ref[rows, :] + jnp.dot(merged, wo_ref[...], preferred_element_type=F32)

    yb = [jnp.dot(o_ref[rows, :], wa_ref[...], preferred_element_type=F32) for rows in ROW_HALVES]
    x2 = [mix(rows, y) for rows, y in zip(ROW_HALVES, yb)]
    gu = [_gate_up(x, g_ref[...], wg_ref, wu_ref) for x in x2]
    for rows, x, (gate, up) in zip(ROW_HALVES, x2, gu):
        y = _down_residual(x, gate, up, wd_ref)
        out_ref[rows, :] = _rms(y, fn_ref[...], NORM_EPS)


def _mix_ffn(x, ya, o, ga, gb, w_ao, w_out, g, wg, wu, wd, fn):
    tm = FFN_TM
    row = lambda w: pl.BlockSpec((tm, w), lambda i: (i, 0))
    return pl.pallas_call(
        _mix_ffn_kernel,
        grid=(N_TOK // tm,),
        in_specs=[row(D_MODEL), row(D_MODEL), row(V_WIDTH), row(D_MODEL), row(D_MODEL),
                  _resident((V_WIDTH, D_MODEL)), _resident((D_MODEL, D_MODEL))]
                 + _ffn_specs() + [_resident((1, D_MODEL))],
        out_specs=row(D_MODEL),
        out_shape=jax.ShapeDtypeStruct((N_TOK, D_MODEL), F32),
        compiler_params=_params(("arbitrary",)),
        name="mix_ffn",
    )(x, ya, o, ga, gb, w_ao, w_out, g, wg, wu, wd, fn)


def kernel(x, positions, rel_bias, ffn1_norm, ffn1_wg, ffn1_wu, ffn1_wd, mix_norm, w_in,
           lambda_q1, lambda_k1, lambda_q2, lambda_k2, subln_g, w_fourier_out, w_attn_out,
           w_out, ffn2_norm, ffn2_wg, ffn2_wu, ffn2_wd, final_norm):
    assert x.shape == (BATCH, SEQ, D_MODEL) and positions.shape == (BATCH, SEQ)
    bf = lambda w: w.astype(BF16)
    row = lambda g: g.reshape(1, -1).astype(F32)
    seq_mat, chan_mat = _dft_constants()
    lam_rows = jnp.concatenate([lambda_q1, lambda_k1, lambda_q2, lambda_k2], axis=0).astype(F32)

    xt = x.reshape(N_TOK, D_MODEL)
    x1, uf, q, k, v, ga, gb = _ffn_proj(xt, row(ffn1_norm[0]), bf(ffn1_wg[0]), bf(ffn1_wu[0]),
                                        bf(ffn1_wd[0]), row(mix_norm[0]), bf(w_in[0]))
    ya = _fourier(uf, seq_mat, chan_mat, bf(w_fourier_out[0]))
    later = [w[0].astype(F32) for w in (ffn2_wg, ffn2_wu, ffn2_wd, w_out, w_attn_out)]
    o, wg2, wu2, wd2, wo, wa = _attention(q, k, v, positions.astype(jnp.int32), rel_bias, lam_rows,
                                          row(subln_g[0]), later)
    out = _mix_ffn(x1, ya, o, ga, gb, wa, wo, row(ffn2_norm[0]), wg2, wu2, wd2, row(final_norm))
    return out.reshape(BATCH, SEQ, D_MODEL)
```

```python
import math

import numpy as np
import jax
import jax.numpy as jnp
from jax import lax
from jax.experimental import pallas as pl
from jax.experimental.pallas import tpu as pltpu

D_MODEL = 1024
BATCH = 8
SEQ = 2048
D_FF = 2816
F_GROUPS = 4
F_GROUP_CH = 128
F_WIDTH = F_GROUPS * F_GROUP_CH
DA_HEADS = 4
DA_HEAD_DIM = 64
DA_V_DIM = 2 * DA_HEAD_DIM
QK_WIDTH = DA_HEADS * 2 * DA_HEAD_DIM
V_WIDTH = DA_HEADS * DA_V_DIM
IN_WIDTH = F_WIDTH + 2 * QK_WIDTH + V_WIDTH + 2 * D_MODEL
REL_BUCKETS = 32
REL_MAX_DIST = 128
NORM_EPS = 1e-6
SUBLN_EPS = 1e-5
LAMBDA_INIT = 0.8 - 0.6 * math.exp(-0.3 * 0)

N_TOK = BATCH * SEQ
LANES = 128
SUBLANES = 8
BF16_ROWS = 16
VMEM_LIMIT = 56 * 1024 * 1024

FFN_TM = 512
SEQ_HALF_ROWS = SEQ // 2 + BF16_ROWS
ATT_TQ = 256
ATT_TILES = 2
ATT_TK = 1024
ATT_SUB = 128
ATT_QK = 256
VT_ROWS = DA_V_DIM + BF16_ROWS
FILL_NONE, FILL_LEFT, FILL_RIGHT = 0, 1, 2
CAST_SLABS = 32
NEAR_SLOTS = (ATT_TQ + 2 * REL_MAX_DIST) // ATT_SUB

assert N_TOK % FFN_TM == 0 and FFN_TM % (2 * BF16_ROWS) == 0
assert SEQ % (ATT_TQ * ATT_TILES) == 0 and SEQ % ATT_TK == 0 and ATT_TK % ATT_QK == 0
assert ATT_QK % ATT_SUB == 0 and ATT_TQ % LANES == 0 and ATT_SUB == LANES
assert (BATCH * SEQ // (ATT_TQ * ATT_TILES)) % CAST_SLABS == 0

BF16 = jnp.bfloat16
F32 = jnp.float32


def _rms(x, g, eps):
    return x * lax.rsqrt(jnp.mean(x * x, axis=-1, keepdims=True) + eps) * g


def _resident(shape):
    return pl.BlockSpec(shape, lambda *_: (0,) * len(shape), pipeline_mode=pl.Buffered(1))


def _params(sem):
    return pltpu.CompilerParams(dimension_semantics=sem, vmem_limit_bytes=VMEM_LIMIT)


ROW_HALVES = [slice(r * (FFN_TM // 2), (r + 1) * (FFN_TM // 2)) for r in range(2)]


def _gate_up(x, g, wg_ref, wu_ref):
    h = _rms(x, g, NORM_EPS).astype(BF16)
    return (jnp.dot(h, wg_ref[...], preferred_element_type=F32),
            jnp.dot(h, wu_ref[...], preferred_element_type=F32))


def _down_residual(x, gate, up, wd_ref):
    a = (gate * jax.nn.sigmoid(gate) * up).astype(BF16)
    return x + 0.5 * jnp.dot(a, wd_ref[...], preferred_element_type=F32)


def _ffn_specs():
    return [_resident((1, D_MODEL)), _resident((D_MODEL, D_FF)), _resident((D_MODEL, D_FF)),
            _resident((D_FF, D_MODEL))]


def _ffn_proj_kernel(x_ref, g_ref, wg_ref, wu_ref, wd_ref, gm_ref, w_ref,
                     x1_ref, uf_ref, q_ref, k_ref, v_ref, ga_ref, gb_ref):
    def proj(rows, x1):
        x1_ref[rows, :] = x1
        h = _rms(x1, gm_ref[...], NORM_EPS).astype(BF16)
        p = jnp.dot(h, w_ref[...], preferred_element_type=F32)
        c = 0
        uf_ref[rows, :] = p[:, c:c + F_WIDTH].astype(BF16)
        c += F_WIDTH
        q_ref[rows, :] = (p[:, c:c + QK_WIDTH]
                          * (DA_HEAD_DIM ** -0.5 * math.log2(math.e))).astype(BF16)
        c += QK_WIDTH
        k_ref[rows, :] = p[:, c:c + QK_WIDTH].astype(BF16)
        c += QK_WIDTH
        v_ref[rows, :] = p[:, c:c + V_WIDTH].astype(BF16)
        c += V_WIDTH
        ga_ref[rows, :] = jax.nn.sigmoid(p[:, c:c + D_MODEL]).astype(BF16)
        c += D_MODEL
        gb_ref[rows, :] = jax.nn.sigmoid(p[:, c:c + D_MODEL]).astype(BF16)

    xs = [x_ref[rows, :] for rows in ROW_HALVES]
    gu = [_gate_up(x, g_ref[...], wg_ref, wu_ref) for x in xs]
    x1 = [_down_residual(x, gate, up, wd_ref) for x, (gate, up) in zip(xs, gu)]
    for rows, x in zip(ROW_HALVES, x1):
        proj(rows, x)


def _ffn_proj(x, g, wg, wu, wd, g_mix, w_in):
    tm = FFN_TM
    row = lambda w: pl.BlockSpec((tm, w), lambda i: (i, 0))
    widths = (F_WIDTH, QK_WIDTH, QK_WIDTH, V_WIDTH, D_MODEL, D_MODEL)
    return pl.pallas_call(
        _ffn_proj_kernel,
        grid=(N_TOK // tm,),
        in_specs=[row(D_MODEL)] + _ffn_specs() + [_resident((1, D_MODEL)),
                                                  _resident((D_MODEL, IN_WIDTH))],
        out_specs=[row(D_MODEL)] + [row(w) for w in widths],
        out_shape=[jax.ShapeDtypeStruct((N_TOK, D_MODEL), F32)]
                  + [jax.ShapeDtypeStruct((N_TOK, w), BF16) for w in widths],
        compiler_params=_params(("arbitrary",)),
        name="ffn_proj",
    )(x, g, wg, wu, wd, g_mix, w_in)


def _dft_constants():
    k = np.arange(SEQ_HALF_ROWS, dtype=np.int64)[:, None]
    n = np.arange(SEQ, dtype=np.int64)[None, :]
    phase = (k * n) % SEQ
    live = (k <= SEQ // 2)
    cos_h = np.where(live, np.cos(2.0 * np.pi * phase / SEQ), 0.0)
    sin_h = np.where(live & (phase % (SEQ // 2) != 0), np.sin(2.0 * np.pi * phase / SEQ), 0.0)
    seq_mat = np.concatenate([cos_h, sin_h], axis=1)
    c = np.arange(F_GROUP_CH, dtype=np.int64)
    angc = 2.0 * np.pi * ((c[:, None] * c[None, :]) % F_GROUP_CH) / F_GROUP_CH
    scale = 1.0 / math.sqrt(SEQ * F_GROUP_CH)
    eye = np.eye(F_GROUPS)
    chan = np.concatenate([np.kron(eye, np.cos(angc)), -np.kron(eye, np.sin(angc))], axis=1) * scale
    return jnp.asarray(seq_mat, dtype=BF16), jnp.asarray(chan, dtype=BF16)


def _flip_rows(x):
    rows, cols = x.shape
    assert rows % SUBLANES == 0
    idx = SUBLANES - 1 - lax.broadcasted_iota(jnp.int32, (SUBLANES, cols), 0)
    groups = [jnp.take_along_axis(x[g * SUBLANES:(g + 1) * SUBLANES, :], idx, axis=0)
              for g in reversed(range(rows // SUBLANES))]
    return jnp.concatenate(groups, axis=0)


def _fourier_kernel(u_ref, seq_ref, chan_ref, w_ref, o_ref, z_ref, d_ref):
    u = u_ref[...]
    for g in range(F_GROUPS):
        cols = slice(g * F_GROUP_CH, (g + 1) * F_GROUP_CH)
        sin_cols = slice(F_WIDTH + g * F_GROUP_CH, F_WIDTH + (g + 1) * F_GROUP_CH)
        z_ref[0:SEQ, cols] = jnp.dot(u[:, cols], chan_ref[cols, cols],
                                     preferred_element_type=F32).astype(BF16)
        z_ref[SEQ:2 * SEQ, cols] = jnp.dot(u[:, cols], chan_ref[cols, sin_cols],
                                           preferred_element_type=F32).astype(BF16)
    half = SEQ // 2
    pc = jnp.dot(seq_ref[:, 0:SEQ], z_ref[0:SEQ, :], preferred_element_type=F32)
    ps = jnp.dot(seq_ref[:, SEQ:2 * SEQ], z_ref[SEQ:2 * SEQ, :], preferred_element_type=F32)
    top = (pc + ps)[0:half, :]
    d_ref[...] = pc - ps
    bottom = _flip_rows(d_ref[1:half + 1, :])
    o_ref[0:half, :] = jnp.dot(top.astype(BF16), w_ref[...],
                               preferred_element_type=F32).astype(BF16)
    o_ref[half:SEQ, :] = jnp.dot(bottom.astype(BF16), w_ref[...],
                                 preferred_element_type=F32).astype(BF16)


def _fourier(uf, seq_mat, chan_mat, w_fo):
    return pl.pallas_call(
        _fourier_kernel,
        grid=(BATCH,),
        in_specs=[
            pl.BlockSpec((SEQ, F_WIDTH), lambda b: (b, 0)),
            _resident((SEQ_HALF_ROWS, 2 * SEQ)),
            _resident((F_WIDTH, 2 * F_WIDTH)),
            _resident((F_WIDTH, D_MODEL)),
        ],
        out_specs=pl.BlockSpec((SEQ, D_MODEL), lambda b: (b, 0)),
        out_shape=jax.ShapeDtypeStruct((N_TOK, D_MODEL), BF16),
        scratch_shapes=[pltpu.VMEM((2 * SEQ, F_WIDTH), BF16),
                        pltpu.VMEM((SEQ_HALF_ROWS, F_WIDTH), F32)],
        compiler_params=_params(("arbitrary",)),
        name="fourier",
    )(uf, seq_mat, chan_mat, w_fo)


def _bucket(rel):
    nb = REL_BUCKETS // 2
    max_exact = nb // 2
    n = jnp.minimum(jnp.abs(rel), REL_MAX_DIST)
    nf = n.astype(F32)
    expo = lax.shift_right_logical(lax.bitcast_convert_type(nf * nf, jnp.int32), 23) - 127
    large = jnp.minimum(expo + 2, nb - 1)
    return jnp.where(rel > 0, nb, 0) + jnp.where(n < max_exact, n, large)


def _attn_kernel(kmin_ref, kmax_ref, qmin_ref, qmax_ref, same_ref,
                 q_ref, k_ref, v_ref, posk_ref, posq_ref, tbl_ref, far_ref, lam_ref, sg_ref,
                 w32_0, w32_1, w32_2, w32_3, w32_4,
                 o_ref, w16_0, w16_1, w16_2, w16_3, w16_4,
                 vt_ref, poskb_ref, bias_ref, fill_ref, cache_ref, cblk_ref):
    b = pl.program_id(0)
    i = pl.program_id(1)
    tq, tk, sub = ATT_TQ, ATT_TK, ATT_SUB
    nblk = SEQ // sub

    @pl.when(i == 0)
    def _():
        for h in range(DA_HEADS):
            for c in range(SEQ // tk):
                blk = v_ref[c * tk:(c + 1) * tk, h * DA_V_DIM:(h + 1) * DA_V_DIM]
                vt_ref[h * VT_ROWS:h * VT_ROWS + DA_V_DIM, c * tk:(c + 1) * tk] = (
                    blk.astype(F32).T.astype(BF16))
            vt_ref[h * VT_ROWS + DA_V_DIM:(h + 1) * VT_ROWS, :] = jnp.ones(
                (VT_ROWS - DA_V_DIM, SEQ), BF16)
        for c in range(SEQ // LANES):
            row = posk_ref[0, :, c * LANES:(c + 1) * LANES]
            halves = [jnp.broadcast_to(part.astype(F32), (LANES, LANES)).T.astype(jnp.int32)
                      for part in (lax.shift_right_arithmetic(row, 16), row & 0xFFFF)]
            poskb_ref[c * LANES:(c + 1) * LANES, :] = lax.shift_left(halves[0], 16) | halves[1]

    @pl.when(jnp.logical_and(b == 0, i == 0))
    def _():
        for e in range(ATT_TILES * nblk):
            fill_ref[e] = FILL_NONE

    lq1, lk1, lq2, lk2 = (lam_ref[r:r + 1, :] for r in range(4))
    lam = (jnp.exp(jnp.sum(lq1 * lk1, axis=-1, keepdims=True))
           - jnp.exp(jnp.sum(lq2 * lk2, axis=-1, keepdims=True)) + LAMBDA_INIT)

    lane = lax.broadcasted_iota(jnp.int32, (tq, DA_V_DIM), 1)
    first_map = lane < DA_HEAD_DIM
    nt = (((1,), (1,)), ((), ()))

    qz = {}
    for j in range(ATT_TILES):
        for h in range(DA_HEADS):
            qh = q_ref[j * tq:(j + 1) * tq, h * DA_V_DIM:(h + 1) * DA_V_DIM]
            zero = jnp.zeros_like(qh)
            qz[j, h] = jnp.concatenate([jnp.where(first_map, qh, zero),
                                        jnp.where(first_map, zero, qh)], axis=0)
    state = {}

    @pl.when(same_ref[b] == 0)
    def _():
        for e in range(ATT_TILES * NEAR_SLOTS):
            cblk_ref[i * ATT_TILES * NEAR_SLOTS + e] = -1

    for j in range(ATT_TILES):
        tile = i * ATT_TILES + j
        posq = posq_ref[j]
        qmin = qmin_ref[b, tile]
        qmax = qmax_ref[b, tile]
        slot = jnp.int32(0)
        for blk in range(nblk):
            rows = slice(blk * sub, (blk + 1) * sub)
            far_right = kmin_ref[b, blk] - qmax >= REL_MAX_DIST
            far_left = qmin - kmax_ref[b, blk] >= REL_MAX_DIST
            far = jnp.logical_or(far_right, far_left)
            near = jnp.logical_not(far)
            cacheable = slot < NEAR_SLOTS
            entry = tile * NEAR_SLOTS + jnp.minimum(slot, NEAR_SLOTS - 1)
            hit = jnp.logical_and(jnp.logical_and(near, cacheable), cblk_ref[entry] == blk)

            code = jnp.where(far_right, FILL_RIGHT, FILL_LEFT)
            fidx = j * nblk + blk

            @pl.when(jnp.logical_and(far, fill_ref[fidx] != code))
            def _(rows=rows, fidx=fidx, far_right=far_right, code=code, j=j):
                for h in range(DA_HEADS):
                    cst = jnp.where(far_right, far_ref[1, h], far_ref[0, h])
                    bias_ref[j, h, rows, :] = jnp.full((sub, tq), cst, F32)
                fill_ref[fidx] = code

            @pl.when(hit)
            def _(rows=rows, fidx=fidx, entry=entry, j=j):
                fill_ref[fidx] = FILL_NONE
                bias_ref[j, :, rows, :] = cache_ref[entry]

            @pl.when(jnp.logical_and(near, jnp.logical_not(hit)))
            def _(rows=rows, blk=blk, fidx=fidx, entry=entry, cacheable=cacheable, j=j, posq=posq):
                fill_ref[fidx] = FILL_NONE
                tbls = [jnp.broadcast_to(tbl_ref[h:h + 1, :], (sub, LANES)) for h in range(DA_HEADS)]
                for t in range(tq // LANES):
                    ln = slice(t * LANES, (t + 1) * LANES)
                    rel = poskb_ref[rows, :] - posq[:, ln]
                    bucket = _bucket(rel)
                    for h in range(DA_HEADS):
                        bias_ref[j, h, rows, ln] = jnp.take_along_axis(
                            tbls[h], bucket, axis=1, mode="promise_in_bounds")

                @pl.when(cacheable)
                def _():
                    cache_ref[entry] = bias_ref[j, :, rows, :]
                    cblk_ref[entry] = blk

            slot = slot + near.astype(jnp.int32)

    def scores(j, c, h):
        parts, m_c = [], None
        for u in range(tk // ATT_QK):
            rows = slice(c * tk + u * ATT_QK, c * tk + (u + 1) * ATT_QK)
            bias = bias_ref[j, h, rows, :]
            s = (lax.dot_general(k_ref[rows, h * DA_V_DIM:(h + 1) * DA_V_DIM], qz[j, h], nt,
                                 preferred_element_type=F32)
                 + jnp.concatenate([bias, bias], axis=1))
            m_u = jnp.max(s, axis=0, keepdims=True)
            m_c = m_u if m_c is None else jnp.maximum(m_c, m_u)
            parts.append(s)
        return parts, m_c

    for src, dst in ((w32_0, w16_0), (w32_1, w16_1), (w32_2, w16_2), (w32_3, w16_3), (w32_4, w16_4)):
        dst[...] = src[...].astype(BF16)

    items = [(j, c, h) for c in range(SEQ // tk) for j in range(ATT_TILES) for h in range(DA_HEADS)]
    ahead = scores(*items[0])
    for n, (j, c, h) in enumerate(items):
        s_parts, m_c = ahead
        if n + 1 < len(items):
            ahead = scores(*items[n + 1])
        vt = vt_ref[h * VT_ROWS:(h + 1) * VT_ROWS, c * tk:(c + 1) * tk]
        if c == 0:
            m_new = m_c
        else:
            m_old, acc_old = state[j, h]
            m_new = jnp.maximum(m_old, m_c)
            alpha = jnp.exp2(m_old - m_new)
        p = jnp.concatenate([jnp.exp2(s - m_new).astype(BF16) for s in s_parts], axis=0)
        acc_new = jnp.dot(vt, p, preferred_element_type=F32)
        if c > 0:
            acc_new = alpha * acc_old + acc_new
        state[j, h] = (m_new, acc_new)
        if c == SEQ // tk - 1:
            l_fin = acc_new[DA_V_DIM:DA_V_DIM + 1, :]
            acc = acc_new[:DA_V_DIM, :]
            r1 = 1.0 / l_fin[:, :tq]
            r2 = lam / l_fin[:, tq:]
            o = (acc[:, :tq] * r1 - acc[:, tq:] * r2).T
            o = _rms(o, sg_ref[...], SUBLN_EPS) * (1.0 - LAMBDA_INIT)
            o_ref[j * tq:(j + 1) * tq, h * DA_V_DIM:(h + 1) * DA_V_DIM] = o.astype(BF16)


def _attention(q, k, v, positions, rel_bias, lam_rows, subln_g, later_weights):
    tq, sub = ATT_TQ, ATT_SUB
    nq = SEQ // tq
    ns = nq // ATT_TILES
    nsub = SEQ // sub
    steps_per_slab = BATCH * ns // CAST_SLABS
    slab_specs = [pl.BlockSpec((w.shape[0] // CAST_SLABS, w.shape[1]),
                               lambda b, i, *_: ((b * ns + i) // steps_per_slab, 0))
                  for w in later_weights]
    log2e = math.log2(math.e)
    tbl_t = rel_bias.T.astype(F32) * log2e
    tbl = jnp.zeros((SUBLANES, LANES), F32).at[:DA_HEADS, :REL_BUCKETS].set(tbl_t)
    nb = REL_BUCKETS // 2
    far = jnp.stack([tbl_t[:, nb - 1], tbl_t[:, 2 * nb - 1]])
    pk = positions.reshape(BATCH, nsub, sub)
    pq = positions.reshape(BATCH, nq, tq)
    posk = positions.reshape(BATCH, 1, SEQ)
    posq = positions.reshape(BATCH * nq, 1, tq)
    same_as_prev = jnp.concatenate([
        jnp.zeros((1,), jnp.int32),
        jnp.all(positions[1:] == positions[:-1], axis=1).astype(jnp.int32)])
    smem = pl.BlockSpec(memory_space=pltpu.SMEM)
    grid_spec = pltpu.PrefetchScalarGridSpec(
        num_scalar_prefetch=5,
        grid=(BATCH, ns),
        in_specs=[
            pl.BlockSpec((ATT_TILES * tq, QK_WIDTH), lambda b, i, *_: (b * ns + i, 0)),
            pl.BlockSpec((SEQ, QK_WIDTH), lambda b, i, *_: (b, 0)),
            pl.BlockSpec((SEQ, V_WIDTH), lambda b, i, *_: (b, 0)),
            pl.BlockSpec((1, 1, SEQ), lambda b, i, *_: (b, 0, 0)),
            pl.BlockSpec((ATT_TILES, 1, tq), lambda b, i, *_: (b * ns + i, 0, 0)),
            _resident((SUBLANES, LANES)),
            smem,
            _resident((4, DA_HEAD_DIM)),
            _resident((1, DA_V_DIM)),
        ] + slab_specs,
        out_specs=[pl.BlockSpec((ATT_TILES * tq, V_WIDTH), lambda b, i, *_: (b * ns + i, 0))] + slab_specs,
        scratch_shapes=[
            pltpu.VMEM((DA_HEADS * VT_ROWS, SEQ), BF16),
            pltpu.VMEM((SEQ, LANES), jnp.int32),
            pltpu.VMEM((ATT_TILES, DA_HEADS, SEQ, tq), F32),
            pltpu.SMEM((ATT_TILES * nsub,), jnp.int32),
            pltpu.VMEM((nq * NEAR_SLOTS, DA_HEADS, sub, tq), F32),
            pltpu.SMEM((nq * NEAR_SLOTS,), jnp.int32),
        ],
    )
    return pl.pallas_call(
        _attn_kernel,
        grid_spec=grid_spec,
        out_shape=[jax.ShapeDtypeStruct((N_TOK, V_WIDTH), BF16)]
                  + [jax.ShapeDtypeStruct(w.shape, BF16) for w in later_weights],
        compiler_params=_params(("arbitrary", "arbitrary")),
        name="diff_attn",
    )(pk.min(-1), pk.max(-1), pq.min(-1), pq.max(-1), same_as_prev,
      q, k, v, posk, posq, tbl, far, lam_rows, subln_g, *later_weights)


def _mix_ffn_kernel(x_ref, ya_ref, o_ref, ga_ref, gb_ref, wa_ref, wo_ref,
                    g_ref, wg_ref, wu_ref, wd_ref, fn_ref, out_ref):
    def mix(rows, yb):
        merged = (ga_ref[rows, :].astype(F32) * ya_ref[rows, :].astype(F32)
                  + gb_ref[rows, :].astype(F32) * yb).astype(BF16)
        return x_ref[rows, :] + jnp.dot(merged, wo_ref[...], preferred_element_type=F32)

    yb = [jnp.dot(o_ref[rows, :], wa_ref[...], preferred_element_type=F32) for rows in ROW_HALVES]
    x2 = [mix(rows, y) for rows, y in zip(ROW_HALVES, yb)]
    gu = [_gate_up(x, g_ref[...], wg_ref, wu_ref) for x in x2]
    for rows, x, (gate, up) in zip(ROW_HALVES, x2, gu):
        y = _down_residual(x, gate, up, wd_ref)
        out_ref[rows, :] = _rms(y, fn_ref[...], NORM_EPS)


def _mix_ffn(x, ya, o, ga, gb, w_ao, w_out, g, wg, wu, wd, fn):
    tm = FFN_TM
    row = lambda w: pl.BlockSpec((tm, w), lambda i: (i, 0))
    return pl.pallas_call(
        _mix_ffn_kernel,
        grid=(N_TOK // tm,),
        in_specs=[row(D_MODEL), row(D_MODEL), row(V_WIDTH), row(D_MODEL), row(D_MODEL),
                  _resident((V_WIDTH, D_MODEL)), _resident((D_MODEL, D_MODEL))]
                 + _ffn_specs() + [_resident((1, D_MODEL))],
        out_specs=row(D_MODEL),
        out_shape=jax.ShapeDtypeStruct((N_TOK, D_MODEL), F32),
        compiler_params=_params(("arbitrary",)),
        name="mix_ffn",
    )(x, ya, o, ga, gb, w_ao, w_out, g, wg, wu, wd, fn)


def kernel(x, positions, rel_bias, ffn1_norm, ffn1_wg, ffn1_wu, ffn1_wd, mix_norm, w_in,
           lambda_q1, lambda_k1, lambda_q2, lambda_k2, subln_g, w_fourier_out, w_attn_out,
           w_out, ffn2_norm, ffn2_wg, ffn2_wu, ffn2_wd, final_norm):
    assert x.shape == (BATCH, SEQ, D_MODEL) and positions.shape == (BATCH, SEQ)
    bf = lambda w: w.astype(BF16)
    row = lambda g: g.reshape(1, -1).astype(F32)
    seq_mat, chan_mat = _dft_constants()
    lam_rows = jnp.concatenate([lambda_q1, lambda_k1, lambda_q2, lambda_k2], axis=0).astype(F32)

    xt = x.reshape(N_TOK, D_MODEL)
    x1, uf, q, k, v, ga, gb = _ffn_proj(xt, row(ffn1_norm[0]), bf(ffn1_wg[0]), bf(ffn1_wu[0]),
                                        bf(ffn1_wd[0]), row(mix_norm[0]), bf(w_in[0]))
    ya = _fourier(uf, seq_mat, chan_mat, bf(w_fourier_out[0]))
    later = [w[0].astype(F32) for w in (ffn2_wg, ffn2_wu, ffn2_wd, w_out, w_attn_out)]
    o, wg2, wu2, wd2, wo, wa = _attention(q, k, v, positions.astype(jnp.int32), rel_bias, lam_rows,
                                          row(subln_g[0]), later)
    out = _mix_ffn(x1, ya, o, ga, gb, wa, wo, row(ffn2_norm[0]), wg2, wu2, wd2, row(final_norm))
    return out.reshape(BATCH, SEQ, D_MODEL)
```

```python
import math

import numpy as np
import jax
import jax.numpy as jnp
from jax import lax
from jax.experimental import pallas as pl
from jax.experimental.pallas import tpu as pltpu

D_MODEL = 1024
BATCH = 8
SEQ = 2048
D_FF = 2816
F_GROUPS = 4
F_GROUP_CH = 128
F_WIDTH = F_GROUPS * F_GROUP_CH
DA_HEADS = 4
DA_HEAD_DIM = 64
DA_V_DIM = 2 * DA_HEAD_DIM
QK_WIDTH = DA_HEADS * 2 * DA_HEAD_DIM
V_WIDTH = DA_HEADS * DA_V_DIM
IN_WIDTH = F_WIDTH + 2 * QK_WIDTH + V_WIDTH + 2 * D_MODEL
REL_BUCKETS = 32
REL_MAX_DIST = 128
NORM_EPS = 1e-6
SUBLN_EPS = 1e-5
LAMBDA_INIT = 0.8 - 0.6 * math.exp(-0.3 * 0)

N_TOK = BATCH * SEQ
LANES = 128
SUBLANES = 8
BF16_ROWS = 16
VMEM_LIMIT = 56 * 1024 * 1024

FFN_TM = 512
SEQ_HALF_ROWS = SEQ // 2 + BF16_ROWS
ATT_TQ = 256
ATT_TILES = 2
ATT_TK = 1024
ATT_SUB = 128
ATT_QK = 256
VT_ROWS = DA_V_DIM + BF16_ROWS
CAST_SLABS = 32
NEAR_SLOTS = (ATT_TQ + 2 * REL_MAX_DIST) // ATT_SUB

assert N_TOK % FFN_TM == 0 and FFN_TM % (2 * BF16_ROWS) == 0
assert SEQ % (ATT_TQ * ATT_TILES) == 0 and SEQ % ATT_TK == 0 and ATT_TK % ATT_QK == 0
assert ATT_QK % ATT_SUB == 0 and ATT_TQ % LANES == 0 and ATT_SUB == LANES
assert (BATCH * SEQ // (ATT_TQ * ATT_TILES)) % CAST_SLABS == 0

BF16 = jnp.bfloat16
F32 = jnp.float32


def _rms(x, g, eps):
    return x * lax.rsqrt(jnp.mean(x * x, axis=-1, keepdims=True) + eps) * g


def _resident(shape):
    return pl.BlockSpec(shape, lambda *_: (0,) * len(shape), pipeline_mode=pl.Buffered(1))


def _params(sem):
    return pltpu.CompilerParams(dimension_semantics=sem, vmem_limit_bytes=VMEM_LIMIT)


ROW_HALVES = [slice(r * (FFN_TM // 2), (r + 1) * (FFN_TM // 2)) for r in range(2)]


def _gate_up(x, g, wg_ref, wu_ref):
    h = _rms(x, g, NORM_EPS).astype(BF16)
    return (jnp.dot(h, wg_ref[...], preferred_element_type=F32),
            jnp.dot(h, wu_ref[...], preferred_element_type=F32))


def _down_residual(x, gate, up, wd_ref):
    a = (gate * jax.nn.sigmoid(gate) * up).astype(BF16)
    return x + 0.5 * jnp.dot(a, wd_ref[...], preferred_element_type=F32)


def _ffn_specs():
    return [_resident((1, D_MODEL)), _resident((D_MODEL, D_FF)), _resident((D_MODEL, D_FF)),
            _resident((D_FF, D_MODEL))]


def _ffn_proj_kernel(x_ref, g_ref, wg_ref, wu_ref, wd_ref, gm_ref, w_ref,
                     x1_ref, uf_ref, q_ref, k_ref, v_ref, ga_ref, gb_ref):
    def proj(rows, x1):
        x1_ref[rows, :] = x1
        h = _rms(x1, gm_ref[...], NORM_EPS).astype(BF16)
        p = jnp.dot(h, w_ref[...], preferred_element_type=F32)
        c = 0
        uf_ref[rows, :] = p[:, c:c + F_WIDTH].astype(BF16)
        c += F_WIDTH
        q_ref[rows, :] = (p[:, c:c + QK_WIDTH]
                          * (DA_HEAD_DIM ** -0.5 * math.log2(math.e))).astype(BF16)
        c += QK_WIDTH
        k_ref[rows, :] = p[:, c:c + QK_WIDTH].astype(BF16)
        c += QK_WIDTH
        v_ref[rows, :] = p[:, c:c + V_WIDTH].astype(BF16)
        c += V_WIDTH
        ga_ref[rows, :] = jax.nn.sigmoid(p[:, c:c + D_MODEL]).astype(BF16)
        c += D_MODEL
        gb_ref[rows, :] = jax.nn.sigmoid(p[:, c:c + D_MODEL]).astype(BF16)

    xs = [x_ref[rows, :] for rows in ROW_HALVES]
    gu = [_gate_up(x, g_ref[...], wg_ref, wu_ref) for x in xs]
    x1 = [_down_residual(x, gate, up, wd_ref) for x, (gate, up) in zip(xs, gu)]
    for rows, x in zip(ROW_HALVES, x1):
        proj(rows, x)


def _ffn_proj(x, g, wg, wu, wd, g_mix, w_in):
    tm = FFN_TM
    row = lambda w: pl.BlockSpec((tm, w), lambda i: (i, 0))
    widths = (F_WIDTH, QK_WIDTH, QK_WIDTH, V_WIDTH, D_MODEL, D_MODEL)
    return pl.pallas_call(
        _ffn_proj_kernel,
        grid=(N_TOK // tm,),
        in_specs=[row(D_MODEL)] + _ffn_specs() + [_resident((1, D_MODEL)),
                                                  _resident((D_MODEL, IN_WIDTH))],
        out_specs=[row(D_MODEL)] + [row(w) for w in widths],
        out_shape=[jax.ShapeDtypeStruct((N_TOK, D_MODEL), F32)]
                  + [jax.ShapeDtypeStruct((N_TOK, w), BF16) for w in widths],
        compiler_params=_params(("arbitrary",)),
        name="ffn_proj",
    )(x, g, wg, wu, wd, g_mix, w_in)


def _dft_constants():
    k = np.arange(SEQ_HALF_ROWS, dtype=np.int64)[:, None]
    n = np.arange(SEQ, dtype=np.int64)[None, :]
    phase = (k * n) % SEQ
    live = (k <= SEQ // 2)
    cos_h = np.where(live, np.cos(2.0 * np.pi * phase / SEQ), 0.0)
    sin_h = np.where(live & (phase % (SEQ // 2) != 0), np.sin(2.0 * np.pi * phase / SEQ), 0.0)
    seq_mat = np.concatenate([cos_h, sin_h], axis=1)
    c = np.arange(F_GROUP_CH, dtype=np.int64)
    angc = 2.0 * np.pi * ((c[:, None] * c[None, :]) % F_GROUP_CH) / F_GROUP_CH
    scale = 1.0 / math.sqrt(SEQ * F_GROUP_CH)
    eye = np.eye(F_GROUPS)
    chan = np.concatenate([np.kron(eye, np.cos(angc)), -np.kron(eye, np.sin(angc))], axis=1) * scale
    return jnp.asarray(seq_mat, dtype=BF16), jnp.asarray(chan, dtype=BF16)


def _flip_rows(x):
    rows, cols = x.shape
    assert rows % SUBLANES == 0
    idx = SUBLANES - 1 - lax.broadcasted_iota(jnp.int32, (SUBLANES, cols), 0)
    groups = [jnp.take_along_axis(x[g * SUBLANES:(g + 1) * SUBLANES, :], idx, axis=0)
              for g in reversed(range(rows // SUBLANES))]
    return jnp.concatenate(groups, axis=0)


def _fourier_kernel(u_ref, seq_ref, chan_ref, w_ref, o_ref, z_ref, d_ref):
    u = u_ref[...]
    for g in range(F_GROUPS):
        cols = slice(g * F_GROUP_CH, (g + 1) * F_GROUP_CH)
        sin_cols = slice(F_WIDTH + g * F_GROUP_CH, F_WIDTH + (g + 1) * F_GROUP_CH)
        z_ref[0:SEQ, cols] = jnp.dot(u[:, cols], chan_ref[cols, cols],
                                     preferred_element_type=F32).astype(BF16)
        z_ref[SEQ:2 * SEQ, cols] = jnp.dot(u[:, cols], chan_ref[cols, sin_cols],
                                           preferred_element_type=F32).astype(BF16)
    half = SEQ // 2
    pc = jnp.dot(seq_ref[:, 0:SEQ], z_ref[0:SEQ, :], preferred_element_type=F32)
    ps = jnp.dot(seq_ref[:, SEQ:2 * SEQ], z_ref[SEQ:2 * SEQ, :], preferred_element_type=F32)
    top = (pc + ps)[0:half, :]
    d_ref[...] = pc - ps
    bottom = _flip_rows(d_ref[1:half + 1, :])
    o_ref[0:half, :] = jnp.dot(top.astype(BF16), w_ref[...],
                               preferred_element_type=F32).astype(BF16)
    o_ref[half:SEQ, :] = jnp.dot(bottom.astype(BF16), w_ref[...],
                                 preferred_element_type=F32).astype(BF16)


def _fourier(uf, seq_mat, chan_mat, w_fo):
    return pl.pallas_call(
        _fourier_kernel,
        grid=(BATCH,),
        in_specs=[
            pl.BlockSpec((SEQ, F_WIDTH), lambda b: (b, 0)),
            _resident((SEQ_HALF_ROWS, 2 * SEQ)),
            _resident((F_WIDTH, 2 * F_WIDTH)),
            _resident((F_WIDTH, D_MODEL)),
        ],
        out_specs=pl.BlockSpec((SEQ, D_MODEL), lambda b: (b, 0)),
        out_shape=jax.ShapeDtypeStruct((N_TOK, D_MODEL), BF16),
        scratch_shapes=[pltpu.VMEM((2 * SEQ, F_WIDTH), BF16),
                        pltpu.VMEM((SEQ_HALF_ROWS, F_WIDTH), F32)],
        compiler_params=_params(("arbitrary",)),
        name="fourier",
    )(uf, seq_mat, chan_mat, w_fo)


def _bucket(rel):
    nb = REL_BUCKETS // 2
    max_exact = nb // 2
    n = jnp.minimum(jnp.abs(rel), REL_MAX_DIST)
    nf = n.astype(F32)
    expo = lax.shift_right_logical(lax.bitcast_convert_type(nf * nf, jnp.int32), 23) - 127
    large = jnp.minimum(expo + 2, nb - 1)
    return jnp.where(rel > 0, nb, 0) + jnp.where(n < max_exact, n, large)


def _attn_kernel(left_ref, right_ref, same_ref,
                 q_ref, k_ref, v_ref, posk_ref, posq_ref, tbl_ref, far_ref, lam_ref, sg_ref,
                 w32_0, w32_1, w32_2, w32_3, w32_4,
                 o_ref, w16_0, w16_1, w16_2, w16_3, w16_4,
                 vt_ref, poskb_ref, bias_ref, fill_ref, cache_ref, cblk_ref):
    b = pl.program_id(0)
    i = pl.program_id(1)
    tq, tk, sub = ATT_TQ, ATT_TK, ATT_SUB
    nblk = SEQ // sub

    @pl.when(i == 0)
    def _():
        for h in range(DA_HEADS):
            for c in range(SEQ // tk):
                blk = v_ref[c * tk:(c + 1) * tk, h * DA_V_DIM:(h + 1) * DA_V_DIM]
                vt_ref[h * VT_ROWS:h * VT_ROWS + DA_V_DIM, c * tk:(c + 1) * tk] = (
                    blk.astype(F32).T.astype(BF16))
            vt_ref[h * VT_ROWS + DA_V_DIM:(h + 1) * VT_ROWS, :] = jnp.ones(
                (VT_ROWS - DA_V_DIM, SEQ), BF16)
        for c in range(SEQ // LANES):
            row = posk_ref[0, :, c * LANES:(c + 1) * LANES]
            halves = [jnp.broadcast_to(part.astype(F32), (LANES, LANES)).T.astype(jnp.int32)
                      for part in (lax.shift_right_arithmetic(row, 16), row & 0xFFFF)]
            poskb_ref[c * LANES:(c + 1) * LANES, :] = lax.shift_left(halves[0], 16) | halves[1]

    @pl.when(jnp.logical_and(b == 0, i == 0))
    def _():
        for e in range(2 * ATT_TILES):
            fill_ref[e] = 0

    lq1, lk1, lq2, lk2 = (lam_ref[r:r + 1, :] for r in range(4))
    lam = (jnp.exp(jnp.sum(lq1 * lk1, axis=-1, keepdims=True))
           - jnp.exp(jnp.sum(lq2 * lk2, axis=-1, keepdims=True)) + LAMBDA_INIT)

    lane = lax.broadcasted_iota(jnp.int32, (tq, DA_V_DIM), 1)
    first_map = lane < DA_HEAD_DIM
    nt = (((1,), (1,)), ((), ()))

    qz = {}
    for j in range(ATT_TILES):
        for h in range(DA_HEADS):
            qh = q_ref[j * tq:(j + 1) * tq, h * DA_V_DIM:(h + 1) * DA_V_DIM]
            zero = jnp.zeros_like(qh)
            qz[j, h] = jnp.concatenate([jnp.where(first_map, qh, zero),
                                        jnp.where(first_map, zero, qh)], axis=0)
    state = {}

    @pl.when(same_ref[b] == 0)
    def _():
        for e in range(ATT_TILES * NEAR_SLOTS):
            cblk_ref[i * ATT_TILES * NEAR_SLOTS + e] = -1

    for j in range(ATT_TILES):
        tile = i * ATT_TILES + j
        posq = posq_ref[j]
        far_left = left_ref[b, tile]
        far_right = right_ref[b, tile]
        near_mask = jnp.bitwise_and(jnp.bitwise_not(jnp.bitwise_or(far_left, far_right)),
                                    (1 << nblk) - 1)
        fill_mask = jnp.bitwise_or(
            jnp.bitwise_and(far_left, jnp.bitwise_not(fill_ref[2 * j])),
            jnp.bitwise_and(far_right, jnp.bitwise_not(fill_ref[2 * j + 1])))
        work_mask = jnp.bitwise_or(near_mask, fill_mask)
        fill_ref[2 * j] = far_left
        fill_ref[2 * j + 1] = far_right
        slot = jnp.int32(0)
        for blk in range(nblk):
            bit = 1 << blk
            rows = slice(blk * sub, (blk + 1) * sub)
            near = jnp.bitwise_and(near_mask, bit) != 0

            @pl.when(jnp.bitwise_and(work_mask, bit) != 0)
            def _(rows=rows, blk=blk, bit=bit, j=j, posq=posq, near=near, slot=slot, tile=tile,
                  fill_mask=fill_mask, far_right=far_right):
                cacheable = slot < NEAR_SLOTS
                entry = tile * NEAR_SLOTS + jnp.minimum(slot, NEAR_SLOTS - 1)
                hit = jnp.logical_and(jnp.logical_and(near, cacheable), cblk_ref[entry] == blk)

                @pl.when(jnp.bitwise_and(fill_mask, bit) != 0)
                def _():
                    to_right = jnp.bitwise_and(far_right, bit) != 0
                    for h in range(DA_HEADS):
                        cst = jnp.where(to_right, far_ref[1, h], far_ref[0, h])
                        bias_ref[j, h, rows, :] = jnp.full((sub, tq), cst, F32)

                @pl.when(hit)
                def _():
                    bias_ref[j, :, rows, :] = cache_ref[entry]

                @pl.when(jnp.logical_and(near, jnp.logical_not(hit)))
                def _():
                    tbls = [jnp.broadcast_to(tbl_ref[h:h + 1, :], (sub, LANES))
                            for h in range(DA_HEADS)]
                    for t in range(tq // LANES):
                        ln = slice(t * LANES, (t + 1) * LANES)
                        rel = poskb_ref[rows, :] - posq[:, ln]
                        bucket = _bucket(rel)
                        for h in range(DA_HEADS):
                            bias_ref[j, h, rows, ln] = jnp.take_along_axis(
                                tbls[h], bucket, axis=1, mode="promise_in_bounds")

                    @pl.when(cacheable)
                    def _():
                        cache_ref[entry] = bias_ref[j, :, rows, :]
                        cblk_ref[entry] = blk

            slot = slot + near.astype(jnp.int32)

    def scores(j, c, h):
        parts, m_c = [], None
        for u in range(tk // ATT_QK):
            rows = slice(c * tk + u * ATT_QK, c * tk + (u + 1) * ATT_QK)
            bias = bias_ref[j, h, rows, :]
            s = (lax.dot_general(k_ref[rows, h * DA_V_DIM:(h + 1) * DA_V_DIM], qz[j, h], nt,
                                 preferred_element_type=F32)
                 + jnp.concatenate([bias, bias], axis=1))
            m_u = jnp.max(s, axis=0, keepdims=True)
            m_c = m_u if m_c is None else jnp.maximum(m_c, m_u)
            parts.append(s)
        return parts, m_c

    for src, dst in ((w32_0, w16_0), (w32_1, w16_1), (w32_2, w16_2), (w32_3, w16_3), (w32_4, w16_4)):
        dst[...] = src[...].astype(BF16)

    items = [(j, c, h) for c in range(SEQ // tk) for j in range(ATT_TILES) for h in range(DA_HEADS)]
    ahead = scores(*items[0])
    for n, (j, c, h) in enumerate(items):
        s_parts, m_c = ahead
        if n + 1 < len(items):
            ahead = scores(*items[n + 1])
        vt = vt_ref[h * VT_ROWS:(h + 1) * VT_ROWS, c * tk:(c + 1) * tk]
        if c == 0:
            m_new = m_c
        else:
            m_old, acc_old = state[j, h]
            m_new = jnp.maximum(m_old, m_c)
            alpha = jnp.exp2(m_old - m_new)
        p = jnp.concatenate([jnp.exp2(s - m_new).astype(BF16) for s in s_parts], axis=0)
        acc_new = jnp.dot(vt, p, preferred_element_type=F32)
        if c > 0:
            acc_new = alpha * acc_old + acc_new
        state[j, h] = (m_new, acc_new)
        if c == SEQ // tk - 1:
            l_fin = acc_new[DA_V_DIM:DA_V_DIM + 1, :]
            acc = acc_new[:DA_V_DIM, :]
            r1 = 1.0 / l_fin[:, :tq]
            r2 = lam / l_fin[:, tq:]
            o = (acc[:, :tq] * r1 - acc[:, tq:] * r2).T
            o = _rms(o, sg_ref[...], SUBLN_EPS) * (1.0 - LAMBDA_INIT)
            o_ref[j * tq:(j + 1) * tq, h * DA_V_DIM:(h + 1) * DA_V_DIM] = o.astype(BF16)


def _attention(q, k, v, positions, rel_bias, lam_rows, subln_g, later_weights):
    tq, sub = ATT_TQ, ATT_SUB
    nq = SEQ // tq
    ns = nq // ATT_TILES
    nsub = SEQ // sub
    steps_per_slab = BATCH * ns // CAST_SLABS
    slab_specs = [pl.BlockSpec((w.shape[0] // CAST_SLABS, w.shape[1]),
                               lambda b, i, *_: ((b * ns + i) // steps_per_slab, 0))
                  for w in later_weights]
    log2e = math.log2(math.e)
    tbl_t = rel_bias.T.astype(F32) * log2e
    tbl = jnp.zeros((SUBLANES, LANES), F32).at[:DA_HEADS, :REL_BUCKETS].set(tbl_t)
    nb = REL_BUCKETS // 2
    far = jnp.stack([tbl_t[:, nb - 1], tbl_t[:, 2 * nb - 1]])
    pk = positions.reshape(BATCH, 1, nsub, sub)
    pq = positions.reshape(BATCH, nq, 1, tq)
    bits = jnp.left_shift(1, jnp.arange(nsub, dtype=jnp.int32))
    far_right = jnp.sum(jnp.where(pk.min(-1) - pq.max(-1) >= REL_MAX_DIST, bits, 0), axis=-1)
    far_left = jnp.sum(jnp.where(pq.min(-1) - pk.max(-1) >= REL_MAX_DIST, bits, 0), axis=-1)
    posk = positions.reshape(BATCH, 1, SEQ)
    posq = positions.reshape(BATCH * nq, 1, tq)
    same_as_prev = jnp.concatenate([
        jnp.zeros((1,), jnp.int32),
        jnp.all(positions[1:] == positions[:-1], axis=1).astype(jnp.int32)])
    smem = pl.BlockSpec(memory_space=pltpu.SMEM)
    grid_spec = pltpu.PrefetchScalarGridSpec(
        num_scalar_prefetch=3,
        grid=(BATCH, ns),
        in_specs=[
            pl.BlockSpec((ATT_TILES * tq, QK_WIDTH), lambda b, i, *_: (b * ns + i, 0)),
            pl.BlockSpec((SEQ, QK_WIDTH), lambda b, i, *_: (b, 0)),
            pl.BlockSpec((SEQ, V_WIDTH), lambda b, i, *_: (b, 0)),
            pl.BlockSpec((1, 1, SEQ), lambda b, i, *_: (b, 0, 0)),
            pl.BlockSpec((ATT_TILES, 1, tq), lambda b, i, *_: (b * ns + i, 0, 0)),
            _resident((SUBLANES, LANES)),
            smem,
            _resident((4, DA_HEAD_DIM)),
            _resident((1, DA_V_DIM)),
        ] + slab_specs,
        out_specs=[pl.BlockSpec((ATT_TILES * tq, V_WIDTH), lambda b, i, *_: (b * ns + i, 0))] + slab_specs,
        scratch_shapes=[
            pltpu.VMEM((DA_HEADS * VT_ROWS, SEQ), BF16),
            pltpu.VMEM((SEQ, LANES), jnp.int32),
            pltpu.VMEM((ATT_TILES, DA_HEADS, SEQ, tq), F32),
            pltpu.SMEM((2 * ATT_TILES,), jnp.int32),
            pltpu.VMEM((nq * NEAR_SLOTS, DA_HEADS, sub, tq), F32),
            pltpu.SMEM((nq * NEAR_SLOTS,), jnp.int32),
        ],
    )
    return pl.pallas_call(
        _attn_kernel,
        grid_spec=grid_spec,
        out_shape=[jax.ShapeDtypeStruct((N_TOK, V_WIDTH), BF16)]
                  + [jax.ShapeDtypeStruct(w.shape, BF16) for w in later_weights],
        compiler_params=_params(("arbitrary", "arbitrary")),
        name="diff_attn",
    )(far_left, far_right, same_as_prev,
      q, k, v, posk, posq, tbl, far, lam_rows, subln_g, *later_weights)


def _mix_ffn_kernel(x_ref, ya_ref, o_ref, ga_ref, gb_ref, wa_ref, wo_ref,
                    g_ref, wg_ref, wu_ref, wd_ref, fn_ref, out_ref):
    def mix(rows, yb):
        merged = (ga_ref[rows, :].astype(F32) * ya_ref[rows, :].astype(F32)
                  + gb_ref[rows, :].astype(F32) * yb).astype(BF16)
        return x_ref[rows, :] + jnp.dot(merged, wo_ref[...], preferred_element_type=F32)

    yb = [jnp.dot(o_ref[rows, :], wa_ref[...], preferred_element_type=F32) for rows in ROW_HALVES]
    x2 = [mix(rows, y) for rows, y in zip(ROW_HALVES, yb)]
    gu = [_gate_up(x, g_ref[...], wg_ref, wu_ref) for x in x2]
    for rows, x, (gate, up) in zip(ROW_HALVES, x2, gu):
        y = _down_residual(x, gate, up, wd_ref)
        out_ref[rows, :] = _rms(y, fn_ref[...], NORM_EPS)


def _mix_ffn(x, ya, o, ga, gb, w_ao, w_out, g, wg, wu, wd, fn):
    tm = FFN_TM
    row = lambda w: pl.BlockSpec((tm, w), lambda i: (i, 0))
    return pl.pallas_call(
        _mix_ffn_kernel,
        grid=(N_TOK // tm,),
        in_specs=[row(D_MODEL), row(D_MODEL), row(V_WIDTH), row(D_MODEL), row(D_MODEL),
                  _resident((V_WIDTH, D_MODEL)), _resident((D_MODEL, D_MODEL))]
                 + _ffn_specs() + [_resident((1, D_MODEL))],
        out_specs=row(D_MODEL),
        out_shape=jax.ShapeDtypeStruct((N_TOK, D_MODEL), F32),
        compiler_params=_params(("arbitrary",)),
        name="mix_ffn",
    )(x, ya, o, ga, gb, w_ao, w_out, g, wg, wu, wd, fn)


def kernel(x, positions, rel_bias, ffn1_norm, ffn1_wg, ffn1_wu, ffn1_wd, mix_norm, w_in,
           lambda_q1, lambda_k1, lambda_q2, lambda_k2, subln_g, w_fourier_out, w_attn_out,
           w_out, ffn2_norm, ffn2_wg, ffn2_wu, ffn2_wd, final_norm):
    assert x.shape == (BATCH, SEQ, D_MODEL) and positions.shape == (BATCH, SEQ)
    bf = lambda w: w.astype(BF16)
    row = lambda g: g.reshape(1, -1).astype(F32)
    seq_mat, chan_mat = _dft_constants()
    lam_rows = jnp.concatenate([lambda_q1, lambda_k1, lambda_q2, lambda_k2], axis=0).astype(F32)

    xt = x.reshape(N_TOK, D_MODEL)
    x1, uf, q, k, v, ga, gb = _ffn_proj(xt, row(ffn1_norm[0]), bf(ffn1_wg[0]), bf(ffn1_wu[0]),
                                        bf(ffn1_wd[0]), row(mix_norm[0]), bf(w_in[0]))
    ya = _fourier(uf, seq_mat, chan_mat, bf(w_fourier_out[0]))
    later = [w[0].astype(F32) for w in (ffn2_wg, ffn2_wu, ffn2_wd, w_out, w_attn_out)]
    o, wg2, wu2, wd2, wo, wa = _attention(q, k, v, positions.astype(jnp.int32), rel_bias, lam_rows,
                                          row(subln_g[0]), later)
    out = _mix_ffn(x1, ya, o, ga, gb, wa, wo, row(ffn2_norm[0]), wg2, wu2, wd2, row(final_norm))
    return out.reshape(BATCH, SEQ, D_MODEL)
```

```python
import math

import numpy as np
import jax
import jax.numpy as jnp
from jax import lax
from jax.experimental import pallas as pl
from jax.experimental.pallas import tpu as pltpu

D_MODEL = 1024
BATCH = 8
SEQ = 2048
D_FF = 2816
F_GROUPS = 4
F_GROUP_CH = 128
F_WIDTH = F_GROUPS * F_GROUP_CH
DA_HEADS = 4
DA_HEAD_DIM = 64
DA_V_DIM = 2 * DA_HEAD_DIM
QK_WIDTH = DA_HEADS * 2 * DA_HEAD_DIM
V_WIDTH = DA_HEADS * DA_V_DIM
IN_WIDTH = F_WIDTH + 2 * QK_WIDTH + V_WIDTH + 2 * D_MODEL
REL_BUCKETS = 32
REL_MAX_DIST = 128
NORM_EPS = 1e-6
SUBLN_EPS = 1e-5
LAMBDA_INIT = 0.8 - 0.6 * math.exp(-0.3 * 0)

N_TOK = BATCH * SEQ
LANES = 128
SUBLANES = 8
BF16_ROWS = 16
VMEM_LIMIT = 56 * 1024 * 1024

FFN_TM = 512
SEQ_HALF_ROWS = SEQ // 2 + BF16_ROWS
ATT_TQ = 256
ATT_TILES = 2
ATT_TK = 1024
ATT_SUB = 128
ATT_QK = 256
VT_ROWS = DA_V_DIM + BF16_ROWS
CAST_SLABS = 32
NEAR_SLOTS = (ATT_TQ + 2 * REL_MAX_DIST) // ATT_SUB

assert N_TOK % FFN_TM == 0 and FFN_TM % (2 * BF16_ROWS) == 0
assert SEQ % (ATT_TQ * ATT_TILES) == 0 and SEQ % ATT_TK == 0 and ATT_TK % ATT_QK == 0
assert ATT_QK % ATT_SUB == 0 and ATT_TQ % LANES == 0 and ATT_SUB == LANES
assert (BATCH * SEQ // (ATT_TQ * ATT_TILES)) % CAST_SLABS == 0

BF16 = jnp.bfloat16
F32 = jnp.float32


def _rms(x, g, eps):
    return x * lax.rsqrt(jnp.mean(x * x, axis=-1, keepdims=True) + eps) * g


def _resident(shape):
    return pl.BlockSpec(shape, lambda *_: (0,) * len(shape), pipeline_mode=pl.Buffered(1))


def _params(sem):
    return pltpu.CompilerParams(dimension_semantics=sem, vmem_limit_bytes=VMEM_LIMIT)


ROW_HALVES = [slice(r * (FFN_TM // 2), (r + 1) * (FFN_TM // 2)) for r in range(2)]


def _gate_up(x, g, wg_ref, wu_ref):
    h = _rms(x, g, NORM_EPS).astype(BF16)
    return (jnp.dot(h, wg_ref[...], preferred_element_type=F32),
            jnp.dot(h, wu_ref[...], preferred_element_type=F32))


def _down_residual(x, gate, up, wd_ref):
    a = (gate * jax.nn.sigmoid(gate) * up).astype(BF16)
    return x + 0.5 * jnp.dot(a, wd_ref[...], preferred_element_type=F32)


def _ffn_specs():
    return [_resident((1, D_MODEL)), _resident((D_MODEL, D_FF)), _resident((D_MODEL, D_FF)),
            _resident((D_FF, D_MODEL))]


def _ffn_proj_kernel(x_ref, g_ref, wg_ref, wu_ref, wd_ref, gm_ref, w_ref,
                     x1_ref, uf_ref, q_ref, k_ref, v_ref, ga_ref, gb_ref):
    def proj(rows, x1):
        x1_ref[rows, :] = x1
        h = _rms(x1, gm_ref[...], NORM_EPS).astype(BF16)
        p = jnp.dot(h, w_ref[...], preferred_element_type=F32)
        c = 0
        uf_ref[rows, :] = p[:, c:c + F_WIDTH].astype(BF16)
        c += F_WIDTH
        q_ref[rows, :] = (p[:, c:c + QK_WIDTH]
                          * (DA_HEAD_DIM ** -0.5 * math.log2(math.e))).astype(BF16)
        c += QK_WIDTH
        k_ref[rows, :] = p[:, c:c + QK_WIDTH].astype(BF16)
        c += QK_WIDTH
        v_ref[rows, :] = p[:, c:c + V_WIDTH].astype(BF16)
        c += V_WIDTH
        ga_ref[rows, :] = jax.nn.sigmoid(p[:, c:c + D_MODEL]).astype(BF16)
        c += D_MODEL
        gb_ref[rows, :] = jax.nn.sigmoid(p[:, c:c + D_MODEL]).astype(BF16)

    xs = [x_ref[rows, :] for rows in ROW_HALVES]
    gu = [_gate_up(x, g_ref[...], wg_ref, wu_ref) for x in xs]
    x1 = [_down_residual(x, gate, up, wd_ref) for x, (gate, up) in zip(xs, gu)]
    for rows, x in zip(ROW_HALVES, x1):
        proj(rows, x)


def _ffn_proj(x, g, wg, wu, wd, g_mix, w_in):
    tm = FFN_TM
    row = lambda w: pl.BlockSpec((tm, w), lambda i: (i, 0))
    widths = (F_WIDTH, QK_WIDTH, QK_WIDTH, V_WIDTH, D_MODEL, D_MODEL)
    return pl.pallas_call(
        _ffn_proj_kernel,
        grid=(N_TOK // tm,),
        in_specs=[row(D_MODEL)] + _ffn_specs() + [_resident((1, D_MODEL)),
                                                  _resident((D_MODEL, IN_WIDTH))],
        out_specs=[row(D_MODEL)] + [row(w) for w in widths],
        out_shape=[jax.ShapeDtypeStruct((N_TOK, D_MODEL), F32)]
                  + [jax.ShapeDtypeStruct((N_TOK, w), BF16) for w in widths],
        compiler_params=_params(("arbitrary",)),
        name="ffn_proj",
    )(x, g, wg, wu, wd, g_mix, w_in)


def _dft_constants():
    k = np.arange(SEQ_HALF_ROWS, dtype=np.int64)[:, None]
    n = np.arange(SEQ, dtype=np.int64)[None, :]
    phase = (k * n) % SEQ
    live = (k <= SEQ // 2)
    cos_h = np.where(live, np.cos(2.0 * np.pi * phase / SEQ), 0.0)
    sin_h = np.where(live & (phase % (SEQ // 2) != 0), np.sin(2.0 * np.pi * phase / SEQ), 0.0)
    seq_mat = np.concatenate([cos_h, sin_h], axis=1)
    c = np.arange(F_GROUP_CH, dtype=np.int64)
    angc = 2.0 * np.pi * ((c[:, None] * c[None, :]) % F_GROUP_CH) / F_GROUP_CH
    scale = 1.0 / math.sqrt(SEQ * F_GROUP_CH)
    eye = np.eye(F_GROUPS)
    chan = np.concatenate([np.kron(eye, np.cos(angc)), -np.kron(eye, np.sin(angc))], axis=1) * scale
    return jnp.asarray(seq_mat, dtype=BF16), jnp.asarray(chan, dtype=BF16)


def _flip_rows(x):
    rows, cols = x.shape
    assert rows % SUBLANES == 0
    idx = SUBLANES - 1 - lax.broadcasted_iota(jnp.int32, (SUBLANES, cols), 0)
    groups = [jnp.take_along_axis(x[g * SUBLANES:(g + 1) * SUBLANES, :], idx, axis=0)
              for g in reversed(range(rows // SUBLANES))]
    return jnp.concatenate(groups, axis=0)


def _fourier_kernel(u_ref, seq_ref, chan_ref, w_ref, o_ref, z_ref, d_ref):
    u = u_ref[...]
    for g in range(F_GROUPS):
        cols = slice(g * F_GROUP_CH, (g + 1) * F_GROUP_CH)
        sin_cols = slice(F_WIDTH + g * F_GROUP_CH, F_WIDTH + (g + 1) * F_GROUP_CH)
        z_ref[0:SEQ, cols] = jnp.dot(u[:, cols], chan_ref[cols, cols],
                                     preferred_element_type=F32).astype(BF16)
        z_ref[SEQ:2 * SEQ, cols] = jnp.dot(u[:, cols], chan_ref[cols, sin_cols],
                                           preferred_element_type=F32).astype(BF16)
    half = SEQ // 2
    pc = jnp.dot(seq_ref[:, 0:SEQ], z_ref[0:SEQ, :], preferred_element_type=F32)
    ps = jnp.dot(seq_ref[:, SEQ:2 * SEQ], z_ref[SEQ:2 * SEQ, :], preferred_element_type=F32)
    top = (pc + ps)[0:half, :]
    d_ref[...] = pc - ps
    bottom = _flip_rows(d_ref[1:half + 1, :])
    o_ref[0:half, :] = jnp.dot(top.astype(BF16), w_ref[...],
                               preferred_element_type=F32).astype(BF16)
    o_ref[half:SEQ, :] = jnp.dot(bottom.astype(BF16), w_ref[...],
                                 preferred_element_type=F32).astype(BF16)


def _fourier(uf, seq_mat, chan_mat, w_fo):
    return pl.pallas_call(
        _fourier_kernel,
        grid=(BATCH,),
        in_specs=[
            pl.BlockSpec((SEQ, F_WIDTH), lambda b: (b, 0)),
            _resident((SEQ_HALF_ROWS, 2 * SEQ)),
            _resident((F_WIDTH, 2 * F_WIDTH)),
            _resident((F_WIDTH, D_MODEL)),
        ],
        out_specs=pl.BlockSpec((SEQ, D_MODEL), lambda b: (b, 0)),
        out_shape=jax.ShapeDtypeStruct((N_TOK, D_MODEL), BF16),
        scratch_shapes=[pltpu.VMEM((2 * SEQ, F_WIDTH), BF16),
                        pltpu.VMEM((SEQ_HALF_ROWS, F_WIDTH), F32)],
        compiler_params=_params(("arbitrary",)),
        name="fourier",
    )(uf, seq_mat, chan_mat, w_fo)


def _bucket(rel):
    nb = REL_BUCKETS // 2
    max_exact = nb // 2
    n = jnp.minimum(jnp.abs(rel), REL_MAX_DIST)
    nf = n.astype(F32)
    expo = lax.shift_right_logical(lax.bitcast_convert_type(nf * nf, jnp.int32), 23) - 127
    large = jnp.minimum(expo + 2, nb - 1)
    return jnp.where(rel > 0, nb, 0) + jnp.where(n < max_exact, n, large)


def _attn_kernel(left_ref, right_ref, same_ref,
                 q_ref, k_ref, v_ref, posk_ref, posq_ref, tbl_ref, far_ref, lam_ref, sg_ref,
                 w32_0, w32_1, w32_2, w32_3, w32_4,
                 o_ref, w16_0, w16_1, w16_2, w16_3, w16_4,
                 vt_ref, poskb_ref, bias_ref, fill_ref, cache_ref, cblk_ref):
    b = pl.program_id(0)
    i = pl.program_id(1)
    tq, tk, sub = ATT_TQ, ATT_TK, ATT_SUB
    nblk = SEQ // sub

    @pl.when(i == 0)
    def _():
        for h in range(DA_HEADS):
            for c in range(SEQ // tk):
                blk = v_ref[c * tk:(c + 1) * tk, h * DA_V_DIM:(h + 1) * DA_V_DIM]
                vt_ref[h * VT_ROWS:h * VT_ROWS + DA_V_DIM, c * tk:(c + 1) * tk] = (
                    blk.astype(F32).T.astype(BF16))
            vt_ref[h * VT_ROWS + DA_V_DIM:(h + 1) * VT_ROWS, :] = jnp.ones(
                (VT_ROWS - DA_V_DIM, SEQ), BF16)
        for c in range(SEQ // LANES):
            row = posk_ref[0, :, c * LANES:(c + 1) * LANES]
            halves = [jnp.broadcast_to(part.astype(F32), (LANES, LANES)).T.astype(jnp.int32)
                      for part in (lax.shift_right_arithmetic(row, 16), row & 0xFFFF)]
            poskb_ref[c * LANES:(c + 1) * LANES, :] = lax.shift_left(halves[0], 16) | halves[1]

    @pl.when(jnp.logical_and(b == 0, i == 0))
    def _():
        for e in range(2 * ATT_TILES):
            fill_ref[e] = 0

    lq1, lk1, lq2, lk2 = (lam_ref[r:r + 1, :] for r in range(4))
    lam = (jnp.exp(jnp.sum(lq1 * lk1, axis=-1, keepdims=True))
           - jnp.exp(jnp.sum(lq2 * lk2, axis=-1, keepdims=True)) + LAMBDA_INIT)

    lane = lax.broadcasted_iota(jnp.int32, (tq, DA_V_DIM), 1)
    first_map = lane < DA_HEAD_DIM
    nt = (((1,), (1,)), ((), ()))

    qz = {}
    for j in range(ATT_TILES):
        for h in range(DA_HEADS):
            qh = q_ref[j * tq:(j + 1) * tq, h * DA_V_DIM:(h + 1) * DA_V_DIM]
            zero = jnp.zeros_like(qh)
            qz[j, h] = jnp.concatenate([jnp.where(first_map, qh, zero),
                                        jnp.where(first_map, zero, qh)], axis=0)
    state = {}

    @pl.when(same_ref[b] == 0)
    def _():
        for e in range(ATT_TILES * NEAR_SLOTS):
            cblk_ref[i * ATT_TILES * NEAR_SLOTS + e] = -1

    for j in range(ATT_TILES):
        tile = i * ATT_TILES + j
        posq = posq_ref[j]
        far_left = left_ref[b, tile]
        far_right = right_ref[b, tile]
        near_mask = jnp.bitwise_and(jnp.bitwise_not(jnp.bitwise_or(far_left, far_right)),
                                    (1 << nblk) - 1)
        fill_mask = jnp.bitwise_or(
            jnp.bitwise_and(far_left, jnp.bitwise_not(fill_ref[2 * j])),
            jnp.bitwise_and(far_right, jnp.bitwise_not(fill_ref[2 * j + 1])))
        work_mask = jnp.bitwise_or(near_mask, fill_mask)
        fill_ref[2 * j] = far_left
        fill_ref[2 * j + 1] = far_right
        slot = jnp.int32(0)
        for blk in range(nblk):
            bit = 1 << blk
            rows = slice(blk * sub, (blk + 1) * sub)
            near = jnp.bitwise_and(near_mask, bit) != 0

            @pl.when(jnp.bitwise_and(work_mask, bit) != 0)
            def _(rows=rows, blk=blk, bit=bit, j=j, posq=posq, near=near, slot=slot, tile=tile,
                  fill_mask=fill_mask, far_right=far_right):
                cacheable = slot < NEAR_SLOTS
                entry = tile * NEAR_SLOTS + jnp.minimum(slot, NEAR_SLOTS - 1)
                hit = jnp.logical_and(jnp.logical_and(near, cacheable), cblk_ref[entry] == blk)

                @pl.when(jnp.bitwise_and(fill_mask, bit) != 0)
                def _():
                    to_right = jnp.bitwise_and(far_right, bit) != 0
                    for h in range(DA_HEADS):
                        cst = jnp.where(to_right, far_ref[1, h], far_ref[0, h])
                        bias_ref[j, h, rows, :] = jnp.full((sub, tq), cst, F32)

                @pl.when(hit)
                def _():
                    bias_ref[j, :, rows, :] = cache_ref[entry]

                @pl.when(jnp.logical_and(near, jnp.logical_not(hit)))
                def _():
                    tbls = [jnp.broadcast_to(tbl_ref[h:h + 1, :], (sub, LANES))
                            for h in range(DA_HEADS)]
                    for t in range(tq // LANES):
                        ln = slice(t * LANES, (t + 1) * LANES)
                        rel = poskb_ref[rows, :] - posq[:, ln]
                        bucket = _bucket(rel)
                        for h in range(DA_HEADS):
                            bias_ref[j, h, rows, ln] = jnp.take_along_axis(
                                tbls[h], bucket, axis=1, mode="promise_in_bounds")

                    @pl.when(cacheable)
                    def _():
                        cache_ref[entry] = bias_ref[j, :, rows, :]
                        cblk_ref[entry] = blk

            slot = slot + near.astype(jnp.int32)

    def scores(j, c, h):
        parts, m_c = [], None
        for u in range(tk // ATT_QK):
            rows = slice(c * tk + u * ATT_QK, c * tk + (u + 1) * ATT_QK)
            bias = bias_ref[j, h, rows, :]
            s = (lax.dot_general(k_ref[rows, h * DA_V_DIM:(h + 1) * DA_V_DIM], qz[j, h], nt,
                                 preferred_element_type=F32)
                 + jnp.concatenate([bias, bias], axis=1))
            m_u = jnp.max(s, axis=0, keepdims=True)
            m_c = m_u if m_c is None else jnp.maximum(m_c, m_u)
            parts.append(s)
        return parts, m_c

    for src, dst in ((w32_0, w16_0), (w32_1, w16_1), (w32_2, w16_2), (w32_3, w16_3), (w32_4, w16_4)):
        dst[...] = src[...].astype(BF16)

    items = [(j, c, h) for j in range(ATT_TILES) for c in range(SEQ // tk) for h in range(DA_HEADS)]
    ahead = scores(*items[0])
    for n, (j, c, h) in enumerate(items):
        s_parts, m_c = ahead
        if n + 1 < len(items):
            ahead = scores(*items[n + 1])
        vt = vt_ref[h * VT_ROWS:(h + 1) * VT_ROWS, c * tk:(c + 1) * tk]
        if c == 0:
            m_new = m_c
        else:
            m_old, acc_old = state[j, h]
            m_new = jnp.maximum(m_old, m_c)
            alpha = jnp.exp2(m_old - m_new)
        p = jnp.concatenate([jnp.exp2(s - m_new).astype(BF16) for s in s_parts], axis=0)
        acc_new = jnp.dot(vt, p, preferred_element_type=F32)
        if c > 0:
            acc_new = alpha * acc_old + acc_new
        state[j, h] = (m_new, acc_new)
        if c == SEQ // tk - 1:
            l_fin = acc_new[DA_V_DIM:DA_V_DIM + 1, :]
            acc = acc_new[:DA_V_DIM, :]
            r1 = 1.0 / l_fin[:, :tq]
            r2 = lam / l_fin[:, tq:]
            o = (acc[:, :tq] * r1 - acc[:, tq:] * r2).T
            o = _rms(o, sg_ref[...], SUBLN_EPS) * (1.0 - LAMBDA_INIT)
            o_ref[j * tq:(j + 1) * tq, h * DA_V_DIM:(h + 1) * DA_V_DIM] = o.astype(BF16)


def _attention(q, k, v, positions, rel_bias, lam_rows, subln_g, later_weights):
    tq, sub = ATT_TQ, ATT_SUB
    nq = SEQ // tq
    ns = nq // ATT_TILES
    nsub = SEQ // sub
    steps_per_slab = BATCH * ns // CAST_SLABS
    slab_specs = [pl.BlockSpec((w.shape[0] // CAST_SLABS, w.shape[1]),
                               lambda b, i, *_: ((b * ns + i) // steps_per_slab, 0))
                  for w in later_weights]
    log2e = math.log2(math.e)
    tbl_t = rel_bias.T.astype(F32) * log2e
    tbl = jnp.zeros((SUBLANES, LANES), F32).at[:DA_HEADS, :REL_BUCKETS].set(tbl_t)
    nb = REL_BUCKETS // 2
    far = jnp.stack([tbl_t[:, nb - 1], tbl_t[:, 2 * nb - 1]])
    pk = positions.reshape(BATCH, 1, nsub, sub)
    pq = positions.reshape(BATCH, nq, 1, tq)
    bits = jnp.left_shift(1, jnp.arange(nsub, dtype=jnp.int32))
    far_right = jnp.sum(jnp.where(pk.min(-1) - pq.max(-1) >= REL_MAX_DIST, bits, 0), axis=-1)
    far_left = jnp.sum(jnp.where(pq.min(-1) - pk.max(-1) >= REL_MAX_DIST, bits, 0), axis=-1)
    posk = positions.reshape(BATCH, 1, SEQ)
    posq = positions.reshape(BATCH * nq, 1, tq)
    same_as_prev = jnp.concatenate([
        jnp.zeros((1,), jnp.int32),
        jnp.all(positions[1:] == positions[:-1], axis=1).astype(jnp.int32)])
    smem = pl.BlockSpec(memory_space=pltpu.SMEM)
    grid_spec = pltpu.PrefetchScalarGridSpec(
        num_scalar_prefetch=3,
        grid=(BATCH, ns),
        in_specs=[
            pl.BlockSpec((ATT_TILES * tq, QK_WIDTH), lambda b, i, *_: (b * ns + i, 0)),
            pl.BlockSpec((SEQ, QK_WIDTH), lambda b, i, *_: (b, 0)),
            pl.BlockSpec((SEQ, V_WIDTH), lambda b, i, *_: (b, 0)),
            pl.BlockSpec((1, 1, SEQ), lambda b, i, *_: (b, 0, 0)),
            pl.BlockSpec((ATT_TILES, 1, tq), lambda b, i, *_: (b * ns + i, 0, 0)),
            _resident((SUBLANES, LANES)),
            smem,
            _resident((4, DA_HEAD_DIM)),
            _resident((1, DA_V_DIM)),
        ] + slab_specs,
        out_specs=[pl.BlockSpec((ATT_TILES * tq, V_WIDTH), lambda b, i, *_: (b * ns + i, 0))] + slab_specs,
        scratch_shapes=[
            pltpu.VMEM((DA_HEADS * VT_ROWS, SEQ), BF16),
            pltpu.VMEM((SEQ, LANES), jnp.int32),
            pltpu.VMEM((ATT_TILES, DA_HEADS, SEQ, tq), F32),
            pltpu.SMEM((2 * ATT_TILES,), jnp.int32),
            pltpu.VMEM((nq * NEAR_SLOTS, DA_HEADS, sub, tq), F32),
            pltpu.SMEM((nq * NEAR_SLOTS,), jnp.int32),
        ],
    )
    return pl.pallas_call(
        _attn_kernel,
        grid_spec=grid_spec,
        out_shape=[jax.ShapeDtypeStruct((N_TOK, V_WIDTH), BF16)]
                  + [jax.ShapeDtypeStruct(w.shape, BF16) for w in later_weights],
        compiler_params=_params(("arbitrary", "arbitrary")),
        name="diff_attn",
    )(far_left, far_right, same_as_prev,
      q, k, v, posk, posq, tbl, far, lam_rows, subln_g, *later_weights)


def _mix_ffn_kernel(x_ref, ya_ref, o_ref, ga_ref, gb_ref, wa_ref, wo_ref,
                    g_ref, wg_ref, wu_ref, wd_ref, fn_ref, out_ref):
    def mix(rows, yb):
        merged = (ga_ref[rows, :].astype(F32) * ya_ref[rows, :].astype(F32)
                  + gb_ref[rows, :].astype(F32) * yb).astype(BF16)
        return x_ref[rows, :] + jnp.dot(merged, wo_ref[...], preferred_element_type=F32)

    yb = [jnp.dot(o_ref[rows, :], wa_ref[...], preferred_element_type=F32) for rows in ROW_HALVES]
    x2 = [mix(rows, y) for rows, y in zip(ROW_HALVES, yb)]
    gu = [_gate_up(x, g_ref[...], wg_ref, wu_ref) for x in x2]
    for rows, x, (gate, up) in zip(ROW_HALVES, x2, gu):
        y = _down_residual(x, gate, up, wd_ref)
        out_ref[rows, :] = _rms(y, fn_ref[...], NORM_EPS)


def _mix_ffn(x, ya, o, ga, gb, w_ao, w_out, g, wg, wu, wd, fn):
    tm = FFN_TM
    row = lambda w: pl.BlockSpec((tm, w), lambda i: (i, 0))
    return pl.pallas_call(
        _mix_ffn_kernel,
        grid=(N_TOK // tm,),
        in_specs=[row(D_MODEL), row(D_MODEL), row(V_WIDTH), row(D_MODEL), row(D_MODEL),
                  _resident((V_WIDTH, D_MODEL)), _resident((D_MODEL, D_MODEL))]
                 + _ffn_specs() + [_resident((1, D_MODEL))],
        out_specs=row(D_MODEL),
        out_shape=jax.ShapeDtypeStruct((N_TOK, D_MODEL), F32),
        compiler_params=_params(("arbitrary",)),
        name="mix_ffn",
    )(x, ya, o, ga, gb, w_ao, w_out, g, wg, wu, wd, fn)


def kernel(x, positions, rel_bias, ffn1_norm, ffn1_wg, ffn1_wu, ffn1_wd, mix_norm, w_in,
           lambda_q1, lambda_k1, lambda_q2, lambda_k2, subln_g, w_fourier_out, w_attn_out,
           w_out, ffn2_norm, ffn2_wg, ffn2_wu, ffn2_wd, final_norm):
    assert x.shape == (BATCH, SEQ, D_MODEL) and positions.shape == (BATCH, SEQ)
    layer0 = lambda w: w.reshape(w.shape[1:])
    bf = lambda w: layer0(w).astype(BF16)
    row = lambda g: g.reshape(1, -1).astype(F32)
    seq_mat, chan_mat = _dft_constants()
    lam_rows = jnp.concatenate([lambda_q1, lambda_k1, lambda_q2, lambda_k2], axis=0).astype(F32)

    xt = x.reshape(N_TOK, D_MODEL)
    x1, uf, q, k, v, ga, gb = _ffn_proj(xt, row(ffn1_norm[0]), bf(ffn1_wg), bf(ffn1_wu),
                                        bf(ffn1_wd), row(mix_norm[0]), bf(w_in))
    ya = _fourier(uf, seq_mat, chan_mat, bf(w_fourier_out))
    later = [layer0(w).astype(F32) for w in (ffn2_wg, ffn2_wu, ffn2_wd, w_out, w_attn_out)]
    o, wg2, wu2, wd2, wo, wa = _attention(q, k, v, positions.astype(jnp.int32), rel_bias, lam_rows,
                                          row(subln_g[0]), later)
    out = _mix_ffn(x1, ya, o, ga, gb, wa, wo, row(ffn2_norm[0]), wg2, wu2, wd2, row(final_norm))
    return out.reshape(BATCH, SEQ, D_MODEL)
```

```python
import math

import numpy as np
import jax
import jax.numpy as jnp
from jax import lax
from jax.experimental import pallas as pl
from jax.experimental.pallas import tpu as pltpu

D_MODEL = 1024
BATCH = 8
SEQ = 2048
D_FF = 2816
F_GROUPS = 4
F_GROUP_CH = 128
F_WIDTH = F_GROUPS * F_GROUP_CH
DA_HEADS = 4
DA_HEAD_DIM = 64
DA_V_DIM = 2 * DA_HEAD_DIM
QK_WIDTH = DA_HEADS * 2 * DA_HEAD_DIM
V_WIDTH = DA_HEADS * DA_V_DIM
IN_WIDTH = F_WIDTH + 2 * QK_WIDTH + V_WIDTH + 2 * D_MODEL
REL_BUCKETS = 32
REL_MAX_DIST = 128
NORM_EPS = 1e-6
SUBLN_EPS = 1e-5
LAMBDA_INIT = 0.8 - 0.6 * math.exp(-0.3 * 0)

N_TOK = BATCH * SEQ
LANES = 128
SUBLANES = 8
BF16_ROWS = 16
VMEM_LIMIT = 56 * 1024 * 1024

FFN_TM = 512
SEQ_HALF_ROWS = SEQ // 2 + BF16_ROWS
ATT_TQ = 256
ATT_TILES = 2
ATT_TK = 1024
ATT_SUB = 128
ATT_QK = 256
VT_ROWS = DA_V_DIM + BF16_ROWS
CAST_STEPS = 16
CAST_SLABS = 32
NEAR_SLOTS = (ATT_TQ + 2 * REL_MAX_DIST) // ATT_SUB

assert N_TOK % FFN_TM == 0 and FFN_TM % (2 * BF16_ROWS) == 0
assert SEQ % (ATT_TQ * ATT_TILES) == 0 and SEQ % ATT_TK == 0 and ATT_TK % ATT_QK == 0
assert ATT_QK % ATT_SUB == 0 and ATT_TQ % LANES == 0 and ATT_SUB == LANES
assert (BATCH * SEQ // (ATT_TQ * ATT_TILES)) % CAST_SLABS == 0
assert D_MODEL % (CAST_STEPS * BF16_ROWS) == 0 and D_FF % (CAST_STEPS * BF16_ROWS) == 0

BF16 = jnp.bfloat16
F32 = jnp.float32


def _rms(x, g, eps):
    return x * lax.rsqrt(jnp.mean(x * x, axis=-1, keepdims=True) + eps) * g


def _resident(shape):
    return pl.BlockSpec(shape, lambda *_: (0,) * len(shape), pipeline_mode=pl.Buffered(1))


def _params(sem):
    return pltpu.CompilerParams(dimension_semantics=sem, vmem_limit_bytes=VMEM_LIMIT)


ROW_HALVES = [slice(r * (FFN_TM // 2), (r + 1) * (FFN_TM // 2)) for r in range(2)]


def _gate_up(x, g, wg_ref, wu_ref):
    h = _rms(x, g, NORM_EPS).astype(BF16)
    return (jnp.dot(h, wg_ref[...], preferred_element_type=F32),
            jnp.dot(h, wu_ref[...], preferred_element_type=F32))


def _down_residual(x, gate, up, wd_ref):
    a = (gate * jax.nn.sigmoid(gate) * up).astype(BF16)
    return x + 0.5 * jnp.dot(a, wd_ref[...], preferred_element_type=F32)


def _ffn_specs():
    return [_resident((1, D_MODEL)), _resident((D_MODEL, D_FF)), _resident((D_MODEL, D_FF)),
            _resident((D_FF, D_MODEL))]


def _ffn_proj_kernel(x_ref, g_ref, wg32_ref, wu32_ref, wd32_ref, gm_ref, win32_ref,
                     x1_ref, uf_ref, q_ref, k_ref, v_ref, ga_ref, gb_ref,
                     wg_ref, wu_ref, wd_ref, w_ref):
    step = pl.program_id(0)

    @pl.when(step < CAST_STEPS)
    def _():
        for src, dst in ((wg32_ref, wg_ref), (wu32_ref, wu_ref), (wd32_ref, wd_ref), (win32_ref, w_ref)):
            rows = src.shape[0]
            dst[pl.ds(pl.multiple_of(step * rows, rows), rows), :] = src[...].astype(BF16)

    def proj(rows, x1):
        x1_ref[rows, :] = x1
        h = _rms(x1, gm_ref[...], NORM_EPS).astype(BF16)
        p = jnp.dot(h, w_ref[...], preferred_element_type=F32)
        c = 0
        uf_ref[rows, :] = p[:, c:c + F_WIDTH].astype(BF16)
        c += F_WIDTH
        q_ref[rows, :] = (p[:, c:c + QK_WIDTH]
                          * (DA_HEAD_DIM ** -0.5 * math.log2(math.e))).astype(BF16)
        c += QK_WIDTH
        k_ref[rows, :] = p[:, c:c + QK_WIDTH].astype(BF16)
        c += QK_WIDTH
        v_ref[rows, :] = p[:, c:c + V_WIDTH].astype(BF16)
        c += V_WIDTH
        ga_ref[rows, :] = jax.nn.sigmoid(p[:, c:c + D_MODEL]).astype(BF16)
        c += D_MODEL
        gb_ref[rows, :] = jax.nn.sigmoid(p[:, c:c + D_MODEL]).astype(BF16)

    @pl.when(step >= CAST_STEPS)
    def _():
        xs = [x_ref[rows, :] for rows in ROW_HALVES]
        gu = [_gate_up(x, g_ref[...], wg_ref, wu_ref) for x in xs]
        x1 = [_down_residual(x, gate, up, wd_ref) for x, (gate, up) in zip(xs, gu)]
        for rows, x in zip(ROW_HALVES, x1):
            proj(rows, x)


def _ffn_proj(x, g, wg, wu, wd, g_mix, w_in):
    tm = FFN_TM
    row = lambda w: pl.BlockSpec((tm, w), lambda s: (jnp.maximum(s - CAST_STEPS, 0), 0))
    slab = lambda w: pl.BlockSpec((w.shape[0] // CAST_STEPS, w.shape[1]),
                                  lambda s: (jnp.minimum(s, CAST_STEPS - 1), 0))
    widths = (F_WIDTH, QK_WIDTH, QK_WIDTH, V_WIDTH, D_MODEL, D_MODEL)
    return pl.pallas_call(
        _ffn_proj_kernel,
        grid=(CAST_STEPS + N_TOK // tm,),
        in_specs=[row(D_MODEL), _resident((1, D_MODEL)), slab(wg), slab(wu), slab(wd),
                  _resident((1, D_MODEL)), slab(w_in)],
        out_specs=[row(D_MODEL)] + [row(w) for w in widths],
        out_shape=[jax.ShapeDtypeStruct((N_TOK, D_MODEL), F32)]
                  + [jax.ShapeDtypeStruct((N_TOK, w), BF16) for w in widths],
        scratch_shapes=[pltpu.VMEM(w.shape, BF16) for w in (wg, wu, wd, w_in)],
        compiler_params=_params(("arbitrary",)),
        name="ffn_proj",
    )(x, g, wg, wu, wd, g_mix, w_in)


def _dft_constants():
    k = np.arange(SEQ_HALF_ROWS, dtype=np.int64)[:, None]
    n = np.arange(SEQ, dtype=np.int64)[None, :]
    phase = (k * n) % SEQ
    live = (k <= SEQ // 2)
    cos_h = np.where(live, np.cos(2.0 * np.pi * phase / SEQ), 0.0)
    sin_h = np.where(live & (phase % (SEQ // 2) != 0), np.sin(2.0 * np.pi * phase / SEQ), 0.0)
    seq_mat = np.concatenate([cos_h, sin_h], axis=1)
    c = np.arange(F_GROUP_CH, dtype=np.int64)
    angc = 2.0 * np.pi * ((c[:, None] * c[None, :]) % F_GROUP_CH) / F_GROUP_CH
    scale = 1.0 / math.sqrt(SEQ * F_GROUP_CH)
    eye = np.eye(F_GROUPS)
    chan = np.concatenate([np.kron(eye, np.cos(angc)), -np.kron(eye, np.sin(angc))], axis=1) * scale
    return jnp.asarray(seq_mat, dtype=BF16), jnp.asarray(chan, dtype=BF16)


def _flip_rows(x):
    rows, cols = x.shape
    assert rows % SUBLANES == 0
    idx = SUBLANES - 1 - lax.broadcasted_iota(jnp.int32, (SUBLANES, cols), 0)
    groups = [jnp.take_along_axis(x[g * SUBLANES:(g + 1) * SUBLANES, :], idx, axis=0)
              for g in reversed(range(rows // SUBLANES))]
    return jnp.concatenate(groups, axis=0)


def _fourier_kernel(u_ref, seq_ref, chan_ref, w_ref, o_ref, z_ref, d_ref):
    u = u_ref[...]
    for g in range(F_GROUPS):
        cols = slice(g * F_GROUP_CH, (g + 1) * F_GROUP_CH)
        sin_cols = slice(F_WIDTH + g * F_GROUP_CH, F_WIDTH + (g + 1) * F_GROUP_CH)
        z_ref[0:SEQ, cols] = jnp.dot(u[:, cols], chan_ref[cols, cols],
                                     preferred_element_type=F32).astype(BF16)
        z_ref[SEQ:2 * SEQ, cols] = jnp.dot(u[:, cols], chan_ref[cols, sin_cols],
                                           preferred_element_type=F32).astype(BF16)
    half = SEQ // 2
    pc = jnp.dot(seq_ref[:, 0:SEQ], z_ref[0:SEQ, :], preferred_element_type=F32)
    ps = jnp.dot(seq_ref[:, SEQ:2 * SEQ], z_ref[SEQ:2 * SEQ, :], preferred_element_type=F32)
    top = (pc + ps)[0:half, :]
    d_ref[...] = pc - ps
    bottom = _flip_rows(d_ref[1:half + 1, :])
    o_ref[0:half, :] = jnp.dot(top.astype(BF16), w_ref[...],
                               preferred_element_type=F32).astype(BF16)
    o_ref[half:SEQ, :] = jnp.dot(bottom.astype(BF16), w_ref[...],
                                 preferred_element_type=F32).astype(BF16)


def _fourier(uf, seq_mat, chan_mat, w_fo):
    return pl.pallas_call(
        _fourier_kernel,
        grid=(BATCH,),
        in_specs=[
            pl.BlockSpec((SEQ, F_WIDTH), lambda b: (b, 0)),
            _resident((SEQ_HALF_ROWS, 2 * SEQ)),
            _resident((F_WIDTH, 2 * F_WIDTH)),
            _resident((F_WIDTH, D_MODEL)),
        ],
        out_specs=pl.BlockSpec((SEQ, D_MODEL), lambda b: (b, 0)),
        out_shape=jax.ShapeDtypeStruct((N_TOK, D_MODEL), BF16),
        scratch_shapes=[pltpu.VMEM((2 * SEQ, F_WIDTH), BF16),
                        pltpu.VMEM((SEQ_HALF_ROWS, F_WIDTH), F32)],
        compiler_params=_params(("arbitrary",)),
        name="fourier",
    )(uf, seq_mat, chan_mat, w_fo)


def _bucket(rel):
    nb = REL_BUCKETS // 2
    max_exact = nb // 2
    n = jnp.minimum(jnp.abs(rel), REL_MAX_DIST)
    nf = n.astype(F32)
    expo = lax.shift_right_logical(lax.bitcast_convert_type(nf * nf, jnp.int32), 23) - 127
    large = jnp.minimum(expo + 2, nb - 1)
    return jnp.where(rel > 0, nb, 0) + jnp.where(n < max_exact, n, large)


def _attn_kernel(left_ref, right_ref, same_ref,
                 q_ref, k_ref, v_ref, posk_ref, posq_ref, tbl_ref, far_ref, lam_ref, sg_ref,
                 w32_0, w32_1, w32_2, w32_3, w32_4,
                 o_ref, w16_0, w16_1, w16_2, w16_3, w16_4,
                 vt_ref, poskb_ref, bias_ref, fill_ref, cache_ref, cblk_ref):
    b = pl.program_id(0)
    i = pl.program_id(1)
    tq, tk, sub = ATT_TQ, ATT_TK, ATT_SUB
    nblk = SEQ // sub

    @pl.when(i == 0)
    def _():
        for h in range(DA_HEADS):
            for c in range(SEQ // tk):
                blk = v_ref[c * tk:(c + 1) * tk, h * DA_V_DIM:(h + 1) * DA_V_DIM]
                vt_ref[h * VT_ROWS:h * VT_ROWS + DA_V_DIM, c * tk:(c + 1) * tk] = (
                    blk.astype(F32).T.astype(BF16))
            vt_ref[h * VT_ROWS + DA_V_DIM:(h + 1) * VT_ROWS, :] = jnp.ones(
                (VT_ROWS - DA_V_DIM, SEQ), BF16)
        for c in range(SEQ // LANES):
            row = posk_ref[0, :, c * LANES:(c + 1) * LANES]
            halves = [jnp.broadcast_to(part.astype(F32), (LANES, LANES)).T.astype(jnp.int32)
                      for part in (lax.shift_right_arithmetic(row, 16), row & 0xFFFF)]
            poskb_ref[c * LANES:(c + 1) * LANES, :] = lax.shift_left(halves[0], 16) | halves[1]

    @pl.when(jnp.logical_and(b == 0, i == 0))
    def _():
        for e in range(2 * ATT_TILES):
            fill_ref[e] = 0

    lq1, lk1, lq2, lk2 = (lam_ref[r:r + 1, :] for r in range(4))
    lam = (jnp.exp(jnp.sum(lq1 * lk1, axis=-1, keepdims=True))
           - jnp.exp(jnp.sum(lq2 * lk2, axis=-1, keepdims=True)) + LAMBDA_INIT)

    lane = lax.broadcasted_iota(jnp.int32, (tq, DA_V_DIM), 1)
    first_map = lane < DA_HEAD_DIM
    nt = (((1,), (1,)), ((), ()))

    qz = {}
    for j in range(ATT_TILES):
        for h in range(DA_HEADS):
            qh = q_ref[j * tq:(j + 1) * tq, h * DA_V_DIM:(h + 1) * DA_V_DIM]
            zero = jnp.zeros_like(qh)
            qz[j, h] = jnp.concatenate([jnp.where(first_map, qh, zero),
                                        jnp.where(first_map, zero, qh)], axis=0)
    state = {}

    @pl.when(same_ref[b] == 0)
    def _():
        for e in range(ATT_TILES * NEAR_SLOTS):
            cblk_ref[i * ATT_TILES * NEAR_SLOTS + e] = -1

    for j in range(ATT_TILES):
        tile = i * ATT_TILES + j
        posq = posq_ref[j]
        far_left = left_ref[b, tile]
        far_right = right_ref[b, tile]
        near_mask = jnp.bitwise_and(jnp.bitwise_not(jnp.bitwise_or(far_left, far_right)),
                                    (1 << nblk) - 1)
        fill_mask = jnp.bitwise_or(
            jnp.bitwise_and(far_left, jnp.bitwise_not(fill_ref[2 * j])),
            jnp.bitwise_and(far_right, jnp.bitwise_not(fill_ref[2 * j + 1])))
        work_mask = jnp.bitwise_or(near_mask, fill_mask)
        fill_ref[2 * j] = far_left
        fill_ref[2 * j + 1] = far_right
        slot = jnp.int32(0)
        for blk in range(nblk):
            bit = 1 << blk
            rows = slice(blk * sub, (blk + 1) * sub)
            near = jnp.bitwise_and(near_mask, bit) != 0

            @pl.when(jnp.bitwise_and(work_mask, bit) != 0)
            def _(rows=rows, blk=blk, bit=bit, j=j, posq=posq, near=near, slot=slot, tile=tile,
                  fill_mask=fill_mask, far_right=far_right):
                cacheable = slot < NEAR_SLOTS
                entry = tile * NEAR_SLOTS + jnp.minimum(slot, NEAR_SLOTS - 1)
                hit = jnp.logical_and(jnp.logical_and(near, cacheable), cblk_ref[entry] == blk)

                @pl.when(jnp.bitwise_and(fill_mask, bit) != 0)
                def _():
                    to_right = jnp.bitwise_and(far_right, bit) != 0
                    for h in range(DA_HEADS):
                        cst = jnp.where(to_right, far_ref[1, h], far_ref[0, h])
                        bias_ref[j, h, rows, :] = jnp.full((sub, tq), cst, F32)

                @pl.when(hit)
                def _():
                    bias_ref[j, :, rows, :] = cache_ref[entry]

                @pl.when(jnp.logical_and(near, jnp.logical_not(hit)))
                def _():
                    tbls = [jnp.broadcast_to(tbl_ref[h:h + 1, :], (sub, LANES))
                            for h in range(DA_HEADS)]
                    for t in range(tq // LANES):
                        ln = slice(t * LANES, (t + 1) * LANES)
                        rel = poskb_ref[rows, :] - posq[:, ln]
                        bucket = _bucket(rel)
                        for h in range(DA_HEADS):
                            bias_ref[j, h, rows, ln] = jnp.take_along_axis(
                                tbls[h], bucket, axis=1, mode="promise_in_bounds")

                    @pl.when(cacheable)
                    def _():
                        cache_ref[entry] = bias_ref[j, :, rows, :]
                        cblk_ref[entry] = blk

            slot = slot + near.astype(jnp.int32)

    def scores(j, c, h):
        parts, m_c = [], None
        for u in range(tk // ATT_QK):
            rows = slice(c * tk + u * ATT_QK, c * tk + (u + 1) * ATT_QK)
            bias = bias_ref[j, h, rows, :]
            s = (lax.dot_general(k_ref[rows, h * DA_V_DIM:(h + 1) * DA_V_DIM], qz[j, h], nt,
                                 preferred_element_type=F32)
                 + jnp.concatenate([bias, bias], axis=1))
            m_u = jnp.max(s, axis=0, keepdims=True)
            m_c = m_u if m_c is None else jnp.maximum(m_c, m_u)
            parts.append(s)
        return parts, m_c

    for src, dst in ((w32_0, w16_0), (w32_1, w16_1), (w32_2, w16_2), (w32_3, w16_3), (w32_4, w16_4)):
        dst[...] = src[...].astype(BF16)

    items = [(j, c, h) for j in range(ATT_TILES) for c in range(SEQ // tk) for h in range(DA_HEADS)]
    ahead = scores(*items[0])
    for n, (j, c, h) in enumerate(items):
        s_parts, m_c = ahead
        if n + 1 < len(items):
            ahead = scores(*items[n + 1])
        vt = vt_ref[h * VT_ROWS:(h + 1) * VT_ROWS, c * tk:(c + 1) * tk]
        if c == 0:
            m_new = m_c
        else:
            m_old, acc_old = state[j, h]
            m_new = jnp.maximum(m_old, m_c)
            alpha = jnp.exp2(m_old - m_new)
        p = jnp.concatenate([jnp.exp2(s - m_new).astype(BF16) for s in s_parts], axis=0)
        acc_new = jnp.dot(vt, p, preferred_element_type=F32)
        if c > 0:
            acc_new = alpha * acc_old + acc_new
        state[j, h] = (m_new, acc_new)
        if c == SEQ // tk - 1:
            l_fin = acc_new[DA_V_DIM:DA_V_DIM + 1, :]
            acc = acc_new[:DA_V_DIM, :]
            r1 = 1.0 / l_fin[:, :tq]
            r2 = lam / l_fin[:, tq:]
            o = (acc[:, :tq] * r1 - acc[:, tq:] * r2).T
            o = _rms(o, sg_ref[...], SUBLN_EPS) * (1.0 - LAMBDA_INIT)
            o_ref[j * tq:(j + 1) * tq, h * DA_V_DIM:(h + 1) * DA_V_DIM] = o.astype(BF16)


def _attention(q, k, v, positions, rel_bias, lam_rows, subln_g, later_weights):
    tq, sub = ATT_TQ, ATT_SUB
    nq = SEQ // tq
    ns = nq // ATT_TILES
    nsub = SEQ // sub
    steps_per_slab = BATCH * ns // CAST_SLABS
    slab_specs = [pl.BlockSpec((w.shape[0] // CAST_SLABS, w.shape[1]),
                               lambda b, i, *_: ((b * ns + i) // steps_per_slab, 0))
                  for w in later_weights]
    log2e = math.log2(math.e)
    tbl_t = rel_bias.T.astype(F32) * log2e
    tbl = jnp.zeros((SUBLANES, LANES), F32).at[:DA_HEADS, :REL_BUCKETS].set(tbl_t)
    nb = REL_BUCKETS // 2
    far = jnp.stack([tbl_t[:, nb - 1], tbl_t[:, 2 * nb - 1]])
    pk = positions.reshape(BATCH, 1, nsub, sub)
    pq = positions.reshape(BATCH, nq, 1, tq)
    bits = jnp.left_shift(1, jnp.arange(nsub, dtype=jnp.int32))
    far_right = jnp.sum(jnp.where(pk.min(-1) - pq.max(-1) >= REL_MAX_DIST, bits, 0), axis=-1)
    far_left = jnp.sum(jnp.where(pq.min(-1) - pk.max(-1) >= REL_MAX_DIST, bits, 0), axis=-1)
    posk = positions.reshape(BATCH, 1, SEQ)
    posq = positions.reshape(BATCH * nq, 1, tq)
    same_as_prev = jnp.concatenate([
        jnp.zeros((1,), jnp.int32),
        jnp.all(positions[1:] == positions[:-1], axis=1).astype(jnp.int32)])
    smem = pl.BlockSpec(memory_space=pltpu.SMEM)
    grid_spec = pltpu.PrefetchScalarGridSpec(
        num_scalar_prefetch=3,
        grid=(BATCH, ns),
        in_specs=[
            pl.BlockSpec((ATT_TILES * tq, QK_WIDTH), lambda b, i, *_: (b * ns + i, 0)),
            pl.BlockSpec((SEQ, QK_WIDTH), lambda b, i, *_: (b, 0)),
            pl.BlockSpec((SEQ, V_WIDTH), lambda b, i, *_: (b, 0)),
            pl.BlockSpec((1, 1, SEQ), lambda b, i, *_: (b, 0, 0)),
            pl.BlockSpec((ATT_TILES, 1, tq), lambda b, i, *_: (b * ns + i, 0, 0)),
            _resident((SUBLANES, LANES)),
            smem,
            _resident((4, DA_HEAD_DIM)),
            _resident((1, DA_V_DIM)),
        ] + slab_specs,
        out_specs=[pl.BlockSpec((ATT_TILES * tq, V_WIDTH), lambda b, i, *_: (b * ns + i, 0))] + slab_specs,
        scratch_shapes=[
            pltpu.VMEM((DA_HEADS * VT_ROWS, SEQ), BF16),
            pltpu.VMEM((SEQ, LANES), jnp.int32),
            pltpu.VMEM((ATT_TILES, DA_HEADS, SEQ, tq), F32),
            pltpu.SMEM((2 * ATT_TILES,), jnp.int32),
            pltpu.VMEM((nq * NEAR_SLOTS, DA_HEADS, sub, tq), F32),
            pltpu.SMEM((nq * NEAR_SLOTS,), jnp.int32),
        ],
    )
    return pl.pallas_call(
        _attn_kernel,
        grid_spec=grid_spec,
        out_shape=[jax.ShapeDtypeStruct((N_TOK, V_WIDTH), BF16)]
                  + [jax.ShapeDtypeStruct(w.shape, BF16) for w in later_weights],
        compiler_params=_params(("arbitrary", "arbitrary")),
        name="diff_attn",
    )(far_left, far_right, same_as_prev,
      q, k, v, posk, posq, tbl, far, lam_rows, subln_g, *later_weights)


def _mix_ffn_kernel(x_ref, ya_ref, o_ref, ga_ref, gb_ref, wa_ref, wo_ref,
                    g_ref, wg_ref, wu_ref, wd_ref, fn_ref, out_ref):
    def mix(rows, yb):
        merged = (ga_ref[rows, :].astype(F32) * ya_ref[rows, :].astype(F32)
                  + gb_ref[rows, :].astype(F32) * yb).astype(BF16)
        return x_ref[rows, :] + jnp.dot(merged, wo_ref[...], preferred_element_type=F32)

    yb = [jnp.dot(o_ref[rows, :], wa_ref[...], preferred_element_type=F32) for rows in ROW_HALVES]
    x2 = [mix(rows, y) for rows, y in zip(ROW_HALVES, yb)]
    gu = [_gate_up(x, g_ref[...], wg_ref, wu_ref) for x in x2]
    for rows, x, (gate, up) in zip(ROW_HALVES, x2, gu):
        y = _down_residual(x, gate, up, wd_ref)
        out_ref[rows, :] = _rms(y, fn_ref[...], NORM_EPS)


def _mix_ffn(x, ya, o, ga, gb, w_ao, w_out, g, wg, wu, wd, fn):
    tm = FFN_TM
    row = lambda w: pl.BlockSpec((tm, w), lambda i: (i, 0))
    return pl.pallas_call(
        _mix_ffn_kernel,
        grid=(N_TOK // tm,),
        in_specs=[row(D_MODEL), row(D_MODEL), row(V_WIDTH), row(D_MODEL), row(D_MODEL),
                  _resident((V_WIDTH, D_MODEL)), _resident((D_MODEL, D_MODEL))]
                 + _ffn_specs() + [_resident((1, D_MODEL))],
        out_specs=row(D_MODEL),
        out_shape=jax.ShapeDtypeStruct((N_TOK, D_MODEL), F32),
        compiler_params=_params(("arbitrary",)),
        name="mix_ffn",
    )(x, ya, o, ga, gb, w_ao, w_out, g, wg, wu, wd, fn)


def kernel(x, positions, rel_bias, ffn1_norm, ffn1_wg, ffn1_wu, ffn1_wd, mix_norm, w_in,
           lambda_q1, lambda_k1, lambda_q2, lambda_k2, subln_g, w_fourier_out, w_attn_out,
           w_out, ffn2_norm, ffn2_wg, ffn2_wu, ffn2_wd, final_norm):
    assert x.shape == (BATCH, SEQ, D_MODEL) and positions.shape == (BATCH, SEQ)
    layer0 = lambda w: w.reshape(w.shape[1:])
    bf = lambda w: layer0(w).astype(BF16)
    row = lambda g: g.reshape(1, -1).astype(F32)
    seq_mat, chan_mat = _dft_constants()
    lam_rows = jnp.concatenate([lambda_q1, lambda_k1, lambda_q2, lambda_k2], axis=0).astype(F32)

    xt = x.reshape(N_TOK, D_MODEL)
    x1, uf, q, k, v, ga, gb = _ffn_proj(xt, row(ffn1_norm[0]), layer0(ffn1_wg).astype(F32), layer0(ffn1_wu).astype(F32),
                                        layer0(ffn1_wd).astype(F32), row(mix_norm[0]), layer0(w_in).astype(F32))
    ya = _fourier(uf, seq_mat, chan_mat, bf(w_fourier_out))
    later = [layer0(w).astype(F32) for w in (ffn2_wg, ffn2_wu, ffn2_wd, w_out, w_attn_out)]
    o, wg2, wu2, wd2, wo, wa = _attention(q, k, v, positions.astype(jnp.int32), rel_bias, lam_rows,
                                          row(subln_g[0]), later)
    out = _mix_ffn(x1, ya, o, ga, gb, wa, wo, row(ffn2_norm[0]), wg2, wu2, wd2, row(final_norm))
    return out.reshape(BATCH, SEQ, D_MODEL)
```

```python
import math

import numpy as np
import jax
import jax.numpy as jnp
from jax import lax
from jax.experimental import pallas as pl
from jax.experimental.pallas import tpu as pltpu

D_MODEL = 1024
BATCH = 8
SEQ = 2048
D_FF = 2816
F_GROUPS = 4
F_GROUP_CH = 128
F_WIDTH = F_GROUPS * F_GROUP_CH
DA_HEADS = 4
DA_HEAD_DIM = 64
DA_V_DIM = 2 * DA_HEAD_DIM
QK_WIDTH = DA_HEADS * 2 * DA_HEAD_DIM
V_WIDTH = DA_HEADS * DA_V_DIM
IN_WIDTH = F_WIDTH + 2 * QK_WIDTH + V_WIDTH + 2 * D_MODEL
REL_BUCKETS = 32
REL_MAX_DIST = 128
NORM_EPS = 1e-6
SUBLN_EPS = 1e-5
LAMBDA_INIT = 0.8 - 0.6 * math.exp(-0.3 * 0)

N_TOK = BATCH * SEQ
LANES = 128
SUBLANES = 8
BF16_ROWS = 16
VMEM_LIMIT = 56 * 1024 * 1024

FFN_TM = 512
SEQ_HALF_ROWS = SEQ // 2 + BF16_ROWS
ATT_TQ = 256
ATT_TILES = 2
ATT_TK = 1024
ATT_SUB = 128
ATT_QK = 256
VT_ROWS = DA_V_DIM + BF16_ROWS
CAST_STEPS = 16
CAST_SLABS = 32
NEAR_SLOTS = (ATT_TQ + 2 * REL_MAX_DIST) // ATT_SUB

assert N_TOK % FFN_TM == 0 and FFN_TM % (2 * BF16_ROWS) == 0
assert SEQ % (ATT_TQ * ATT_TILES) == 0 and SEQ % ATT_TK == 0 and ATT_TK % ATT_QK == 0
assert ATT_QK % ATT_SUB == 0 and ATT_TQ % LANES == 0 and ATT_SUB == LANES
assert (BATCH * SEQ // (ATT_TQ * ATT_TILES)) % CAST_SLABS == 0
assert D_MODEL % (CAST_STEPS * BF16_ROWS) == 0 and D_FF % (CAST_STEPS * BF16_ROWS) == 0

BF16 = jnp.bfloat16
F32 = jnp.float32


def _rms(x, g, eps):
    return x * lax.rsqrt(jnp.mean(x * x, axis=-1, keepdims=True) + eps) * g


def _resident(shape):
    return pl.BlockSpec(shape, lambda *_: (0,) * len(shape), pipeline_mode=pl.Buffered(1))


def _params(sem):
    return pltpu.CompilerParams(dimension_semantics=sem, vmem_limit_bytes=VMEM_LIMIT)


ROW_HALVES = [slice(r * (FFN_TM // 2), (r + 1) * (FFN_TM // 2)) for r in range(2)]


def _gate_up(x, g, wg_ref, wu_ref):
    h = _rms(x, g, NORM_EPS).astype(BF16)
    return (jnp.dot(h, wg_ref[...], preferred_element_type=F32),
            jnp.dot(h, wu_ref[...], preferred_element_type=F32))


def _down_residual(x, gate, up, wd_ref):
    a = (gate * jax.nn.sigmoid(gate) * up).astype(BF16)
    return x + 0.5 * jnp.dot(a, wd_ref[...], preferred_element_type=F32)


def _ffn_specs():
    return [_resident((1, D_MODEL)), _resident((D_MODEL, D_FF)), _resident((D_MODEL, D_FF)),
            _resident((D_FF, D_MODEL))]


def _ffn_proj_kernel(x_ref, g_ref, wg32_ref, wu32_ref, wd32_ref, gm_ref, win32_ref,
                     x1_ref, uf_ref, q_ref, k_ref, v_ref, ga_ref, gb_ref,
                     wg_ref, wu_ref, wd_ref, w_ref):
    step = pl.program_id(0)

    @pl.when(step < CAST_STEPS)
    def _():
        for src, dst in ((wg32_ref, wg_ref), (wu32_ref, wu_ref), (wd32_ref, wd_ref), (win32_ref, w_ref)):
            rows = src.shape[0]
            dst[pl.ds(pl.multiple_of(step * rows, rows), rows), :] = src[...].astype(BF16)

    def proj(rows, x1):
        x1_ref[rows, :] = x1
        h = _rms(x1, gm_ref[...], NORM_EPS).astype(BF16)
        p = jnp.dot(h, w_ref[...], preferred_element_type=F32)
        c = 0
        uf_ref[rows, :] = p[:, c:c + F_WIDTH].astype(BF16)
        c += F_WIDTH
        q_ref[rows, :] = (p[:, c:c + QK_WIDTH]
                          * (DA_HEAD_DIM ** -0.5 * math.log2(math.e))).astype(BF16)
        c += QK_WIDTH
        k_ref[rows, :] = p[:, c:c + QK_WIDTH].astype(BF16)
        c += QK_WIDTH
        v_ref[rows, :] = p[:, c:c + V_WIDTH].astype(BF16)
        c += V_WIDTH
        ga_ref[rows, :] = jax.nn.sigmoid(p[:, c:c + D_MODEL]).astype(BF16)
        c += D_MODEL
        gb_ref[rows, :] = jax.nn.sigmoid(p[:, c:c + D_MODEL]).astype(BF16)

    @pl.when(step >= CAST_STEPS)
    def _():
        xs = [x_ref[rows, :] for rows in ROW_HALVES]
        gu = [_gate_up(x, g_ref[...], wg_ref, wu_ref) for x in xs]
        x1 = [_down_residual(x, gate, up, wd_ref) for x, (gate, up) in zip(xs, gu)]
        for rows, x in zip(ROW_HALVES, x1):
            proj(rows, x)


def _ffn_proj(x, g, wg, wu, wd, g_mix, w_in):
    tm = FFN_TM
    row = lambda w: pl.BlockSpec((tm, w), lambda s: (jnp.maximum(s - CAST_STEPS, 0), 0))
    slab = lambda w: pl.BlockSpec((w.shape[0] // CAST_STEPS, w.shape[1]),
                                  lambda s: (jnp.minimum(s, CAST_STEPS - 1), 0))
    widths = (F_WIDTH, QK_WIDTH, QK_WIDTH, V_WIDTH, D_MODEL, D_MODEL)
    return pl.pallas_call(
        _ffn_proj_kernel,
        grid=(CAST_STEPS + N_TOK // tm,),
        in_specs=[row(D_MODEL), _resident((1, D_MODEL)), slab(wg), slab(wu), slab(wd),
                  _resident((1, D_MODEL)), slab(w_in)],
        out_specs=[row(D_MODEL)] + [row(w) for w in widths],
        out_shape=[jax.ShapeDtypeStruct((N_TOK, D_MODEL), F32)]
                  + [jax.ShapeDtypeStruct((N_TOK, w), BF16) for w in widths],
        scratch_shapes=[pltpu.VMEM(w.shape, BF16) for w in (wg, wu, wd, w_in)],
        compiler_params=_params(("arbitrary",)),
        name="ffn_proj",
    )(x, g, wg, wu, wd, g_mix, w_in)


def _dft_constants():
    k = np.arange(SEQ_HALF_ROWS, dtype=np.int64)[:, None]
    n = np.arange(SEQ, dtype=np.int64)[None, :]
    phase = (k * n) % SEQ
    live = (k <= SEQ // 2)
    cos_h = np.where(live, np.cos(2.0 * np.pi * phase / SEQ), 0.0)
    sin_h = np.where(live & (phase % (SEQ // 2) != 0), np.sin(2.0 * np.pi * phase / SEQ), 0.0)
    seq_mat = np.concatenate([cos_h, sin_h], axis=1)
    c = np.arange(F_GROUP_CH, dtype=np.int64)
    angc = 2.0 * np.pi * ((c[:, None] * c[None, :]) % F_GROUP_CH) / F_GROUP_CH
    scale = 1.0 / math.sqrt(SEQ * F_GROUP_CH)
    eye = np.eye(F_GROUPS)
    chan = np.concatenate([np.kron(eye, np.cos(angc)), -np.kron(eye, np.sin(angc))], axis=1) * scale
    return jnp.asarray(seq_mat, dtype=BF16), jnp.asarray(chan, dtype=BF16)


def _flip_rows(x):
    rows, cols = x.shape
    assert rows % SUBLANES == 0
    idx = SUBLANES - 1 - lax.broadcasted_iota(jnp.int32, (SUBLANES, cols), 0)
    groups = [jnp.take_along_axis(x[g * SUBLANES:(g + 1) * SUBLANES, :], idx, axis=0)
              for g in reversed(range(rows // SUBLANES))]
    return jnp.concatenate(groups, axis=0)


def _fourier_kernel(u_ref, seq_ref, chan_ref, w32_ref, o_ref, z_ref, d_ref, w_ref):
    @pl.when(pl.program_id(0) == 0)
    def _():
        w_ref[...] = w32_ref[...].astype(BF16)

    u = u_ref[...]
    for g in range(F_GROUPS):
        cols = slice(g * F_GROUP_CH, (g + 1) * F_GROUP_CH)
        sin_cols = slice(F_WIDTH + g * F_GROUP_CH, F_WIDTH + (g + 1) * F_GROUP_CH)
        z_ref[0:SEQ, cols] = jnp.dot(u[:, cols], chan_ref[cols, cols],
                                     preferred_element_type=F32).astype(BF16)
        z_ref[SEQ:2 * SEQ, cols] = jnp.dot(u[:, cols], chan_ref[cols, sin_cols],
                                           preferred_element_type=F32).astype(BF16)
    half = SEQ // 2
    pc = jnp.dot(seq_ref[:, 0:SEQ], z_ref[0:SEQ, :], preferred_element_type=F32)
    ps = jnp.dot(seq_ref[:, SEQ:2 * SEQ], z_ref[SEQ:2 * SEQ, :], preferred_element_type=F32)
    top = (pc + ps)[0:half, :]
    d_ref[...] = pc - ps
    bottom = _flip_rows(d_ref[1:half + 1, :])
    o_ref[0:half, :] = jnp.dot(top.astype(BF16), w_ref[...],
                               preferred_element_type=F32).astype(BF16)
    o_ref[half:SEQ, :] = jnp.dot(bottom.astype(BF16), w_ref[...],
                                 preferred_element_type=F32).astype(BF16)


def _fourier(uf, seq_mat, chan_mat, w_fo):
    return pl.pallas_call(
        _fourier_kernel,
        grid=(BATCH,),
        in_specs=[
            pl.BlockSpec((SEQ, F_WIDTH), lambda b: (b, 0)),
            _resident((SEQ_HALF_ROWS, 2 * SEQ)),
            _resident((F_WIDTH, 2 * F_WIDTH)),
            _resident((F_WIDTH, D_MODEL)),
        ],
        out_specs=pl.BlockSpec((SEQ, D_MODEL), lambda b: (b, 0)),
        out_shape=jax.ShapeDtypeStruct((N_TOK, D_MODEL), BF16),
        scratch_shapes=[pltpu.VMEM((2 * SEQ, F_WIDTH), BF16),
                        pltpu.VMEM((SEQ_HALF_ROWS, F_WIDTH), F32),
                        pltpu.VMEM((F_WIDTH, D_MODEL), BF16)],
        compiler_params=_params(("arbitrary",)),
        name="fourier",
    )(uf, seq_mat, chan_mat, w_fo)


def _bucket(rel):
    nb = REL_BUCKETS // 2
    max_exact = nb // 2
    n = jnp.minimum(jnp.abs(rel), REL_MAX_DIST)
    nf = n.astype(F32)
    expo = lax.shift_right_logical(lax.bitcast_convert_type(nf * nf, jnp.int32), 23) - 127
    large = jnp.minimum(expo + 2, nb - 1)
    return jnp.where(rel > 0, nb, 0) + jnp.where(n < max_exact, n, large)


def _attn_kernel(left_ref, right_ref, same_ref,
                 q_ref, k_ref, v_ref, posk_ref, posq_ref, tbl_ref, far_ref, lam_ref, sg_ref,
                 w32_0, w32_1, w32_2, w32_3, w32_4,
                 o_ref, w16_0, w16_1, w16_2, w16_3, w16_4,
                 vt_ref, poskb_ref, bias_ref, fill_ref, cache_ref, cblk_ref):
    b = pl.program_id(0)
    i = pl.program_id(1)
    tq, tk, sub = ATT_TQ, ATT_TK, ATT_SUB
    nblk = SEQ // sub

    @pl.when(i == 0)
    def _():
        for h in range(DA_HEADS):
            for c in range(SEQ // tk):
                blk = v_ref[c * tk:(c + 1) * tk, h * DA_V_DIM:(h + 1) * DA_V_DIM]
                vt_ref[h * VT_ROWS:h * VT_ROWS + DA_V_DIM, c * tk:(c + 1) * tk] = (
                    blk.astype(F32).T.astype(BF16))
            vt_ref[h * VT_ROWS + DA_V_DIM:(h + 1) * VT_ROWS, :] = jnp.ones(
                (VT_ROWS - DA_V_DIM, SEQ), BF16)
        for c in range(SEQ // LANES):
            row = posk_ref[0, :, c * LANES:(c + 1) * LANES]
            halves = [jnp.broadcast_to(part.astype(F32), (LANES, LANES)).T.astype(jnp.int32)
                      for part in (lax.shift_right_arithmetic(row, 16), row & 0xFFFF)]
            poskb_ref[c * LANES:(c + 1) * LANES, :] = lax.shift_left(halves[0], 16) | halves[1]

    @pl.when(jnp.logical_and(b == 0, i == 0))
    def _():
        for e in range(2 * ATT_TILES):
            fill_ref[e] = 0

    lq1, lk1, lq2, lk2 = (lam_ref[r:r + 1, :] for r in range(4))
    lam = (jnp.exp(jnp.sum(lq1 * lk1, axis=-1, keepdims=True))
           - jnp.exp(jnp.sum(lq2 * lk2, axis=-1, keepdims=True)) + LAMBDA_INIT)

    lane = lax.broadcasted_iota(jnp.int32, (tq, DA_V_DIM), 1)
    first_map = lane < DA_HEAD_DIM
    nt = (((1,), (1,)), ((), ()))

    qz = {}
    for j in range(ATT_TILES):
        for h in range(DA_HEADS):
            qh = q_ref[j * tq:(j + 1) * tq, h * DA_V_DIM:(h + 1) * DA_V_DIM]
            zero = jnp.zeros_like(qh)
            qz[j, h] = jnp.concatenate([jnp.where(first_map, qh, zero),
                                        jnp.where(first_map, zero, qh)], axis=0)
    state = {}

    @pl.when(same_ref[b] == 0)
    def _():
        for e in range(ATT_TILES * NEAR_SLOTS):
            cblk_ref[i * ATT_TILES * NEAR_SLOTS + e] = -1

    for j in range(ATT_TILES):
        tile = i * ATT_TILES + j
        posq = posq_ref[j]
        far_left = left_ref[b, tile]
        far_right = right_ref[b, tile]
        near_mask = jnp.bitwise_and(jnp.bitwise_not(jnp.bitwise_or(far_left, far_right)),
                                    (1 << nblk) - 1)
        fill_mask = jnp.bitwise_or(
            jnp.bitwise_and(far_left, jnp.bitwise_not(fill_ref[2 * j])),
            jnp.bitwise_and(far_right, jnp.bitwise_not(fill_ref[2 * j + 1])))
        work_mask = jnp.bitwise_or(near_mask, fill_mask)
        fill_ref[2 * j] = far_left
        fill_ref[2 * j + 1] = far_right
        slot = jnp.int32(0)
        for blk in range(nblk):
            bit = 1 << blk
            rows = slice(blk * sub, (blk + 1) * sub)
            near = jnp.bitwise_and(near_mask, bit) != 0

            @pl.when(jnp.bitwise_and(work_mask, bit) != 0)
            def _(rows=rows, blk=blk, bit=bit, j=j, posq=posq, near=near, slot=slot, tile=tile,
                  fill_mask=fill_mask, far_right=far_right):
                cacheable = slot < NEAR_SLOTS
                entry = tile * NEAR_SLOTS + jnp.minimum(slot, NEAR_SLOTS - 1)
                hit = jnp.logical_and(jnp.logical_and(near, cacheable), cblk_ref[entry] == blk)

                @pl.when(jnp.bitwise_and(fill_mask, bit) != 0)
                def _():
                    to_right = jnp.bitwise_and(far_right, bit) != 0
                    for h in range(DA_HEADS):
                        cst = jnp.where(to_right, far_ref[1, h], far_ref[0, h])
                        bias_ref[j, h, rows, :] = jnp.full((sub, tq), cst, F32)

                @pl.when(hit)
                def _():
                    bias_ref[j, :, rows, :] = cache_ref[entry]

                @pl.when(jnp.logical_and(near, jnp.logical_not(hit)))
                def _():
                    tbls = [jnp.broadcast_to(tbl_ref[h:h + 1, :], (sub, LANES))
                            for h in range(DA_HEADS)]
                    for t in range(tq // LANES):
                        ln = slice(t * LANES, (t + 1) * LANES)
                        rel = poskb_ref[rows, :] - posq[:, ln]
                        bucket = _bucket(rel)
                        for h in range(DA_HEADS):
                            bias_ref[j, h, rows, ln] = jnp.take_along_axis(
                                tbls[h], bucket, axis=1, mode="promise_in_bounds")

                    @pl.when(cacheable)
                    def _():
                        cache_ref[entry] = bias_ref[j, :, rows, :]
                        cblk_ref[entry] = blk

            slot = slot + near.astype(jnp.int32)

    def scores(j, c, h):
        parts, m_c = [], None
        for u in range(tk // ATT_QK):
            rows = slice(c * tk + u * ATT_QK, c * tk + (u + 1) * ATT_QK)
            bias = bias_ref[j, h, rows, :]
            s = (lax.dot_general(k_ref[rows, h * DA_V_DIM:(h + 1) * DA_V_DIM], qz[j, h], nt,
                                 preferred_element_type=F32)
                 + jnp.concatenate([bias, bias], axis=1))
            m_u = jnp.max(s, axis=0, keepdims=True)
            m_c = m_u if m_c is None else jnp.maximum(m_c, m_u)
            parts.append(s)
        return parts, m_c

    for src, dst in ((w32_0, w16_0), (w32_1, w16_1), (w32_2, w16_2), (w32_3, w16_3), (w32_4, w16_4)):
        dst[...] = src[...].astype(BF16)

    items = [(j, c, h) for j in range(ATT_TILES) for c in range(SEQ // tk) for h in range(DA_HEADS)]
    ahead = scores(*items[0])
    for n, (j, c, h) in enumerate(items):
        s_parts, m_c = ahead
        if n + 1 < len(items):
            ahead = scores(*items[n + 1])
        vt = vt_ref[h * VT_ROWS:(h + 1) * VT_ROWS, c * tk:(c + 1) * tk]
        if c == 0:
            m_new = m_c
        else:
            m_old, acc_old = state[j, h]
            m_new = jnp.maximum(m_old, m_c)
            alpha = jnp.exp2(m_old - m_new)
        p = jnp.concatenate([jnp.exp2(s - m_new).astype(BF16) for s in s_parts], axis=0)
        acc_new = jnp.dot(vt, p, preferred_element_type=F32)
        if c > 0:
            acc_new = alpha * acc_old + acc_new
        state[j, h] = (m_new, acc_new)
        if c == SEQ // tk - 1:
            l_fin = acc_new[DA_V_DIM:DA_V_DIM + 1, :]
            acc = acc_new[:DA_V_DIM, :]
            r1 = 1.0 / l_fin[:, :tq]
            r2 = lam / l_fin[:, tq:]
            o = (acc[:, :tq] * r1 - acc[:, tq:] * r2).T
            o = _rms(o, sg_ref[...], SUBLN_EPS) * (1.0 - LAMBDA_INIT)
            o_ref[j * tq:(j + 1) * tq, h * DA_V_DIM:(h + 1) * DA_V_DIM] = o.astype(BF16)


def _attention(q, k, v, positions, rel_bias, lam_rows, subln_g, later_weights):
    tq, sub = ATT_TQ, ATT_SUB
    nq = SEQ // tq
    ns = nq // ATT_TILES
    nsub = SEQ // sub
    steps_per_slab = BATCH * ns // CAST_SLABS
    slab_specs = [pl.BlockSpec((w.shape[0] // CAST_SLABS, w.shape[1]),
                               lambda b, i, *_: ((b * ns + i) // steps_per_slab, 0))
                  for w in later_weights]
    log2e = math.log2(math.e)
    tbl_t = rel_bias.T.astype(F32) * log2e
    tbl = jnp.zeros((SUBLANES, LANES), F32).at[:DA_HEADS, :REL_BUCKETS].set(tbl_t)
    nb = REL_BUCKETS // 2
    far = jnp.stack([tbl_t[:, nb - 1], tbl_t[:, 2 * nb - 1]])
    pk = positions.reshape(BATCH, 1, nsub, sub)
    pq = positions.reshape(BATCH, nq, 1, tq)
    bits = jnp.left_shift(1, jnp.arange(nsub, dtype=jnp.int32))
    far_right = jnp.sum(jnp.where(pk.min(-1) - pq.max(-1) >= REL_MAX_DIST, bits, 0), axis=-1)
    far_left = jnp.sum(jnp.where(pq.min(-1) - pk.max(-1) >= REL_MAX_DIST, bits, 0), axis=-1)
    posk = positions.reshape(BATCH, 1, SEQ)
    posq = positions.reshape(BATCH * nq, 1, tq)
    same_as_prev = jnp.concatenate([
        jnp.zeros((1,), jnp.int32),
        jnp.all(positions[1:] == positions[:-1], axis=1).astype(jnp.int32)])
    smem = pl.BlockSpec(memory_space=pltpu.SMEM)
    grid_spec = pltpu.PrefetchScalarGridSpec(
        num_scalar_prefetch=3,
        grid=(BATCH, ns),
        in_specs=[
            pl.BlockSpec((ATT_TILES * tq, QK_WIDTH), lambda b, i, *_: (b * ns + i, 0)),
            pl.BlockSpec((SEQ, QK_WIDTH), lambda b, i, *_: (b, 0)),
            pl.BlockSpec((SEQ, V_WIDTH), lambda b, i, *_: (b, 0)),
            pl.BlockSpec((1, 1, SEQ), lambda b, i, *_: (b, 0, 0)),
            pl.BlockSpec((ATT_TILES, 1, tq), lambda b, i, *_: (b * ns + i, 0, 0)),
            _resident((SUBLANES, LANES)),
            smem,
            _resident((4, DA_HEAD_DIM)),
            _resident((1, DA_V_DIM)),
        ] + slab_specs,
        out_specs=[pl.BlockSpec((ATT_TILES * tq, V_WIDTH), lambda b, i, *_: (b * ns + i, 0))] + slab_specs,
        scratch_shapes=[
            pltpu.VMEM((DA_HEADS * VT_ROWS, SEQ), BF16),
            pltpu.VMEM((SEQ, LANES), jnp.int32),
            pltpu.VMEM((ATT_TILES, DA_HEADS, SEQ, tq), F32),
            pltpu.SMEM((2 * ATT_TILES,), jnp.int32),
            pltpu.VMEM((nq * NEAR_SLOTS, DA_HEADS, sub, tq), F32),
            pltpu.SMEM((nq * NEAR_SLOTS,), jnp.int32),
        ],
    )
    return pl.pallas_call(
        _attn_kernel,
        grid_spec=grid_spec,
        out_shape=[jax.ShapeDtypeStruct((N_TOK, V_WIDTH), BF16)]
                  + [jax.ShapeDtypeStruct(w.shape, BF16) for w in later_weights],
        compiler_params=_params(("arbitrary", "arbitrary")),
        name="diff_attn",
    )(far_left, far_right, same_as_prev,
      q, k, v, posk, posq, tbl, far, lam_rows, subln_g, *later_weights)


def _mix_ffn_kernel(x_ref, ya_ref, o_ref, ga_ref, gb_ref, wa_ref, wo_ref,
                    g_ref, wg_ref, wu_ref, wd_ref, fn_ref, out_ref):
    def mix(rows, yb):
        merged = (ga_ref[rows, :].astype(F32) * ya_ref[rows, :].astype(F32)
                  + gb_ref[rows, :].astype(F32) * yb).astype(BF16)
        return x_ref[rows, :] + jnp.dot(merged, wo_ref[...], preferred_element_type=F32)

    yb = [jnp.dot(o_ref[rows, :], wa_ref[...], preferred_element_type=F32) for rows in ROW_HALVES]
    x2 = [mix(rows, y) for rows, y in zip(ROW_HALVES, yb)]
    gu = [_gate_up(x, g_ref[...], wg_ref, wu_ref) for x in x2]
    for rows, x, (gate, up) in zip(ROW_HALVES, x2, gu):
        y = _down_residual(x, gate, up, wd_ref)
        out_ref[rows, :] = _rms(y, fn_ref[...], NORM_EPS)


def _mix_ffn(x, ya, o, ga, gb, w_ao, w_out, g, wg, wu, wd, fn):
    tm = FFN_TM
    row = lambda w: pl.BlockSpec((tm, w), lambda i: (i, 0))
    return pl.pallas_call(
        _mix_ffn_kernel,
        grid=(N_TOK // tm,),
        in_specs=[row(D_MODEL), row(D_MODEL), row(V_WIDTH), row(D_MODEL), row(D_MODEL),
                  _resident((V_WIDTH, D_MODEL)), _resident((D_MODEL, D_MODEL))]
                 + _ffn_specs() + [_resident((1, D_MODEL))],
        out_specs=row(D_MODEL),
        out_shape=jax.ShapeDtypeStruct((N_TOK, D_MODEL), F32),
        compiler_params=_params(("arbitrary",)),
        name="mix_ffn",
    )(x, ya, o, ga, gb, w_ao, w_out, g, wg, wu, wd, fn)


def kernel(x, positions, rel_bias, ffn1_norm, ffn1_wg, ffn1_wu, ffn1_wd, mix_norm, w_in,
           lambda_q1, lambda_k1, lambda_q2, lambda_k2, subln_g, w_fourier_out, w_attn_out,
           w_out, ffn2_norm, ffn2_wg, ffn2_wu, ffn2_wd, final_norm):
    assert x.shape == (BATCH, SEQ, D_MODEL) and positions.shape == (BATCH, SEQ)
    w32 = lambda w: w.reshape(w.shape[1:]).astype(F32)
    row = lambda g: g.reshape(1, -1).astype(F32)
    seq_mat, chan_mat = _dft_constants()
    lam_rows = jnp.concatenate([lambda_q1, lambda_k1, lambda_q2, lambda_k2], axis=0).astype(F32)

    xt = x.reshape(N_TOK, D_MODEL)
    x1, uf, q, k, v, ga, gb = _ffn_proj(xt, row(ffn1_norm[0]), w32(ffn1_wg), w32(ffn1_wu), w32(ffn1_wd),
                                        row(mix_norm[0]), w32(w_in))
    ya = _fourier(uf, seq_mat, chan_mat, w32(w_fourier_out))
    later = [w32(w) for w in (ffn2_wg, ffn2_wu, ffn2_wd, w_out, w_attn_out)]
    o, wg2, wu2, wd2, wo, wa = _attention(q, k, v, positions.astype(jnp.int32), rel_bias, lam_rows,
                                          row(subln_g[0]), later)
    out = _mix_ffn(x1, ya, o, ga, gb, wa, wo, row(ffn2_norm[0]), wg2, wu2, wd2, row(final_norm))
    return out.reshape(BATCH, SEQ, D_MODEL)
```

```python
import math

import numpy as np
import jax
import jax.numpy as jnp
from jax import lax
from jax.experimental import pallas as pl
from jax.experimental.pallas import tpu as pltpu

D_MODEL = 1024
BATCH = 8
SEQ = 2048
D_FF = 2816
F_GROUPS = 4
F_GROUP_CH = 128
F_WIDTH = F_GROUPS * F_GROUP_CH
DA_HEADS = 4
DA_HEAD_DIM = 64
DA_V_DIM = 2 * DA_HEAD_DIM
QK_WIDTH = DA_HEADS * 2 * DA_HEAD_DIM
V_WIDTH = DA_HEADS * DA_V_DIM
IN_WIDTH = F_WIDTH + 2 * QK_WIDTH + V_WIDTH + 2 * D_MODEL
REL_BUCKETS = 32
REL_MAX_DIST = 128
NORM_EPS = 1e-6
SUBLN_EPS = 1e-5
LAMBDA_INIT = 0.8 - 0.6 * math.exp(-0.3 * 0)

N_TOK = BATCH * SEQ
LANES = 128
SUBLANES = 8
BF16_ROWS = 16
VMEM_LIMIT = 56 * 1024 * 1024

FFN_TM = 512
SEQ_HALF_ROWS = SEQ // 2 + BF16_ROWS
ATT_TQ = 256
ATT_TILES = 2
ATT_TK = 1024
ATT_SUB = 128
ATT_QK = 256
VT_ROWS = DA_V_DIM + BF16_ROWS
CAST_STEPS = 16
NEAR_SLOTS = (ATT_TQ + 2 * REL_MAX_DIST) // ATT_SUB

assert N_TOK % FFN_TM == 0 and FFN_TM % (2 * BF16_ROWS) == 0
assert SEQ % (ATT_TQ * ATT_TILES) == 0 and SEQ % ATT_TK == 0 and ATT_TK % ATT_QK == 0
assert ATT_QK % ATT_SUB == 0 and ATT_TQ % LANES == 0 and ATT_SUB == LANES
assert all(n % (CAST_STEPS * BF16_ROWS) == 0 for n in (D_MODEL, D_FF, V_WIDTH))

BF16 = jnp.bfloat16
F32 = jnp.float32


def _rms(x, g, eps):
    return x * lax.rsqrt(jnp.mean(x * x, axis=-1, keepdims=True) + eps) * g


def _resident(shape):
    return pl.BlockSpec(shape, lambda *_: (0,) * len(shape), pipeline_mode=pl.Buffered(1))


def _params(sem):
    return pltpu.CompilerParams(dimension_semantics=sem, vmem_limit_bytes=VMEM_LIMIT)


ROW_HALVES = [slice(r * (FFN_TM // 2), (r + 1) * (FFN_TM // 2)) for r in range(2)]


def _gate_up(x, g, wg_ref, wu_ref):
    h = _rms(x, g, NORM_EPS).astype(BF16)
    return (jnp.dot(h, wg_ref[...], preferred_element_type=F32),
            jnp.dot(h, wu_ref[...], preferred_element_type=F32))


def _down_residual(x, gate, up, wd_ref):
    a = (gate * jax.nn.sigmoid(gate) * up).astype(BF16)
    return x + 0.5 * jnp.dot(a, wd_ref[...], preferred_element_type=F32)


def _cast_slabs(step, pairs):
    for src, dst in pairs:
        rows = src.shape[0]
        dst[pl.ds(pl.multiple_of(step * rows, rows), rows), :] = src[...].astype(BF16)


def _tile_spec(width):
    return pl.BlockSpec((FFN_TM, width), lambda s: (jnp.maximum(s - CAST_STEPS, 0), 0))


def _slab_spec(w):
    return pl.BlockSpec((w.shape[0] // CAST_STEPS, w.shape[1]),
                        lambda s: (jnp.minimum(s, CAST_STEPS - 1), 0))


def _ffn_proj_kernel(x_ref, g_ref, wg32_ref, wu32_ref, wd32_ref, gm_ref, win32_ref,
                     x1_ref, uf_ref, q_ref, k_ref, v_ref, ga_ref, gb_ref,
                     wg_ref, wu_ref, wd_ref, w_ref):
    step = pl.program_id(0)

    @pl.when(step < CAST_STEPS)
    def _():
        _cast_slabs(step, ((wg32_ref, wg_ref), (wu32_ref, wu_ref), (wd32_ref, wd_ref), (win32_ref, w_ref)))

    def proj(rows, x1):
        x1_ref[rows, :] = x1
        h = _rms(x1, gm_ref[...], NORM_EPS).astype(BF16)
        p = jnp.dot(h, w_ref[...], preferred_element_type=F32)
        c = 0
        uf_ref[rows, :] = p[:, c:c + F_WIDTH].astype(BF16)
        c += F_WIDTH
        q_ref[rows, :] = (p[:, c:c + QK_WIDTH]
                          * (DA_HEAD_DIM ** -0.5 * math.log2(math.e))).astype(BF16)
        c += QK_WIDTH
        k_ref[rows, :] = p[:, c:c + QK_WIDTH].astype(BF16)
        c += QK_WIDTH
        v_ref[rows, :] = p[:, c:c + V_WIDTH].astype(BF16)
        c += V_WIDTH
        ga_ref[rows, :] = jax.nn.sigmoid(p[:, c:c + D_MODEL]).astype(BF16)
        c += D_MODEL
        gb_ref[rows, :] = jax.nn.sigmoid(p[:, c:c + D_MODEL]).astype(BF16)

    @pl.when(step >= CAST_STEPS)
    def _():
        xs = [x_ref[rows, :] for rows in ROW_HALVES]
        gu = [_gate_up(x, g_ref[...], wg_ref, wu_ref) for x in xs]
        x1 = [_down_residual(x, gate, up, wd_ref) for x, (gate, up) in zip(xs, gu)]
        for rows, x in zip(ROW_HALVES, x1):
            proj(rows, x)


def _ffn_proj(x, g, wg, wu, wd, g_mix, w_in):
    tm, row, slab = FFN_TM, _tile_spec, _slab_spec
    widths = (F_WIDTH, QK_WIDTH, QK_WIDTH, V_WIDTH, D_MODEL, D_MODEL)
    return pl.pallas_call(
        _ffn_proj_kernel,
        grid=(CAST_STEPS + N_TOK // tm,),
        in_specs=[row(D_MODEL), _resident((1, D_MODEL)), slab(wg), slab(wu), slab(wd),
                  _resident((1, D_MODEL)), slab(w_in)],
        out_specs=[row(D_MODEL)] + [row(w) for w in widths],
        out_shape=[jax.ShapeDtypeStruct((N_TOK, D_MODEL), F32)]
                  + [jax.ShapeDtypeStruct((N_TOK, w), BF16) for w in widths],
        scratch_shapes=[pltpu.VMEM(w.shape, BF16) for w in (wg, wu, wd, w_in)],
        compiler_params=_params(("arbitrary",)),
        name="ffn_proj",
    )(x, g, wg, wu, wd, g_mix, w_in)


def _dft_constants():
    k = np.arange(SEQ_HALF_ROWS, dtype=np.int64)[:, None]
    n = np.arange(SEQ, dtype=np.int64)[None, :]
    phase = (k * n) % SEQ
    live = (k <= SEQ // 2)
    cos_h = np.where(live, np.cos(2.0 * np.pi * phase / SEQ), 0.0)
    sin_h = np.where(live & (phase % (SEQ // 2) != 0), np.sin(2.0 * np.pi * phase / SEQ), 0.0)
    seq_mat = np.concatenate([cos_h, sin_h], axis=1)
    c = np.arange(F_GROUP_CH, dtype=np.int64)
    angc = 2.0 * np.pi * ((c[:, None] * c[None, :]) % F_GROUP_CH) / F_GROUP_CH
    scale = 1.0 / math.sqrt(SEQ * F_GROUP_CH)
    eye = np.eye(F_GROUPS)
    chan = np.concatenate([np.kron(eye, np.cos(angc)), -np.kron(eye, np.sin(angc))], axis=1) * scale
    return jnp.asarray(seq_mat, dtype=BF16), jnp.asarray(chan, dtype=BF16)


def _flip_rows(x):
    rows, cols = x.shape
    assert rows % SUBLANES == 0
    idx = SUBLANES - 1 - lax.broadcasted_iota(jnp.int32, (SUBLANES, cols), 0)
    groups = [jnp.take_along_axis(x[g * SUBLANES:(g + 1) * SUBLANES, :], idx, axis=0)
              for g in reversed(range(rows // SUBLANES))]
    return jnp.concatenate(groups, axis=0)


def _fourier_kernel(u_ref, seq_ref, chan_ref, w32_ref, o_ref, z_ref, d_ref, w_ref):
    @pl.when(pl.program_id(0) == 0)
    def _():
        w_ref[...] = w32_ref[...].astype(BF16)

    u = u_ref[...]
    for g in range(F_GROUPS):
        cols = slice(g * F_GROUP_CH, (g + 1) * F_GROUP_CH)
        sin_cols = slice(F_WIDTH + g * F_GROUP_CH, F_WIDTH + (g + 1) * F_GROUP_CH)
        z_ref[0:SEQ, cols] = jnp.dot(u[:, cols], chan_ref[cols, cols],
                                     preferred_element_type=F32).astype(BF16)
        z_ref[SEQ:2 * SEQ, cols] = jnp.dot(u[:, cols], chan_ref[cols, sin_cols],
                                           preferred_element_type=F32).astype(BF16)
    half = SEQ // 2
    pc = jnp.dot(seq_ref[:, 0:SEQ], z_ref[0:SEQ, :], preferred_element_type=F32)
    ps = jnp.dot(seq_ref[:, SEQ:2 * SEQ], z_ref[SEQ:2 * SEQ, :], preferred_element_type=F32)
    top = (pc + ps)[0:half, :]
    d_ref[...] = pc - ps
    bottom = _flip_rows(d_ref[1:half + 1, :])
    o_ref[0:half, :] = jnp.dot(top.astype(BF16), w_ref[...],
                               preferred_element_type=F32).astype(BF16)
    o_ref[half:SEQ, :] = jnp.dot(bottom.astype(BF16), w_ref[...],
                                 preferred_element_type=F32).astype(BF16)


def _fourier(uf, seq_mat, chan_mat, w_fo):
    return pl.pallas_call(
        _fourier_kernel,
        grid=(BATCH,),
        in_specs=[
            pl.BlockSpec((SEQ, F_WIDTH), lambda b: (b, 0)),
            _resident((SEQ_HALF_ROWS, 2 * SEQ)),
            _resident((F_WIDTH, 2 * F_WIDTH)),
            _resident((F_WIDTH, D_MODEL)),
        ],
        out_specs=pl.BlockSpec((SEQ, D_MODEL), lambda b: (b, 0)),
        out_shape=jax.ShapeDtypeStruct((N_TOK, D_MODEL), BF16),
        scratch_shapes=[pltpu.VMEM((2 * SEQ, F_WIDTH), BF16),
                        pltpu.VMEM((SEQ_HALF_ROWS, F_WIDTH), F32),
                        pltpu.VMEM((F_WIDTH, D_MODEL), BF16)],
        compiler_params=_params(("arbitrary",)),
        name="fourier",
    )(uf, seq_mat, chan_mat, w_fo)


def _bucket(rel):
    nb = REL_BUCKETS // 2
    max_exact = nb // 2
    n = jnp.minimum(jnp.abs(rel), REL_MAX_DIST)
    nf = n.astype(F32)
    expo = lax.shift_right_logical(lax.bitcast_convert_type(nf * nf, jnp.int32), 23) - 127
    large = jnp.minimum(expo + 2, nb - 1)
    return jnp.where(rel > 0, nb, 0) + jnp.where(n < max_exact, n, large)


def _attn_kernel(left_ref, right_ref, same_ref,
                 q_ref, k_ref, v_ref, posk_ref, posq_ref, tbl_ref, far_ref, lam_ref, sg_ref,
                 o_ref,
                 vt_ref, poskb_ref, bias_ref, fill_ref, cache_ref, cblk_ref):
    b = pl.program_id(0)
    i = pl.program_id(1)
    tq, tk, sub = ATT_TQ, ATT_TK, ATT_SUB
    nblk = SEQ // sub

    @pl.when(i == 0)
    def _():
        for h in range(DA_HEADS):
            for c in range(SEQ // tk):
                blk = v_ref[c * tk:(c + 1) * tk, h * DA_V_DIM:(h + 1) * DA_V_DIM]
                vt_ref[h * VT_ROWS:h * VT_ROWS + DA_V_DIM, c * tk:(c + 1) * tk] = (
                    blk.astype(F32).T.astype(BF16))
            vt_ref[h * VT_ROWS + DA_V_DIM:(h + 1) * VT_ROWS, :] = jnp.ones(
                (VT_ROWS - DA_V_DIM, SEQ), BF16)
        for c in range(SEQ // LANES):
            row = posk_ref[0, :, c * LANES:(c + 1) * LANES]
            halves = [jnp.broadcast_to(part.astype(F32), (LANES, LANES)).T.astype(jnp.int32)
                      for part in (lax.shift_right_arithmetic(row, 16), row & 0xFFFF)]
            poskb_ref[c * LANES:(c + 1) * LANES, :] = lax.shift_left(halves[0], 16) | halves[1]

    @pl.when(jnp.logical_and(b == 0, i == 0))
    def _():
        for e in range(2 * ATT_TILES):
            fill_ref[e] = 0

    lq1, lk1, lq2, lk2 = (lam_ref[r:r + 1, :] for r in range(4))
    lam = (jnp.exp(jnp.sum(lq1 * lk1, axis=-1, keepdims=True))
           - jnp.exp(jnp.sum(lq2 * lk2, axis=-1, keepdims=True)) + LAMBDA_INIT)

    lane = lax.broadcasted_iota(jnp.int32, (tq, DA_V_DIM), 1)
    first_map = lane < DA_HEAD_DIM
    nt = (((1,), (1,)), ((), ()))

    qz = {}
    for j in range(ATT_TILES):
        for h in range(DA_HEADS):
            qh = q_ref[j * tq:(j + 1) * tq, h * DA_V_DIM:(h + 1) * DA_V_DIM]
            zero = jnp.zeros_like(qh)
            qz[j, h] = jnp.concatenate([jnp.where(first_map, qh, zero),
                                        jnp.where(first_map, zero, qh)], axis=0)
    state = {}

    @pl.when(same_ref[b] == 0)
    def _():
        for e in range(ATT_TILES * NEAR_SLOTS):
            cblk_ref[i * ATT_TILES * NEAR_SLOTS + e] = -1

    for j in range(ATT_TILES):
        tile = i * ATT_TILES + j
        posq = posq_ref[j]
        far_left = left_ref[b, tile]
        far_right = right_ref[b, tile]
        near_mask = jnp.bitwise_and(jnp.bitwise_not(jnp.bitwise_or(far_left, far_right)),
                                    (1 << nblk) - 1)
        fill_mask = jnp.bitwise_or(
            jnp.bitwise_and(far_left, jnp.bitwise_not(fill_ref[2 * j])),
            jnp.bitwise_and(far_right, jnp.bitwise_not(fill_ref[2 * j + 1])))
        work_mask = jnp.bitwise_or(near_mask, fill_mask)
        fill_ref[2 * j] = far_left
        fill_ref[2 * j + 1] = far_right
        slot = jnp.int32(0)
        for blk in range(nblk):
            bit = 1 << blk
            rows = slice(blk * sub, (blk + 1) * sub)
            near = jnp.bitwise_and(near_mask, bit) != 0

            @pl.when(jnp.bitwise_and(work_mask, bit) != 0)
            def _(rows=rows, blk=blk, bit=bit, j=j, posq=posq, near=near, slot=slot, tile=tile,
                  fill_mask=fill_mask, far_right=far_right):
                cacheable = slot < NEAR_SLOTS
                entry = tile * NEAR_SLOTS + jnp.minimum(slot, NEAR_SLOTS - 1)
                hit = jnp.logical_and(jnp.logical_and(near, cacheable), cblk_ref[entry] == blk)

                @pl.when(jnp.bitwise_and(fill_mask, bit) != 0)
                def _():
                    to_right = jnp.bitwise_and(far_right, bit) != 0
                    for h in range(DA_HEADS):
                        cst = jnp.where(to_right, far_ref[1, h], far_ref[0, h])
                        bias_ref[j, h, rows, :] = jnp.full((sub, tq), cst, F32)

                @pl.when(hit)
                def _():
                    bias_ref[j, :, rows, :] = cache_ref[entry]

                @pl.when(jnp.logical_and(near, jnp.logical_not(hit)))
                def _():
                    tbls = [jnp.broadcast_to(tbl_ref[h:h + 1, :], (sub, LANES))
                            for h in range(DA_HEADS)]
                    for t in range(tq // LANES):
                        ln = slice(t * LANES, (t + 1) * LANES)
                        rel = poskb_ref[rows, :] - posq[:, ln]
                        bucket = _bucket(rel)
                        for h in range(DA_HEADS):
                            bias_ref[j, h, rows, ln] = jnp.take_along_axis(
                                tbls[h], bucket, axis=1, mode="promise_in_bounds")

                    @pl.when(cacheable)
                    def _():
                        cache_ref[entry] = bias_ref[j, :, rows, :]
                        cblk_ref[entry] = blk

            slot = slot + near.astype(jnp.int32)

    def scores(j, c, h):
        parts, m_c = [], None
        for u in range(tk // ATT_QK):
            rows = slice(c * tk + u * ATT_QK, c * tk + (u + 1) * ATT_QK)
            bias = bias_ref[j, h, rows, :]
            s = (lax.dot_general(k_ref[rows, h * DA_V_DIM:(h + 1) * DA_V_DIM], qz[j, h], nt,
                                 preferred_element_type=F32)
                 + jnp.concatenate([bias, bias], axis=1))
            m_u = jnp.max(s, axis=0, keepdims=True)
            m_c = m_u if m_c is None else jnp.maximum(m_c, m_u)
            parts.append(s)
        return parts, m_c

    items = [(j, c, h) for j in range(ATT_TILES) for c in range(SEQ // tk) for h in range(DA_HEADS)]
    ahead = scores(*items[0])
    for n, (j, c, h) in enumerate(items):
        s_parts, m_c = ahead
        if n + 1 < len(items):
            ahead = scores(*items[n + 1])
        vt = vt_ref[h * VT_ROWS:(h + 1) * VT_ROWS, c * tk:(c + 1) * tk]
        if c == 0:
            m_new = m_c
        else:
            m_old, acc_old = state[j, h]
            m_new = jnp.maximum(m_old, m_c)
            alpha = jnp.exp2(m_old - m_new)
        p = jnp.concatenate([jnp.exp2(s - m_new).astype(BF16) for s in s_parts], axis=0)
        acc_new = jnp.dot(vt, p, preferred_element_type=F32)
        if c > 0:
            acc_new = alpha * acc_old + acc_new
        state[j, h] = (m_new, acc_new)
        if c == SEQ // tk - 1:
            l_fin = acc_new[DA_V_DIM:DA_V_DIM + 1, :]
            acc = acc_new[:DA_V_DIM, :]
            r1 = 1.0 / l_fin[:, :tq]
            r2 = lam / l_fin[:, tq:]
            o = (acc[:, :tq] * r1 - acc[:, tq:] * r2).T
            o = _rms(o, sg_ref[...], SUBLN_EPS) * (1.0 - LAMBDA_INIT)
            o_ref[j * tq:(j + 1) * tq, h * DA_V_DIM:(h + 1) * DA_V_DIM] = o.astype(BF16)


def _attention(q, k, v, positions, rel_bias, lam_rows, subln_g):
    tq, sub = ATT_TQ, ATT_SUB
    nq = SEQ // tq
    ns = nq // ATT_TILES
    nsub = SEQ // sub
    log2e = math.log2(math.e)
    tbl_t = rel_bias.T.astype(F32) * log2e
    tbl = jnp.zeros((SUBLANES, LANES), F32).at[:DA_HEADS, :REL_BUCKETS].set(tbl_t)
    nb = REL_BUCKETS // 2
    far = jnp.stack([tbl_t[:, nb - 1], tbl_t[:, 2 * nb - 1]])
    pk = positions.reshape(BATCH, 1, nsub, sub)
    pq = positions.reshape(BATCH, nq, 1, tq)
    bits = jnp.left_shift(1, jnp.arange(nsub, dtype=jnp.int32))
    far_right = jnp.sum(jnp.where(pk.min(-1) - pq.max(-1) >= REL_MAX_DIST, bits, 0), axis=-1)
    far_left = jnp.sum(jnp.where(pq.min(-1) - pk.max(-1) >= REL_MAX_DIST, bits, 0), axis=-1)
    posk = positions.reshape(BATCH, 1, SEQ)
    posq = positions.reshape(BATCH * nq, 1, tq)
    same_as_prev = jnp.concatenate([
        jnp.zeros((1,), jnp.int32),
        jnp.all(positions[1:] == positions[:-1], axis=1).astype(jnp.int32)])
    smem = pl.BlockSpec(memory_space=pltpu.SMEM)
    grid_spec = pltpu.PrefetchScalarGridSpec(
        num_scalar_prefetch=3,
        grid=(BATCH, ns),
        in_specs=[
            pl.BlockSpec((ATT_TILES * tq, QK_WIDTH), lambda b, i, *_: (b * ns + i, 0)),
            pl.BlockSpec((SEQ, QK_WIDTH), lambda b, i, *_: (b, 0)),
            pl.BlockSpec((SEQ, V_WIDTH), lambda b, i, *_: (b, 0)),
            pl.BlockSpec((1, 1, SEQ), lambda b, i, *_: (b, 0, 0)),
            pl.BlockSpec((ATT_TILES, 1, tq), lambda b, i, *_: (b * ns + i, 0, 0)),
            _resident((SUBLANES, LANES)),
            smem,
            _resident((4, DA_HEAD_DIM)),
            _resident((1, DA_V_DIM)),
        ],
        out_specs=pl.BlockSpec((ATT_TILES * tq, V_WIDTH), lambda b, i, *_: (b * ns + i, 0)),
        scratch_shapes=[
            pltpu.VMEM((DA_HEADS * VT_ROWS, SEQ), BF16),
            pltpu.VMEM((SEQ, LANES), jnp.int32),
            pltpu.VMEM((ATT_TILES, DA_HEADS, SEQ, tq), F32),
            pltpu.SMEM((2 * ATT_TILES,), jnp.int32),
            pltpu.VMEM((nq * NEAR_SLOTS, DA_HEADS, sub, tq), F32),
            pltpu.SMEM((nq * NEAR_SLOTS,), jnp.int32),
        ],
    )
    return pl.pallas_call(
        _attn_kernel,
        grid_spec=grid_spec,
        out_shape=jax.ShapeDtypeStruct((N_TOK, V_WIDTH), BF16),
        compiler_params=_params(("arbitrary", "arbitrary")),
        name="diff_attn",
    )(far_left, far_right, same_as_prev,
      q, k, v, posk, posq, tbl, far, lam_rows, subln_g)


def _mix_ffn_kernel(x_ref, ya_ref, o_ref, ga_ref, gb_ref, wa32_ref, wo32_ref,
                    g_ref, wg32_ref, wu32_ref, wd32_ref, fn_ref, out_ref,
                    wa_ref, wo_ref, wg_ref, wu_ref, wd_ref):
    step = pl.program_id(0)

    @pl.when(step < CAST_STEPS)
    def _():
        _cast_slabs(step, ((wa32_ref, wa_ref), (wo32_ref, wo_ref), (wg32_ref, wg_ref),
                           (wu32_ref, wu_ref), (wd32_ref, wd_ref)))

    def mix(rows, yb):
        merged = (ga_ref[rows, :].astype(F32) * ya_ref[rows, :].astype(F32)
                  + gb_ref[rows, :].astype(F32) * yb).astype(BF16)
        return x_ref[rows, :] + jnp.dot(merged, wo_ref[...], preferred_element_type=F32)

    @pl.when(step >= CAST_STEPS)
    def _():
        yb = [jnp.dot(o_ref[rows, :], wa_ref[...], preferred_element_type=F32) for rows in ROW_HALVES]
        x2 = [mix(rows, y) for rows, y in zip(ROW_HALVES, yb)]
        gu = [_gate_up(x, g_ref[...], wg_ref, wu_ref) for x in x2]
        for rows, x, (gate, up) in zip(ROW_HALVES, x2, gu):
            y = _down_residual(x, gate, up, wd_ref)
            out_ref[rows, :] = _rms(y, fn_ref[...], NORM_EPS)


def _mix_ffn(x, ya, o, ga, gb, w_ao, w_out, g, wg, wu, wd, fn):
    tm, row, slab = FFN_TM, _tile_spec, _slab_spec
    return pl.pallas_call(
        _mix_ffn_kernel,
        grid=(CAST_STEPS + N_TOK // tm,),
        in_specs=[row(D_MODEL), row(D_MODEL), row(V_WIDTH), row(D_MODEL), row(D_MODEL),
                  slab(w_ao), slab(w_out), _resident((1, D_MODEL)), slab(wg), slab(wu), slab(wd),
                  _resident((1, D_MODEL))],
        out_specs=row(D_MODEL),
        out_shape=jax.ShapeDtypeStruct((N_TOK, D_MODEL), F32),
        scratch_shapes=[pltpu.VMEM(w.shape, BF16) for w in (w_ao, w_out, wg, wu, wd)],
        compiler_params=_params(("arbitrary",)),
        name="mix_ffn",
    )(x, ya, o, ga, gb, w_ao, w_out, g, wg, wu, wd, fn)


def kernel(x, positions, rel_bias, ffn1_norm, ffn1_wg, ffn1_wu, ffn1_wd, mix_norm, w_in,
           lambda_q1, lambda_k1, lambda_q2, lambda_k2, subln_g, w_fourier_out, w_attn_out,
           w_out, ffn2_norm, ffn2_wg, ffn2_wu, ffn2_wd, final_norm):
    assert x.shape == (BATCH, SEQ, D_MODEL) and positions.shape == (BATCH, SEQ)
    w32 = lambda w: w.reshape(w.shape[1:]).astype(F32)
    row = lambda g: g.reshape(1, -1).astype(F32)
    seq_mat, chan_mat = _dft_constants()
    lam_rows = jnp.concatenate([lambda_q1, lambda_k1, lambda_q2, lambda_k2], axis=0).astype(F32)

    xt = x.reshape(N_TOK, D_MODEL)
    x1, uf, q, k, v, ga, gb = _ffn_proj(xt, row(ffn1_norm[0]), w32(ffn1_wg), w32(ffn1_wu), w32(ffn1_wd),
                                        row(mix_norm[0]), w32(w_in))
    ya = _fourier(uf, seq_mat, chan_mat, w32(w_fourier_out))
    o = _attention(q, k, v, positions.astype(jnp.int32), rel_bias, lam_rows, row(subln_g[0]))
    out = _mix_ffn(x1, ya, o, ga, gb, w32(w_attn_out), w32(w_out), row(ffn2_norm[0]),
                   w32(ffn2_wg), w32(ffn2_wu), w32(ffn2_wd), row(final_norm))
    return out.reshape(BATCH, SEQ, D_MODEL)
```

```python
import math

import numpy as np
import jax
import jax.numpy as jnp
from jax import lax
from jax.experimental import pallas as pl
from jax.experimental.pallas import tpu as pltpu

D_MODEL = 1024
BATCH = 8
SEQ = 2048
D_FF = 2816
F_GROUPS = 4
F_GROUP_CH = 128
F_WIDTH = F_GROUPS * F_GROUP_CH
DA_HEADS = 4
DA_HEAD_DIM = 64
DA_V_DIM = 2 * DA_HEAD_DIM
QK_WIDTH = DA_HEADS * 2 * DA_HEAD_DIM
V_WIDTH = DA_HEADS * DA_V_DIM
IN_WIDTH = F_WIDTH + 2 * QK_WIDTH + V_WIDTH + 2 * D_MODEL
REL_BUCKETS = 32
REL_MAX_DIST = 128
NORM_EPS = 1e-6
SUBLN_EPS = 1e-5
LAMBDA_INIT = 0.8 - 0.6 * math.exp(-0.3 * 0)

N_TOK = BATCH * SEQ
LANES = 128
SUBLANES = 8
BF16_ROWS = 16
VMEM_LIMIT = 56 * 1024 * 1024

FFN_TM = 512
SEQ_HALF_ROWS = SEQ // 2 + BF16_ROWS
ATT_TQ = 256
ATT_TILES = 2
ATT_TK = 1024
ATT_SUB = 128
ATT_QK = 256
VT_ROWS = DA_V_DIM + BF16_ROWS
CAST_STEPS = 16
CAST_SLABS = 32
NEAR_SLOTS = (ATT_TQ + 2 * REL_MAX_DIST) // ATT_SUB

assert N_TOK % FFN_TM == 0 and FFN_TM % (2 * BF16_ROWS) == 0
assert SEQ % (ATT_TQ * ATT_TILES) == 0 and SEQ % ATT_TK == 0 and ATT_TK % ATT_QK == 0
assert ATT_QK % ATT_SUB == 0 and ATT_TQ % LANES == 0 and ATT_SUB == LANES
assert (BATCH * SEQ // (ATT_TQ * ATT_TILES)) % CAST_SLABS == 0
assert D_MODEL % (CAST_STEPS * BF16_ROWS) == 0 and D_FF % (CAST_STEPS * BF16_ROWS) == 0

BF16 = jnp.bfloat16
F32 = jnp.float32


def _rms(x, g, eps):
    return x * lax.rsqrt(jnp.mean(x * x, axis=-1, keepdims=True) + eps) * g


def _resident(shape):
    return pl.BlockSpec(shape, lambda *_: (0,) * len(shape), pipeline_mode=pl.Buffered(1))


def _params(sem):
    return pltpu.CompilerParams(dimension_semantics=sem, vmem_limit_bytes=VMEM_LIMIT)


ROW_HALVES = [slice(r * (FFN_TM // 2), (r + 1) * (FFN_TM // 2)) for r in range(2)]


def _gate_up(x, g, wg_ref, wu_ref):
    h = _rms(x, g, NORM_EPS).astype(BF16)
    return (jnp.dot(h, wg_ref[...], preferred_element_type=F32),
            jnp.dot(h, wu_ref[...], preferred_element_type=F32))


def _down_residual(x, gate, up, wd_ref):
    a = (gate * jax.nn.sigmoid(gate) * up).astype(BF16)
    return x + 0.5 * jnp.dot(a, wd_ref[...], preferred_element_type=F32)


def _ffn_specs():
    return [_resident((1, D_MODEL)), _resident((D_MODEL, D_FF)), _resident((D_MODEL, D_FF)),
            _resident((D_FF, D_MODEL))]


def _ffn_proj_kernel(x_ref, g_ref, wg32_ref, wu32_ref, wd32_ref, gm_ref, win32_ref,
                     x1_ref, uf_ref, q_ref, k_ref, vt_ref, ga_ref, gb_ref,
                     wg_ref, wu_ref, wd_ref, w_ref):
    step = pl.program_id(0)

    @pl.when(step < CAST_STEPS)
    def _():
        for src, dst in ((wg32_ref, wg_ref), (wu32_ref, wu_ref), (wd32_ref, wd_ref), (win32_ref, w_ref)):
            rows = src.shape[0]
            dst[pl.ds(pl.multiple_of(step * rows, rows), rows), :] = src[...].astype(BF16)

    def proj(rows, x1):
        x1_ref[rows, :] = x1
        h = _rms(x1, gm_ref[...], NORM_EPS).astype(BF16)
        p = jnp.dot(h, w_ref[...], preferred_element_type=F32)
        c = 0
        uf_ref[rows, :] = p[:, c:c + F_WIDTH].astype(BF16)
        c += F_WIDTH
        q_ref[rows, :] = (p[:, c:c + QK_WIDTH]
                          * (DA_HEAD_DIM ** -0.5 * math.log2(math.e))).astype(BF16)
        c += QK_WIDTH
        k_ref[rows, :] = p[:, c:c + QK_WIDTH].astype(BF16)
        c += QK_WIDTH
        for h in range(DA_HEADS):
            vh = p[:, c + h * DA_V_DIM:c + (h + 1) * DA_V_DIM]
            vt_ref[h * VT_ROWS:h * VT_ROWS + DA_V_DIM, rows] = vh.T.astype(BF16)
            vt_ref[h * VT_ROWS + DA_V_DIM:(h + 1) * VT_ROWS, rows] = jnp.ones(
                (VT_ROWS - DA_V_DIM, vh.shape[0]), BF16)
        c += V_WIDTH
        ga_ref[rows, :] = jax.nn.sigmoid(p[:, c:c + D_MODEL]).astype(BF16)
        c += D_MODEL
        gb_ref[rows, :] = jax.nn.sigmoid(p[:, c:c + D_MODEL]).astype(BF16)

    @pl.when(step >= CAST_STEPS)
    def _():
        xs = [x_ref[rows, :] for rows in ROW_HALVES]
        gu = [_gate_up(x, g_ref[...], wg_ref, wu_ref) for x in xs]
        x1 = [_down_residual(x, gate, up, wd_ref) for x, (gate, up) in zip(xs, gu)]
        for rows, x in zip(ROW_HALVES, x1):
            proj(rows, x)


def _ffn_proj(x, g, wg, wu, wd, g_mix, w_in):
    tm = FFN_TM
    row = lambda w: pl.BlockSpec((tm, w), lambda s: (jnp.maximum(s - CAST_STEPS, 0), 0))
    slab = lambda w: pl.BlockSpec((w.shape[0] // CAST_STEPS, w.shape[1]),
                                  lambda s: (jnp.minimum(s, CAST_STEPS - 1), 0))
    col = pl.BlockSpec((DA_HEADS * VT_ROWS, tm), lambda s: (0, jnp.maximum(s - CAST_STEPS, 0)))
    out = lambda w: jax.ShapeDtypeStruct((N_TOK, w), BF16)
    return pl.pallas_call(
        _ffn_proj_kernel,
        grid=(CAST_STEPS + N_TOK // tm,),
        in_specs=[row(D_MODEL), _resident((1, D_MODEL)), slab(wg), slab(wu), slab(wd),
                  _resident((1, D_MODEL)), slab(w_in)],
        out_specs=[row(D_MODEL), row(F_WIDTH), row(QK_WIDTH), row(QK_WIDTH), col,
                   row(D_MODEL), row(D_MODEL)],
        out_shape=[jax.ShapeDtypeStruct((N_TOK, D_MODEL), F32), out(F_WIDTH), out(QK_WIDTH),
                   out(QK_WIDTH), jax.ShapeDtypeStruct((DA_HEADS * VT_ROWS, N_TOK), BF16),
                   out(D_MODEL), out(D_MODEL)],
        scratch_shapes=[pltpu.VMEM(w.shape, BF16) for w in (wg, wu, wd, w_in)],
        compiler_params=_params(("arbitrary",)),
        name="ffn_proj",
    )(x, g, wg, wu, wd, g_mix, w_in)


def _dft_constants():
    k = np.arange(SEQ_HALF_ROWS, dtype=np.int64)[:, None]
    n = np.arange(SEQ, dtype=np.int64)[None, :]
    phase = (k * n) % SEQ
    live = (k <= SEQ // 2)
    cos_h = np.where(live, np.cos(2.0 * np.pi * phase / SEQ), 0.0)
    sin_h = np.where(live & (phase % (SEQ // 2) != 0), np.sin(2.0 * np.pi * phase / SEQ), 0.0)
    seq_mat = np.concatenate([cos_h, sin_h], axis=1)
    c = np.arange(F_GROUP_CH, dtype=np.int64)
    angc = 2.0 * np.pi * ((c[:, None] * c[None, :]) % F_GROUP_CH) / F_GROUP_CH
    scale = 1.0 / math.sqrt(SEQ * F_GROUP_CH)
    eye = np.eye(F_GROUPS)
    chan = np.concatenate([np.kron(eye, np.cos(angc)), -np.kron(eye, np.sin(angc))], axis=1) * scale
    return jnp.asarray(seq_mat, dtype=BF16), jnp.asarray(chan, dtype=BF16)


def _flip_rows(x):
    rows, cols = x.shape
    assert rows % SUBLANES == 0
    idx = SUBLANES - 1 - lax.broadcasted_iota(jnp.int32, (SUBLANES, cols), 0)
    groups = [jnp.take_along_axis(x[g * SUBLANES:(g + 1) * SUBLANES, :], idx, axis=0)
              for g in reversed(range(rows // SUBLANES))]
    return jnp.concatenate(groups, axis=0)


def _fourier_kernel(u_ref, seq_ref, chan_ref, w32_ref, o_ref, z_ref, d_ref, w_ref):
    @pl.when(pl.program_id(0) == 0)
    def _():
        w_ref[...] = w32_ref[...].astype(BF16)

    u = u_ref[...]
    for g in range(F_GROUPS):
        cols = slice(g * F_GROUP_CH, (g + 1) * F_GROUP_CH)
        sin_cols = slice(F_WIDTH + g * F_GROUP_CH, F_WIDTH + (g + 1) * F_GROUP_CH)
        z_ref[0:SEQ, cols] = jnp.dot(u[:, cols], chan_ref[cols, cols],
                                     preferred_element_type=F32).astype(BF16)
        z_ref[SEQ:2 * SEQ, cols] = jnp.dot(u[:, cols], chan_ref[cols, sin_cols],
                                           preferred_element_type=F32).astype(BF16)
    half = SEQ // 2
    pc = jnp.dot(seq_ref[:, 0:SEQ], z_ref[0:SEQ, :], preferred_element_type=F32)
    ps = jnp.dot(seq_ref[:, SEQ:2 * SEQ], z_ref[SEQ:2 * SEQ, :], preferred_element_type=F32)
    top = (pc + ps)[0:half, :]
    d_ref[...] = pc - ps
    bottom = _flip_rows(d_ref[1:half + 1, :])
    o_ref[0:half, :] = jnp.dot(top.astype(BF16), w_ref[...],
                               preferred_element_type=F32).astype(BF16)
    o_ref[half:SEQ, :] = jnp.dot(bottom.astype(BF16), w_ref[...],
                                 preferred_element_type=F32).astype(BF16)


def _fourier(uf, seq_mat, chan_mat, w_fo):
    return pl.pallas_call(
        _fourier_kernel,
        grid=(BATCH,),
        in_specs=[
            pl.BlockSpec((SEQ, F_WIDTH), lambda b: (b, 0)),
            _resident((SEQ_HALF_ROWS, 2 * SEQ)),
            _resident((F_WIDTH, 2 * F_WIDTH)),
            _resident((F_WIDTH, D_MODEL)),
        ],
        out_specs=pl.BlockSpec((SEQ, D_MODEL), lambda b: (b, 0)),
        out_shape=jax.ShapeDtypeStruct((N_TOK, D_MODEL), BF16),
        scratch_shapes=[pltpu.VMEM((2 * SEQ, F_WIDTH), BF16),
                        pltpu.VMEM((SEQ_HALF_ROWS, F_WIDTH), F32),
                        pltpu.VMEM((F_WIDTH, D_MODEL), BF16)],
        compiler_params=_params(("arbitrary",)),
        name="fourier",
    )(uf, seq_mat, chan_mat, w_fo)


def _bucket(rel):
    nb = REL_BUCKETS // 2
    max_exact = nb // 2
    n = jnp.minimum(jnp.abs(rel), REL_MAX_DIST)
    nf = n.astype(F32)
    expo = lax.shift_right_logical(lax.bitcast_convert_type(nf * nf, jnp.int32), 23) - 127
    large = jnp.minimum(expo + 2, nb - 1)
    return jnp.where(rel > 0, nb, 0) + jnp.where(n < max_exact, n, large)


def _attn_kernel(left_ref, right_ref, same_ref,
                 q_ref, k_ref, vt_ref, posk_ref, posq_ref, tbl_ref, far_ref, lam_ref, sg_ref,
                 w32_0, w32_1, w32_2, w32_3, w32_4,
                 o_ref, w16_0, w16_1, w16_2, w16_3, w16_4,
                 poskb_ref, bias_ref, fill_ref, cache_ref, cblk_ref):
    b = pl.program_id(0)
    i = pl.program_id(1)
    tq, tk, sub = ATT_TQ, ATT_TK, ATT_SUB
    nblk = SEQ // sub

    @pl.when(i == 0)
    def _():
        for c in range(SEQ // LANES):
            row = posk_ref[0, :, c * LANES:(c + 1) * LANES]
            halves = [jnp.broadcast_to(part.astype(F32), (LANES, LANES)).T.astype(jnp.int32)
                      for part in (lax.shift_right_arithmetic(row, 16), row & 0xFFFF)]
            poskb_ref[c * LANES:(c + 1) * LANES, :] = lax.shift_left(halves[0], 16) | halves[1]

    @pl.when(jnp.logical_and(b == 0, i == 0))
    def _():
        for e in range(2 * ATT_TILES):
            fill_ref[e] = 0

    lq1, lk1, lq2, lk2 = (lam_ref[r:r + 1, :] for r in range(4))
    lam = (jnp.exp(jnp.sum(lq1 * lk1, axis=-1, keepdims=True))
           - jnp.exp(jnp.sum(lq2 * lk2, axis=-1, keepdims=True)) + LAMBDA_INIT)

    lane = lax.broadcasted_iota(jnp.int32, (tq, DA_V_DIM), 1)
    first_map = lane < DA_HEAD_DIM
    nt = (((1,), (1,)), ((), ()))

    qz = {}
    for j in range(ATT_TILES):
        for h in range(DA_HEADS):
            qh = q_ref[j * tq:(j + 1) * tq, h * DA_V_DIM:(h + 1) * DA_V_DIM]
            zero = jnp.zeros_like(qh)
            qz[j, h] = jnp.concatenate([jnp.where(first_map, qh, zero),
                                        jnp.where(first_map, zero, qh)], axis=0)
    state = {}

    @pl.when(same_ref[b] == 0)
    def _():
        for e in range(ATT_TILES * NEAR_SLOTS):
            cblk_ref[i * ATT_TILES * NEAR_SLOTS + e] = -1

    for j in range(ATT_TILES):
        tile = i * ATT_TILES + j
        posq = posq_ref[j]
        far_left = left_ref[b, tile]
        far_right = right_ref[b, tile]
        near_mask = jnp.bitwise_and(jnp.bitwise_not(jnp.bitwise_or(far_left, far_right)),
                                    (1 << nblk) - 1)
        fill_mask = jnp.bitwise_or(
            jnp.bitwise_and(far_left, jnp.bitwise_not(fill_ref[2 * j])),
            jnp.bitwise_and(far_right, jnp.bitwise_not(fill_ref[2 * j + 1])))
        work_mask = jnp.bitwise_or(near_mask, fill_mask)
        fill_ref[2 * j] = far_left
        fill_ref[2 * j + 1] = far_right
        slot = jnp.int32(0)
        for blk in range(nblk):
            bit = 1 << blk
            rows = slice(blk * sub, (blk + 1) * sub)
            near = jnp.bitwise_and(near_mask, bit) != 0

            @pl.when(jnp.bitwise_and(work_mask, bit) != 0)
            def _(rows=rows, blk=blk, bit=bit, j=j, posq=posq, near=near, slot=slot, tile=tile,
                  fill_mask=fill_mask, far_right=far_right):
                cacheable = slot < NEAR_SLOTS
                entry = tile * NEAR_SLOTS + jnp.minimum(slot, NEAR_SLOTS - 1)
                hit = jnp.logical_and(jnp.logical_and(near, cacheable), cblk_ref[entry] == blk)

                @pl.when(jnp.bitwise_and(fill_mask, bit) != 0)
                def _():
                    to_right = jnp.bitwise_and(far_right, bit) != 0
                    for h in range(DA_HEADS):
                        cst = jnp.where(to_right, far_ref[1, h], far_ref[0, h])
                        bias_ref[j, h, rows, :] = jnp.full((sub, tq), cst, F32)

                @pl.when(hit)
                def _():
                    bias_ref[j, :, rows, :] = cache_ref[entry]

                @pl.when(jnp.logical_and(near, jnp.logical_not(hit)))
                def _():
                    tbls = [jnp.broadcast_to(tbl_ref[h:h + 1, :], (sub, LANES))
                            for h in range(DA_HEADS)]
                    for t in range(tq // LANES):
                        ln = slice(t * LANES, (t + 1) * LANES)
                        rel = poskb_ref[rows, :] - posq[:, ln]
                        bucket = _bucket(rel)
                        for h in range(DA_HEADS):
                            bias_ref[j, h, rows, ln] = jnp.take_along_axis(
                                tbls[h], bucket, axis=1, mode="promise_in_bounds")

                    @pl.when(cacheable)
                    def _():
                        cache_ref[entry] = bias_ref[j, :, rows, :]
                        cblk_ref[entry] = blk

            slot = slot + near.astype(jnp.int32)

    def scores(j, c, h):
        parts, m_c = [], None
        for u in range(tk // ATT_QK):
            rows = slice(c * tk + u * ATT_QK, c * tk + (u + 1) * ATT_QK)
            bias = bias_ref[j, h, rows, :]
            s = (lax.dot_general(k_ref[rows, h * DA_V_DIM:(h + 1) * DA_V_DIM], qz[j, h], nt,
                                 preferred_element_type=F32)
                 + jnp.concatenate([bias, bias], axis=1))
            m_u = jnp.max(s, axis=0, keepdims=True)
            m_c = m_u if m_c is None else jnp.maximum(m_c, m_u)
            parts.append(s)
        return parts, m_c

    for src, dst in ((w32_0, w16_0), (w32_1, w16_1), (w32_2, w16_2), (w32_3, w16_3), (w32_4, w16_4)):
        dst[...] = src[...].astype(BF16)

    items = [(j, c, h) for j in range(ATT_TILES) for c in range(SEQ // tk) for h in range(DA_HEADS)]
    ahead = scores(*items[0])
    for n, (j, c, h) in enumerate(items):
        s_parts, m_c = ahead
        if n + 1 < len(items):
            ahead = scores(*items[n + 1])
        vt = vt_ref[h * VT_ROWS:(h + 1) * VT_ROWS, c * tk:(c + 1) * tk]
        if c == 0:
            m_new = m_c
        else:
            m_old, acc_old = state[j, h]
            m_new = jnp.maximum(m_old, m_c)
            alpha = jnp.exp2(m_old - m_new)
        p = jnp.concatenate([jnp.exp2(s - m_new).astype(BF16) for s in s_parts], axis=0)
        acc_new = jnp.dot(vt, p, preferred_element_type=F32)
        if c > 0:
            acc_new = alpha * acc_old + acc_new
        state[j, h] = (m_new, acc_new)
        if c == SEQ // tk - 1:
            l_fin = acc_new[DA_V_DIM:DA_V_DIM + 1, :]
            acc = acc_new[:DA_V_DIM, :]
            r1 = 1.0 / l_fin[:, :tq]
            r2 = lam / l_fin[:, tq:]
            o = (acc[:, :tq] * r1 - acc[:, tq:] * r2).T
            o = _rms(o, sg_ref[...], SUBLN_EPS) * (1.0 - LAMBDA_INIT)
            o_ref[j * tq:(j + 1) * tq, h * DA_V_DIM:(h + 1) * DA_V_DIM] = o.astype(BF16)


def _attention(q, k, vt, positions, rel_bias, lam_rows, subln_g, later_weights):
    tq, sub = ATT_TQ, ATT_SUB
    nq = SEQ // tq
    ns = nq // ATT_TILES
    nsub = SEQ // sub
    steps_per_slab = BATCH * ns // CAST_SLABS
    slab_specs = [pl.BlockSpec((w.shape[0] // CAST_SLABS, w.shape[1]),
                               lambda b, i, *_: ((b * ns + i) // steps_per_slab, 0))
                  for w in later_weights]
    log2e = math.log2(math.e)
    tbl_t = rel_bias.T.astype(F32) * log2e
    tbl = jnp.zeros((SUBLANES, LANES), F32).at[:DA_HEADS, :REL_BUCKETS].set(tbl_t)
    nb = REL_BUCKETS // 2
    far = jnp.stack([tbl_t[:, nb - 1], tbl_t[:, 2 * nb - 1]])
    pk = positions.reshape(BATCH, 1, nsub, sub)
    pq = positions.reshape(BATCH, nq, 1, tq)
    bits = jnp.left_shift(1, jnp.arange(nsub, dtype=jnp.int32))
    far_right = jnp.sum(jnp.where(pk.min(-1) - pq.max(-1) >= REL_MAX_DIST, bits, 0), axis=-1)
    far_left = jnp.sum(jnp.where(pq.min(-1) - pk.max(-1) >= REL_MAX_DIST, bits, 0), axis=-1)
    posk = positions.reshape(BATCH, 1, SEQ)
    posq = positions.reshape(BATCH * nq, 1, tq)
    same_as_prev = jnp.concatenate([
        jnp.zeros((1,), jnp.int32),
        jnp.all(positions[1:] == positions[:-1], axis=1).astype(jnp.int32)])
    smem = pl.BlockSpec(memory_space=pltpu.SMEM)
    grid_spec = pltpu.PrefetchScalarGridSpec(
        num_scalar_prefetch=3,
        grid=(BATCH, ns),
        in_specs=[
            pl.BlockSpec((ATT_TILES * tq, QK_WIDTH), lambda b, i, *_: (b * ns + i, 0)),
            pl.BlockSpec((SEQ, QK_WIDTH), lambda b, i, *_: (b, 0)),
            pl.BlockSpec((DA_HEADS * VT_ROWS, SEQ), lambda b, i, *_: (0, b)),
            pl.BlockSpec((1, 1, SEQ), lambda b, i, *_: (b, 0, 0)),
            pl.BlockSpec((ATT_TILES, 1, tq), lambda b, i, *_: (b * ns + i, 0, 0)),
            _resident((SUBLANES, LANES)),
            smem,
            _resident((4, DA_HEAD_DIM)),
            _resident((1, DA_V_DIM)),
        ] + slab_specs,
        out_specs=[pl.BlockSpec((ATT_TILES * tq, V_WIDTH), lambda b, i, *_: (b * ns + i, 0))] + slab_specs,
        scratch_shapes=[
            pltpu.VMEM((SEQ, LANES), jnp.int32),
            pltpu.VMEM((ATT_TILES, DA_HEADS, SEQ, tq), F32),
            pltpu.SMEM((2 * ATT_TILES,), jnp.int32),
            pltpu.VMEM((nq * NEAR_SLOTS, DA_HEADS, sub, tq), F32),
            pltpu.SMEM((nq * NEAR_SLOTS,), jnp.int32),
        ],
    )
    return pl.pallas_call(
        _attn_kernel,
        grid_spec=grid_spec,
        out_shape=[jax.ShapeDtypeStruct((N_TOK, V_WIDTH), BF16)]
                  + [jax.ShapeDtypeStruct(w.shape, BF16) for w in later_weights],
        compiler_params=_params(("arbitrary", "arbitrary")),
        name="diff_attn",
    )(far_left, far_right, same_as_prev,
      q, k, vt, posk, posq, tbl, far, lam_rows, subln_g, *later_weights)


def _mix_ffn_kernel(x_ref, ya_ref, o_ref, ga_ref, gb_ref, wa_ref, wo_ref,
                    g_ref, wg_ref, wu_ref, wd_ref, fn_ref, out_ref):
    def mix(rows, yb):
        merged = (ga_ref[rows, :].astype(F32) * ya_ref[rows, :].astype(F32)
                  + gb_ref[rows, :].astype(F32) * yb).astype(BF16)
        return x_ref[rows, :] + jnp.dot(merged, wo_ref[...], preferred_element_type=F32)

    yb = [jnp.dot(o_ref[rows, :], wa_ref[...], preferred_element_type=F32) for rows in ROW_HALVES]
    x2 = [mix(rows, y) for rows, y in zip(ROW_HALVES, yb)]
    gu = [_gate_up(x, g_ref[...], wg_ref, wu_ref) for x in x2]
    for rows, x, (gate, up) in zip(ROW_HALVES, x2, gu):
        y = _down_residual(x, gate, up, wd_ref)
        out_ref[rows, :] = _rms(y, fn_ref[...], NORM_EPS)


def _mix_ffn(x, ya, o, ga, gb, w_ao, w_out, g, wg, wu, wd, fn):
    tm = FFN_TM
    row = lambda w: pl.BlockSpec((tm, w), lambda i: (i, 0))
    return pl.pallas_call(
        _mix_ffn_kernel,
        grid=(N_TOK // tm,),
        in_specs=[row(D_MODEL), row(D_MODEL), row(V_WIDTH), row(D_MODEL), row(D_MODEL),
                  _resident((V_WIDTH, D_MODEL)), _resident((D_MODEL, D_MODEL))]
                 + _ffn_specs() + [_resident((1, D_MODEL))],
        out_specs=row(D_MODEL),
        out_shape=jax.ShapeDtypeStruct((N_TOK, D_MODEL), F32),
        compiler_params=_params(("arbitrary",)),
        name="mix_ffn",
    )(x, ya, o, ga, gb, w_ao, w_out, g, wg, wu, wd, fn)


def kernel(x, positions, rel_bias, ffn1_norm, ffn1_wg, ffn1_wu, ffn1_wd, mix_norm, w_in,
           lambda_q1, lambda_k1, lambda_q2, lambda_k2, subln_g, w_fourier_out, w_attn_out,
           w_out, ffn2_norm, ffn2_wg, ffn2_wu, ffn2_wd, final_norm):
    assert x.shape == (BATCH, SEQ, D_MODEL) and positions.shape == (BATCH, SEQ)
    w32 = lambda w: w.reshape(w.shape[1:]).astype(F32)
    row = lambda g: g.reshape(1, -1).astype(F32)
    seq_mat, chan_mat = _dft_constants()
    lam_rows = jnp.concatenate([lambda_q1, lambda_k1, lambda_q2, lambda_k2], axis=0).astype(F32)

    xt = x.reshape(N_TOK, D_MODEL)
    x1, uf, q, k, vt, ga, gb = _ffn_proj(xt, row(ffn1_norm[0]), w32(ffn1_wg), w32(ffn1_wu), w32(ffn1_wd),
                                        row(mix_norm[0]), w32(w_in))
    ya = _fourier(uf, seq_mat, chan_mat, w32(w_fourier_out))
    later = [w32(w) for w in (ffn2_wg, ffn2_wu, ffn2_wd, w_out, w_attn_out)]
    o, wg2, wu2, wd2, wo, wa = _attention(q, k, vt, positions.astype(jnp.int32), rel_bias, lam_rows,
                                          row(subln_g[0]), later)
    out = _mix_ffn(x1, ya, o, ga, gb, wa, wo, row(ffn2_norm[0]), wg2, wu2, wd2, row(final_norm))
    return out.reshape(BATCH, SEQ, D_MODEL)
```

```python
import math

import numpy as np
import jax
import jax.numpy as jnp
from jax import lax
from jax.experimental import pallas as pl
from jax.experimental.pallas import tpu as pltpu

D_MODEL = 1024
BATCH = 8
SEQ = 2048
D_FF = 2816
F_GROUPS = 4
F_GROUP_CH = 128
F_WIDTH = F_GROUPS * F_GROUP_CH
DA_HEADS = 4
DA_HEAD_DIM = 64
DA_V_DIM = 2 * DA_HEAD_DIM
QK_WIDTH = DA_HEADS * 2 * DA_HEAD_DIM
V_WIDTH = DA_HEADS * DA_V_DIM
IN_WIDTH = F_WIDTH + 2 * QK_WIDTH + V_WIDTH + 2 * D_MODEL
REL_BUCKETS = 32
REL_MAX_DIST = 128
NORM_EPS = 1e-6
SUBLN_EPS = 1e-5
LAMBDA_INIT = 0.8 - 0.6 * math.exp(-0.3 * 0)

N_TOK = BATCH * SEQ
LANES = 128
SUBLANES = 8
BF16_ROWS = 16
VMEM_LIMIT = 56 * 1024 * 1024

FFN_TM = 512
SEQ_HALF_ROWS = SEQ // 2 + BF16_ROWS
ATT_TQ = 256
ATT_TILES = 2
ATT_TK = 1024
ATT_SUB = 128
ATT_QK = 256
VT_ROWS = DA_V_DIM + BF16_ROWS
CAST_STEPS = 16
CAST_SLABS = 32
NEAR_SLOTS = (ATT_TQ + 2 * REL_MAX_DIST) // ATT_SUB

assert N_TOK % FFN_TM == 0 and FFN_TM % (2 * BF16_ROWS) == 0
assert SEQ % (ATT_TQ * ATT_TILES) == 0 and SEQ % ATT_TK == 0 and ATT_TK % ATT_QK == 0
assert ATT_QK % ATT_SUB == 0 and ATT_TQ % LANES == 0 and ATT_SUB == LANES
assert (BATCH * SEQ // (ATT_TQ * ATT_TILES)) % CAST_SLABS == 0
assert D_MODEL % (CAST_STEPS * BF16_ROWS) == 0 and D_FF % (CAST_STEPS * BF16_ROWS) == 0

BF16 = jnp.bfloat16
F32 = jnp.float32


def _rms(x, g, eps):
    return x * lax.rsqrt(jnp.mean(x * x, axis=-1, keepdims=True) + eps) * g


def _resident(shape):
    return pl.BlockSpec(shape, lambda *_: (0,) * len(shape), pipeline_mode=pl.Buffered(1))


def _params(sem):
    return pltpu.CompilerParams(dimension_semantics=sem, vmem_limit_bytes=VMEM_LIMIT)


ROW_HALVES = [slice(r * (FFN_TM // 2), (r + 1) * (FFN_TM // 2)) for r in range(2)]


def _gate_up(x, g, wg_ref, wu_ref):
    h = _rms(x, g, NORM_EPS).astype(BF16)
    return (jnp.dot(h, wg_ref[...], preferred_element_type=F32),
            jnp.dot(h, wu_ref[...], preferred_element_type=F32))


def _down_residual(x, gate, up, wd_ref):
    a = (gate * jax.nn.sigmoid(gate) * up).astype(BF16)
    return x + 0.5 * jnp.dot(a, wd_ref[...], preferred_element_type=F32)


def _ffn_specs():
    return [_resident((1, D_MODEL)), _resident((D_MODEL, D_FF)), _resident((D_MODEL, D_FF)),
            _resident((D_FF, D_MODEL))]


def _ffn_proj_kernel(x_ref, g_ref, wg32_ref, wu32_ref, wd32_ref, gm_ref, win32_ref,
                     x1_ref, uf_ref, q_ref, k_ref, vt_ref, ga_ref, gb_ref,
                     wg_ref, wu_ref, wd_ref, w_ref):
    step = pl.program_id(0)

    @pl.when(step < CAST_STEPS)
    def _():
        for src, dst in ((wg32_ref, wg_ref), (wu32_ref, wu_ref), (wd32_ref, wd_ref), (win32_ref, w_ref)):
            rows = src.shape[0]
            dst[pl.ds(pl.multiple_of(step * rows, rows), rows), :] = src[...].astype(BF16)

    def proj(rows, x1):
        x1_ref[rows, :] = x1
        h = _rms(x1, gm_ref[...], NORM_EPS).astype(BF16)
        p = jnp.dot(h, w_ref[...], preferred_element_type=F32)
        c = 0
        uf_ref[rows, :] = p[:, c:c + F_WIDTH].astype(BF16)
        c += F_WIDTH
        q_ref[rows, :] = (p[:, c:c + QK_WIDTH]
                          * (DA_HEAD_DIM ** -0.5 * math.log2(math.e))).astype(BF16)
        c += QK_WIDTH
        k_ref[rows, :] = p[:, c:c + QK_WIDTH].astype(BF16)
        c += QK_WIDTH
        for h in range(DA_HEADS):
            vh = p[:, c + h * DA_V_DIM:c + (h + 1) * DA_V_DIM]
            vt_ref[h * VT_ROWS:h * VT_ROWS + DA_V_DIM, rows] = vh.T.astype(BF16)
            vt_ref[h * VT_ROWS + DA_V_DIM:(h + 1) * VT_ROWS, rows] = jnp.ones(
                (VT_ROWS - DA_V_DIM, vh.shape[0]), BF16)
        c += V_WIDTH
        ga_ref[rows, :] = jax.nn.sigmoid(p[:, c:c + D_MODEL]).astype(BF16)
        c += D_MODEL
        gb_ref[rows, :] = jax.nn.sigmoid(p[:, c:c + D_MODEL]).astype(BF16)

    @pl.when(step >= CAST_STEPS)
    def _():
        xs = [x_ref[rows, :] for rows in ROW_HALVES]
        gu = [_gate_up(x, g_ref[...], wg_ref, wu_ref) for x in xs]
        x1 = [_down_residual(x, gate, up, wd_ref) for x, (gate, up) in zip(xs, gu)]
        for rows, x in zip(ROW_HALVES, x1):
            proj(rows, x)


def _ffn_proj(x, g, wg, wu, wd, g_mix, w_in):
    tm = FFN_TM
    row = lambda w: pl.BlockSpec((tm, w), lambda s: (jnp.maximum(s - CAST_STEPS, 0), 0))
    slab = lambda w: pl.BlockSpec((w.shape[0] // CAST_STEPS, w.shape[1]),
                                  lambda s: (jnp.minimum(s, CAST_STEPS - 1), 0))
    col = pl.BlockSpec((DA_HEADS * VT_ROWS, tm), lambda s: (0, jnp.maximum(s - CAST_STEPS, 0)))
    out = lambda w: jax.ShapeDtypeStruct((N_TOK, w), BF16)
    return pl.pallas_call(
        _ffn_proj_kernel,
        grid=(CAST_STEPS + N_TOK // tm,),
        in_specs=[row(D_MODEL), _resident((1, D_MODEL)), slab(wg), slab(wu), slab(wd),
                  _resident((1, D_MODEL)), slab(w_in)],
        out_specs=[row(D_MODEL), row(F_WIDTH), row(QK_WIDTH), row(QK_WIDTH), col,
                   row(D_MODEL), row(D_MODEL)],
        out_shape=[jax.ShapeDtypeStruct((N_TOK, D_MODEL), F32), out(F_WIDTH), out(QK_WIDTH),
                   out(QK_WIDTH), jax.ShapeDtypeStruct((DA_HEADS * VT_ROWS, N_TOK), BF16),
                   out(D_MODEL), out(D_MODEL)],
        scratch_shapes=[pltpu.VMEM(w.shape, BF16) for w in (wg, wu, wd, w_in)],
        compiler_params=_params(("arbitrary",)),
        name="ffn_proj",
    )(x, g, wg, wu, wd, g_mix, w_in)


def _dft_constants():
    k = np.arange(SEQ_HALF_ROWS, dtype=np.int64)[:, None]
    n = np.arange(SEQ, dtype=np.int64)[None, :]
    phase = (k * n) % SEQ
    live = (k <= SEQ // 2)
    cos_h = np.where(live, np.cos(2.0 * np.pi * phase / SEQ), 0.0)
    sin_h = np.where(live & (phase % (SEQ // 2) != 0), np.sin(2.0 * np.pi * phase / SEQ), 0.0)
    seq_mat = np.concatenate([cos_h, sin_h], axis=1)
    c = np.arange(F_GROUP_CH, dtype=np.int64)
    angc = 2.0 * np.pi * ((c[:, None] * c[None, :]) % F_GROUP_CH) / F_GROUP_CH
    scale = 1.0 / math.sqrt(SEQ * F_GROUP_CH)
    eye = np.eye(F_GROUPS)
    chan = np.concatenate([np.kron(eye, np.cos(angc)), -np.kron(eye, np.sin(angc))], axis=1) * scale
    return jnp.asarray(seq_mat, dtype=BF16), jnp.asarray(chan, dtype=BF16)


def _flip_rows(x):
    rows, cols = x.shape
    assert rows % SUBLANES == 0
    idx = SUBLANES - 1 - lax.broadcasted_iota(jnp.int32, (SUBLANES, cols), 0)
    groups = [jnp.take_along_axis(x[g * SUBLANES:(g + 1) * SUBLANES, :], idx, axis=0)
              for g in reversed(range(rows // SUBLANES))]
    return jnp.concatenate(groups, axis=0)


def _fourier_kernel(u_ref, seq_ref, chan_ref, w32_ref, o_ref, z_ref, d_ref, w_ref):
    @pl.when(pl.program_id(0) == 0)
    def _():
        w_ref[...] = w32_ref[...].astype(BF16)

    u = u_ref[...]
    for g in range(F_GROUPS):
        cols = slice(g * F_GROUP_CH, (g + 1) * F_GROUP_CH)
        sin_cols = slice(F_WIDTH + g * F_GROUP_CH, F_WIDTH + (g + 1) * F_GROUP_CH)
        z_ref[0:SEQ, cols] = jnp.dot(u[:, cols], chan_ref[cols, cols],
                                     preferred_element_type=F32).astype(BF16)
        z_ref[SEQ:2 * SEQ, cols] = jnp.dot(u[:, cols], chan_ref[cols, sin_cols],
                                           preferred_element_type=F32).astype(BF16)
    half = SEQ // 2
    pc = jnp.dot(seq_ref[:, 0:SEQ], z_ref[0:SEQ, :], preferred_element_type=F32)
    ps = jnp.dot(seq_ref[:, SEQ:2 * SEQ], z_ref[SEQ:2 * SEQ, :], preferred_element_type=F32)
    top = (pc + ps)[0:half, :]
    d_ref[...] = pc - ps
    bottom = _flip_rows(d_ref[1:half + 1, :])
    o_ref[0:half, :] = jnp.dot(top.astype(BF16), w_ref[...],
                               preferred_element_type=F32).astype(BF16)
    o_ref[half:SEQ, :] = jnp.dot(bottom.astype(BF16), w_ref[...],
                                 preferred_element_type=F32).astype(BF16)


def _fourier(uf, seq_mat, chan_mat, w_fo):
    return pl.pallas_call(
        _fourier_kernel,
        grid=(BATCH,),
        in_specs=[
            pl.BlockSpec((SEQ, F_WIDTH), lambda b: (b, 0)),
            _resident((SEQ_HALF_ROWS, 2 * SEQ)),
            _resident((F_WIDTH, 2 * F_WIDTH)),
            _resident((F_WIDTH, D_MODEL)),
        ],
        out_specs=pl.BlockSpec((SEQ, D_MODEL), lambda b: (b, 0)),
        out_shape=jax.ShapeDtypeStruct((N_TOK, D_MODEL), BF16),
        scratch_shapes=[pltpu.VMEM((2 * SEQ, F_WIDTH), BF16),
                        pltpu.VMEM((SEQ_HALF_ROWS, F_WIDTH), F32),
                        pltpu.VMEM((F_WIDTH, D_MODEL), BF16)],
        compiler_params=_params(("arbitrary",)),
        name="fourier",
    )(uf, seq_mat, chan_mat, w_fo)


def _bucket(rel):
    nb = REL_BUCKETS // 2
    max_exact = nb // 2
    n = jnp.minimum(jnp.abs(rel), REL_MAX_DIST)
    nf = n.astype(F32)
    expo = lax.shift_right_logical(lax.bitcast_convert_type(nf * nf, jnp.int32), 23) - 127
    large = jnp.minimum(expo + 2, nb - 1)
    return jnp.where(rel > 0, nb, 0) + jnp.where(n < max_exact, n, large)


def _attn_kernel(left_ref, right_ref, same_ref,
                 q_ref, k_ref, vt_ref, posk_ref, posq_ref, tbl_ref, far_ref, lam_ref, sg_ref,
                 w32_0, w32_1, w32_2, w32_3, w32_4,
                 o_ref, w16_0, w16_1, w16_2, w16_3, w16_4,
                 poskb_ref, bias_ref, fill_ref, cache_ref, cblk_ref):
    b = pl.program_id(0)
    i = pl.program_id(1)
    tq, tk, sub = ATT_TQ, ATT_TK, ATT_SUB
    nblk = SEQ // sub

    @pl.when(i == 0)
    def _():
        for c in range(SEQ // LANES):
            row = posk_ref[0, :, c * LANES:(c + 1) * LANES]
            halves = [jnp.broadcast_to(part.astype(F32), (LANES, LANES)).T.astype(jnp.int32)
                      for part in (lax.shift_right_arithmetic(row, 16), row & 0xFFFF)]
            poskb_ref[c * LANES:(c + 1) * LANES, :] = lax.shift_left(halves[0], 16) | halves[1]

    @pl.when(jnp.logical_and(b == 0, i == 0))
    def _():
        for e in range(2 * ATT_TILES):
            fill_ref[e] = 0

    lq1, lk1, lq2, lk2 = (lam_ref[r:r + 1, :] for r in range(4))
    lam = (jnp.exp(jnp.sum(lq1 * lk1, axis=-1, keepdims=True))
           - jnp.exp(jnp.sum(lq2 * lk2, axis=-1, keepdims=True)) + LAMBDA_INIT)

    lane = lax.broadcasted_iota(jnp.int32, (tq, DA_V_DIM), 1)
    first_map = lane < DA_HEAD_DIM
    nt = (((1,), (1,)), ((), ()))

    qz = {}
    for j in range(ATT_TILES):
        for h in range(DA_HEADS):
            qh = q_ref[j * tq:(j + 1) * tq, h * DA_V_DIM:(h + 1) * DA_V_DIM]
            zero = jnp.zeros_like(qh)
            qz[j, h] = jnp.concatenate([jnp.where(first_map, qh, zero),
                                        jnp.where(first_map, zero, qh)], axis=0)
    state = {}

    @pl.when(same_ref[b] == 0)
    def _():
        for e in range(ATT_TILES * NEAR_SLOTS):
            cblk_ref[i * ATT_TILES * NEAR_SLOTS + e] = -1

    for j in range(ATT_TILES):
        tile = i * ATT_TILES + j
        posq = posq_ref[j]
        far_left = left_ref[b, tile]
        far_right = right_ref[b, tile]
        near_mask = jnp.bitwise_and(jnp.bitwise_not(jnp.bitwise_or(far_left, far_right)),
                                    (1 << nblk) - 1)
        fill_mask = jnp.bitwise_or(
            jnp.bitwise_and(far_left, jnp.bitwise_not(fill_ref[2 * j])),
            jnp.bitwise_and(far_right, jnp.bitwise_not(fill_ref[2 * j + 1])))
        work_mask = jnp.bitwise_or(near_mask, fill_mask)
        fill_ref[2 * j] = far_left
        fill_ref[2 * j + 1] = far_right
        slot = jnp.int32(0)
        for blk in range(nblk):
            bit = 1 << blk
            rows = slice(blk * sub, (blk + 1) * sub)
            near = jnp.bitwise_and(near_mask, bit) != 0

            @pl.when(jnp.bitwise_and(work_mask, bit) != 0)
            def _(rows=rows, blk=blk, bit=bit, j=j, posq=posq, near=near, slot=slot, tile=tile,
                  fill_mask=fill_mask, far_right=far_right):
                cacheable = slot < NEAR_SLOTS
                entry = tile * NEAR_SLOTS + jnp.minimum(slot, NEAR_SLOTS - 1)
                hit = jnp.logical_and(jnp.logical_and(near, cacheable), cblk_ref[entry] == blk)

                @pl.when(jnp.bitwise_and(fill_mask, bit) != 0)
                def _():
                    to_right = jnp.bitwise_and(far_right, bit) != 0
                    for h in range(DA_HEADS):
                        cst = jnp.where(to_right, far_ref[1, h], far_ref[0, h])
                        bias_ref[j, h, rows, :] = jnp.full((sub, tq), cst, F32)

                @pl.when(hit)
                def _():
                    bias_ref[j, :, rows, :] = cache_ref[entry]

                @pl.when(jnp.logical_and(near, jnp.logical_not(hit)))
                def _():
                    tbls = [jnp.broadcast_to(tbl_ref[h:h + 1, :], (sub, LANES))
                            for h in range(DA_HEADS)]
                    for t in range(tq // LANES):
                        ln = slice(t * LANES, (t + 1) * LANES)
                        rel = poskb_ref[rows, :] - posq[:, ln]
                        bucket = _bucket(rel)
                        for h in range(DA_HEADS):
                            bias_ref[j, h, rows, ln] = jnp.take_along_axis(
                                tbls[h], bucket, axis=1, mode="promise_in_bounds")

                    @pl.when(cacheable)
                    def _():
                        cache_ref[entry] = bias_ref[j, :, rows, :]
                        cblk_ref[entry] = blk

            slot = slot + near.astype(jnp.int32)

    def scores(j, c, h):
        parts, m_c = [], None
        for u in range(tk // ATT_QK):
            rows = slice(c * tk + u * ATT_QK, c * tk + (u + 1) * ATT_QK)
            bias = bias_ref[j, h, rows, :]
            s = (lax.dot_general(k_ref[rows, h * DA_V_DIM:(h + 1) * DA_V_DIM], qz[j, h], nt,
                                 preferred_element_type=F32)
                 + jnp.concatenate([bias, bias], axis=1))
            m_u = jnp.max(s, axis=0, keepdims=True)
            m_c = m_u if m_c is None else jnp.maximum(m_c, m_u)
            parts.append(s)
        return parts, m_c

    for src, dst in ((w32_0, w16_0), (w32_1, w16_1), (w32_2, w16_2), (w32_3, w16_3), (w32_4, w16_4)):
        dst[...] = src[...].astype(BF16)

    items = [(j, c, h) for j in range(ATT_TILES) for c in range(SEQ // tk) for h in range(DA_HEADS)]
    ahead = scores(*items[0])
    for n, (j, c, h) in enumerate(items):
        s_parts, m_c = ahead
        if n + 1 < len(items):
            ahead = scores(*items[n + 1])
        vt = vt_ref[h * VT_ROWS:(h + 1) * VT_ROWS, c * tk:(c + 1) * tk]
        if c == 0:
            m_new = m_c
        else:
            m_old, acc_old = state[j, h]
            m_new = jnp.maximum(m_old, m_c)
            alpha = jnp.exp2(m_old - m_new)
        p = jnp.concatenate([jnp.exp2(s - m_new).astype(BF16) for s in s_parts], axis=0)
        acc_new = jnp.dot(vt, p, preferred_element_type=F32)
        if c > 0:
            acc_new = alpha * acc_old + acc_new
        state[j, h] = (m_new, acc_new)
        if c == SEQ // tk - 1:
            l_fin = acc_new[DA_V_DIM:DA_V_DIM + 1, :]
            acc = acc_new[:DA_V_DIM, :]
            r1 = 1.0 / l_fin[:, :tq]
            r2 = lam / l_fin[:, tq:]
            ot = acc[:, :tq] * r1 - acc[:, tq:] * r2
            ms = jnp.mean(ot * ot, axis=0, keepdims=True)
            ot = ot * lax.rsqrt(ms + SUBLN_EPS) * (sg_ref[...] * (1.0 - LAMBDA_INIT))
            o_ref[h * DA_V_DIM:(h + 1) * DA_V_DIM, j * tq:(j + 1) * tq] = ot.astype(BF16)


def _attention(q, k, vt, positions, rel_bias, lam_rows, subln_g, later_weights):
    tq, sub = ATT_TQ, ATT_SUB
    nq = SEQ // tq
    ns = nq // ATT_TILES
    nsub = SEQ // sub
    steps_per_slab = BATCH * ns // CAST_SLABS
    slab_specs = [pl.BlockSpec((w.shape[0] // CAST_SLABS, w.shape[1]),
                               lambda b, i, *_: ((b * ns + i) // steps_per_slab, 0))
                  for w in later_weights]
    log2e = math.log2(math.e)
    tbl_t = rel_bias.T.astype(F32) * log2e
    tbl = jnp.zeros((SUBLANES, LANES), F32).at[:DA_HEADS, :REL_BUCKETS].set(tbl_t)
    nb = REL_BUCKETS // 2
    far = jnp.stack([tbl_t[:, nb - 1], tbl_t[:, 2 * nb - 1]])
    pk = positions.reshape(BATCH, 1, nsub, sub)
    pq = positions.reshape(BATCH, nq, 1, tq)
    bits = jnp.left_shift(1, jnp.arange(nsub, dtype=jnp.int32))
    far_right = jnp.sum(jnp.where(pk.min(-1) - pq.max(-1) >= REL_MAX_DIST, bits, 0), axis=-1)
    far_left = jnp.sum(jnp.where(pq.min(-1) - pk.max(-1) >= REL_MAX_DIST, bits, 0), axis=-1)
    posk = positions.reshape(BATCH, 1, SEQ)
    posq = positions.reshape(BATCH * nq, 1, tq)
    same_as_prev = jnp.concatenate([
        jnp.zeros((1,), jnp.int32),
        jnp.all(positions[1:] == positions[:-1], axis=1).astype(jnp.int32)])
    smem = pl.BlockSpec(memory_space=pltpu.SMEM)
    grid_spec = pltpu.PrefetchScalarGridSpec(
        num_scalar_prefetch=3,
        grid=(BATCH, ns),
        in_specs=[
            pl.BlockSpec((ATT_TILES * tq, QK_WIDTH), lambda b, i, *_: (b * ns + i, 0)),
            pl.BlockSpec((SEQ, QK_WIDTH), lambda b, i, *_: (b, 0)),
            pl.BlockSpec((DA_HEADS * VT_ROWS, SEQ), lambda b, i, *_: (0, b)),
            pl.BlockSpec((1, 1, SEQ), lambda b, i, *_: (b, 0, 0)),
            pl.BlockSpec((ATT_TILES, 1, tq), lambda b, i, *_: (b * ns + i, 0, 0)),
            _resident((SUBLANES, LANES)),
            smem,
            _resident((4, DA_HEAD_DIM)),
            _resident((DA_V_DIM, 1)),
        ] + slab_specs,
        out_specs=[pl.BlockSpec((V_WIDTH, ATT_TILES * tq), lambda b, i, *_: (0, b * ns + i))] + slab_specs,
        scratch_shapes=[
            pltpu.VMEM((SEQ, LANES), jnp.int32),
            pltpu.VMEM((ATT_TILES, DA_HEADS, SEQ, tq), F32),
            pltpu.SMEM((2 * ATT_TILES,), jnp.int32),
            pltpu.VMEM((nq * NEAR_SLOTS, DA_HEADS, sub, tq), F32),
            pltpu.SMEM((nq * NEAR_SLOTS,), jnp.int32),
        ],
    )
    return pl.pallas_call(
        _attn_kernel,
        grid_spec=grid_spec,
        out_shape=[jax.ShapeDtypeStruct((V_WIDTH, N_TOK), BF16)]
                  + [jax.ShapeDtypeStruct(w.shape, BF16) for w in later_weights],
        compiler_params=_params(("arbitrary", "arbitrary")),
        name="diff_attn",
    )(far_left, far_right, same_as_prev,
      q, k, vt, posk, posq, tbl, far, lam_rows, subln_g, *later_weights)


def _mix_ffn_kernel(x_ref, ya_ref, o_ref, ga_ref, gb_ref, wa_ref, wo_ref,
                    g_ref, wg_ref, wu_ref, wd_ref, fn_ref, out_ref):
    def mix(rows, yb):
        merged = (ga_ref[rows, :].astype(F32) * ya_ref[rows, :].astype(F32)
                  + gb_ref[rows, :].astype(F32) * yb).astype(BF16)
        return x_ref[rows, :] + jnp.dot(merged, wo_ref[...], preferred_element_type=F32)

    tn = (((0,), (0,)), ((), ()))
    yb = [lax.dot_general(o_ref[:, rows], wa_ref[...], tn, preferred_element_type=F32)
          for rows in ROW_HALVES]
    x2 = [mix(rows, y) for rows, y in zip(ROW_HALVES, yb)]
    gu = [_gate_up(x, g_ref[...], wg_ref, wu_ref) for x in x2]
    for rows, x, (gate, up) in zip(ROW_HALVES, x2, gu):
        y = _down_residual(x, gate, up, wd_ref)
        out_ref[rows, :] = _rms(y, fn_ref[...], NORM_EPS)


def _mix_ffn(x, ya, o, ga, gb, w_ao, w_out, g, wg, wu, wd, fn):
    tm = FFN_TM
    row = lambda w: pl.BlockSpec((tm, w), lambda i: (i, 0))
    return pl.pallas_call(
        _mix_ffn_kernel,
        grid=(N_TOK // tm,),
        in_specs=[row(D_MODEL), row(D_MODEL), pl.BlockSpec((V_WIDTH, tm), lambda i: (0, i)),
                  row(D_MODEL), row(D_MODEL),
                  _resident((V_WIDTH, D_MODEL)), _resident((D_MODEL, D_MODEL))]
                 + _ffn_specs() + [_resident((1, D_MODEL))],
        out_specs=row(D_MODEL),
        out_shape=jax.ShapeDtypeStruct((N_TOK, D_MODEL), F32),
        compiler_params=_params(("arbitrary",)),
        name="mix_ffn",
    )(x, ya, o, ga, gb, w_ao, w_out, g, wg, wu, wd, fn)


def kernel(x, positions, rel_bias, ffn1_norm, ffn1_wg, ffn1_wu, ffn1_wd, mix_norm, w_in,
           lambda_q1, lambda_k1, lambda_q2, lambda_k2, subln_g, w_fourier_out, w_attn_out,
           w_out, ffn2_norm, ffn2_wg, ffn2_wu, ffn2_wd, final_norm):
    assert x.shape == (BATCH, SEQ, D_MODEL) and positions.shape == (BATCH, SEQ)
    w32 = lambda w: w.reshape(w.shape[1:]).astype(F32)
    row = lambda g: g.reshape(1, -1).astype(F32)
    seq_mat, chan_mat = _dft_constants()
    lam_rows = jnp.concatenate([lambda_q1, lambda_k1, lambda_q2, lambda_k2], axis=0).astype(F32)

    xt = x.reshape(N_TOK, D_MODEL)
    x1, uf, q, k, vt, ga, gb = _ffn_proj(xt, row(ffn1_norm[0]), w32(ffn1_wg), w32(ffn1_wu), w32(ffn1_wd),
                                        row(mix_norm[0]), w32(w_in))
    ya = _fourier(uf, seq_mat, chan_mat, w32(w_fourier_out))
    later = [w32(w) for w in (ffn2_wg, ffn2_wu, ffn2_wd, w_out, w_attn_out)]
    o, wg2, wu2, wd2, wo, wa = _attention(q, k, vt, positions.astype(jnp.int32), rel_bias, lam_rows,
                                          subln_g[0].reshape(-1, 1).astype(F32), later)
    out = _mix_ffn(x1, ya, o, ga, gb, wa, wo, row(ffn2_norm[0]), wg2, wu2, wd2, row(final_norm))
    return out.reshape(BATCH, SEQ, D_MODEL)
```
